```python
import math, functools
import jax, jax.numpy as jnp
from jax import lax
import numpy as np

D_MODEL = 1024
BATCH = 16
SEQ = 256
DEPTH = 4
DEC_BATCH = 4
DEC_SEQ = 2048
PAST_LEN = 256

GRID_W = 64
HEAD_DIM = 64
D_ATT = D_MODEL // 2
D_CMLP = D_MODEL // 4
D_CONV = D_MODEL - D_ATT - D_CMLP
D_MIX = D_ATT + D_CMLP + D_CONV
H_ATT = D_ATT // HEAD_DIM
H_CMLP = D_CMLP // HEAD_DIM
CHUNK = 128
CONV_W = 3
NA_KH = 8
NA_KW = 16
NA_QB = 16
NA_KB = 32
D_IN = 3 * D_ATT + 2 * D_CMLP + 3 * D_CONV
D_FF = 2816
N_EXPERTS = 8
TOP_K = 2
N_MOD = 6
EPS = 1e-6
NEG_INF = -1e30
N_DENSE = (DEPTH + 1) // 2
N_MOE = DEPTH // 2

kernel_name = "hybrid_natten_gmlp_shortconv_dit_step"


def rms_norm(x, g):
    xf = x.astype(jnp.float32)
    y = xf * lax.rsqrt(jnp.mean(xf * xf, axis=-1, keepdims=True) + EPS)
    return (y * g.astype(jnp.float32)).astype(x.dtype)


def ada_modulation(cvec, w, b):
    mod = jax.nn.silu(cvec) @ w + b
    return mod.reshape(cvec.shape[0], N_MOD, 1, D_MODEL)


def split_projection(p):
    sizes = [D_ATT] * 3 + [D_CMLP] * 2 + [D_CONV] * 3
    offs = [int(o) for o in np.cumsum(sizes)[:-1]]
    return jnp.split(p, offs, axis=-1)


def context_attention(q, k, v):
    s = jnp.einsum('bqhd,bkhd->bhqk', q, k).astype(jnp.float32) * (HEAD_DIM ** -0.5)
    p = jax.nn.softmax(s, axis=-1).astype(v.dtype)
    return jnp.einsum('bhqk,bkhd->bqhd', p, v)


def neighbourhood_attention(q, k, v, k_ctx, v_ctx, rpb):
    bsz, n, nh, dh = q.shape
    rows = n // GRID_W
    kh = min(NA_KH, rows)
    ncb = GRID_W // NA_QB
    r = jnp.arange(rows)
    row_start = jnp.clip(r - kh // 2, 0, rows - kh)
    row_idx = row_start[:, None] + jnp.arange(kh)
    qcol = jnp.arange(ncb)[:, None] * NA_QB + jnp.arange(NA_QB)
    kb0 = jnp.clip(jnp.arange(ncb) * NA_QB - NA_KW // 2, 0, GRID_W - NA_KB)
    col_idx = kb0[:, None] + jnp.arange(NA_KB)
    col_start = jnp.clip(qcol - NA_KW // 2, 0, GRID_W - NA_KW)
    kcol = col_idx[:, None, :]
    valid = (kcol >= col_start[:, :, None]) & (kcol < col_start[:, :, None] + NA_KW)
    kg = k.reshape(bsz, rows, GRID_W, nh, dh)
    vg = v.reshape(bsz, rows, GRID_W, nh, dh)
    ri = row_idx[:, :, None, None]
    ci = col_idx[None, None, :, :]
    kl = kg[:, ri, ci]
    vl = vg[:, ri, ci]
    qb = q.reshape(bsz, rows, ncb, NA_QB, nh, dh)
    scale = HEAD_DIM ** -0.5
    s_loc = jnp.einsum('brnqhd,brinkhd->bhrnqik', qb, kl).astype(jnp.float32) * scale
    dr_i = (row_idx - r[:, None]) + (NA_KH - 1)
    dc_i = jnp.clip(kcol - qcol[:, :, None] + (NA_KW - 1), 0, 2 * NA_KW - 2)
    bias = rpb[:, dr_i[:, None, None, :, None], dc_i[None, :, :, None, :]]
    s_loc = s_loc + bias[None].astype(jnp.float32)
    s_loc = jnp.where(valid[None, None, None, :, :, None, :], s_loc, NEG_INF)
    s_ctx = jnp.einsum('brnqhd,bkhd->bhrnqk', qb, k_ctx).astype(jnp.float32) * scale
    n_loc = kh * NA_KB
    s = jnp.concatenate([s_loc.reshape(bsz, nh, rows, ncb, NA_QB, n_loc), s_ctx], axis=-1)
    p = jax.nn.softmax(s, axis=-1).astype(v.dtype)
    p_loc = p[..., :n_loc].reshape(bsz, nh, rows, ncb, NA_QB, kh, NA_KB)
    p_ctx = p[..., n_loc:]
    o = (jnp.einsum('bhrnqik,brinkhd->brnqhd', p_loc, vl)
         + jnp.einsum('bhrnqk,bkhd->brnqhd', p_ctx, v_ctx))
    return o.reshape(bsz, n, nh, dh)


def chunk_gating_mlp(u, vm, g_v, w_s, b_s):
    bsz, n, _ = u.shape
    u = jax.nn.gelu(u)
    vm = rms_norm(jax.nn.gelu(vm), g_v)
    vc = vm.reshape(bsz, n // CHUNK, CHUNK, H_CMLP, HEAD_DIM)
    mixed = jnp.einsum('hpq,bcqhd->bcphd', w_s, vc) + b_s.T[None, None, :, :, None]
    return u * mixed.reshape(bsz, n, D_CMLP)


def short_gated_conv(bg, cg, hx, w_conv):
    z = cg * hx
    n = z.shape[1]
    pad = CONV_W // 2
    zp = jnp.pad(z, ((0, 0), (pad, pad), (0, 0)))
    y = sum(zp[:, j:j + n] * w_conv[:, j] for j in range(CONV_W))
    return bg * y


def token_mixer(h, w_in, w_out, g_v, w_s, b_s, w_conv, attend):
    bsz, n, _ = h.shape
    q, k, v, u, vm, bg, cg, hx = split_projection(h @ w_in)
    q = q.reshape(bsz, n, H_ATT, HEAD_DIM)
    k = k.reshape(bsz, n, H_ATT, HEAD_DIM)
    v = v.reshape(bsz, n, H_ATT, HEAD_DIM)
    o_att = attend(q, k, v).reshape(bsz, n, D_ATT)
    o_mlp = chunk_gating_mlp(u, vm, g_v, w_s, b_s)
    o_conv = short_gated_conv(bg, cg, hx, w_conv)
    y = jnp.concatenate([o_att, o_mlp, o_conv], axis=-1) @ w_out
    return y, k, v


def swiglu(x, w_gu, w_d):
    g, u = jnp.split(x @ w_gu, 2, axis=-1)
    return (jax.nn.silu(g) * u) @ w_d


def moe_swiglu(x, w_router, w_gu, w_d):
    t = x.reshape(-1, D_MODEL)
    logits = (t @ w_router).astype(jnp.float32)
    top_v, top_i = lax.top_k(logits, TOP_K)
    gates = jax.nn.softmax(top_v, axis=-1)
    combine = jnp.sum(jax.nn.one_hot(top_i, N_EXPERTS, dtype=jnp.float32) * gates[..., None], axis=-2)
    combine = combine.astype(x.dtype)
    out = jnp.zeros_like(t)
    for e in range(N_EXPERTS):
        out = out + combine[:, e:e + 1] * swiglu(t, w_gu[e], w_d[e])
    return out.reshape(x.shape)


def setup_inputs(seed: int = 0) -> dict:
    key = jax.random.key(seed)
    ks = jax.random.split(key, 24)
    nrm = lambda k, shape, s: jax.random.normal(k, shape, jnp.float32) * s
    cache_shape = (DEC_BATCH, DEPTH, PAST_LEN, H_ATT, HEAD_DIM)
    return {
        'x_prompt': nrm(ks[0], (BATCH, SEQ, D_MODEL), 1.0),
        'x_sample': nrm(ks[1], (DEC_BATCH, DEC_SEQ, D_MODEL), 1.0),
        'cache_k': nrm(ks[2], cache_shape, 1.0),
        'cache_v': nrm(ks[3], cache_shape, 1.0),
        'c': nrm(ks[4], (DEC_BATCH, D_MODEL), 1.0),
        'c_ctx': nrm(ks[5], (D_MODEL,), 1.0),
        'w_ada': nrm(ks[6], (DEPTH, D_MODEL, N_MOD * D_MODEL), 0.5 * D_MODEL ** -0.5),
        'b_ada': nrm(ks[7], (DEPTH, N_MOD * D_MODEL), 0.01),
        'norm_g': 1.0 + nrm(ks[8], (DEPTH, 4, D_MODEL), 0.01),
        'w_in': nrm(ks[9], (DEPTH, D_MODEL, D_IN), D_MODEL ** -0.5),
        'w_out': nrm(ks[10], (DEPTH, D_MIX, D_MODEL), D_MIX ** -0.5),
        'rpb': nrm(ks[11], (DEPTH, H_ATT, 2 * NA_KH - 1, 2 * NA_KW - 1), 0.1),
        'g_v': 1.0 + nrm(ks[12], (DEPTH, D_CMLP), 0.01),
        'w_s': nrm(ks[13], (DEPTH, H_CMLP, CHUNK, CHUNK), CHUNK ** -0.5),
        'b_s': 1.0 + nrm(ks[14], (DEPTH, H_CMLP, CHUNK), 0.01),
        'w_conv': nrm(ks[15], (DEPTH, D_CONV, CONV_W), CONV_W ** -0.5),
        'w_ffn_gu': nrm(ks[16], (N_DENSE, D_MODEL, 2 * D_FF), D_MODEL ** -0.5),
        'w_ffn_d': nrm(ks[17], (N_DENSE, D_FF, D_MODEL), D_FF ** -0.5),
        'w_router': nrm(ks[18], (N_MOE, D_MODEL, N_EXPERTS), D_MODEL ** -0.5),
        'w_moe_gu': nrm(ks[19], (N_MOE, N_EXPERTS, D_MODEL, 2 * D_FF), D_MODEL ** -0.5),
        'w_moe_d': nrm(ks[20], (N_MOE, N_EXPERTS, D_FF, D_MODEL), D_FF ** -0.5),
    }


def reference(x_prompt, x_sample, cache_k, cache_v, c, c_ctx, w_ada, b_ada, norm_g,
              w_in, w_out, rpb, g_v, w_s, b_s, w_conv, w_ffn_gu, w_ffn_d,
              w_router, w_moe_gu, w_moe_d):
    def run_layer(x, cvec, l, attend):
        mod = ada_modulation(cvec, w_ada[l], b_ada[l])
        sh_m, sc_m, gt_m, sh_f, sc_f, gt_f = (mod[:, i] for i in range(N_MOD))
        h = rms_norm(x, norm_g[l, 0]) * (1 + sc_m) + sh_m
        y, k, v = token_mixer(h, w_in[l], w_out[l], g_v[l], w_s[l], b_s[l], w_conv[l], attend)
        x = x + gt_m * rms_norm(y, norm_g[l, 1])
        h = rms_norm(x, norm_g[l, 2]) * (1 + sc_f) + sh_f
        if l % 2 == 0:
            f = swiglu(h, w_ffn_gu[l // 2], w_ffn_d[l // 2])
        else:
            f = moe_swiglu(h, w_router[l // 2], w_moe_gu[l // 2], w_moe_d[l // 2])
        x = x + gt_f * rms_norm(f, norm_g[l, 3])
        return x, k, v

    cvec_ctx = c_ctx[None]
    xp = x_prompt
    ks_list, vs_list = [], []
    for l in range(DEPTH):
        xp, k, v = run_layer(xp, cvec_ctx, l, context_attention)
        ks_list.append(k)
        vs_list.append(v)
    new_cache_k = jnp.stack(ks_list, axis=1)
    new_cache_v = jnp.stack(vs_list, axis=1)

    xs = x_sample
    for l in range(DEPTH):
        attend = functools.partial(neighbourhood_attention, k_ctx=cache_k[:, l], v_ctx=cache_v[:, l], rpb=rpb[l])
        xs, _, _ = run_layer(xs, c, l, attend)
    return (xp, xs, new_cache_k, new_cache_v)
```

```python
import functools

import numpy as np
import jax
import jax.numpy as jnp
from jax import lax
from jax.experimental import pallas as pl
from jax.experimental.pallas import tpu as pltpu

F32 = jnp.float32
BF16 = jnp.bfloat16

D_MODEL = 1024
BATCH = 16
SEQ = 256
DEPTH = 4
DEC_BATCH = 4
DEC_SEQ = 2048
PAST_LEN = 256
GRID_W = 64
HEAD_DIM = 64
D_ATT = 512
D_CMLP = 256
D_CONV = 256
H_ATT = 8
H_CMLP = 4
CHUNK = 128
NA_KH = 8
NA_KW = 16
D_IN = 2816
D_FF = 2816
N_EXPERTS = 8
N_MOD = 6
EPS = 1e-6
NEG_INF = -1e30

T_P = BATCH * SEQ
T_S = DEC_BATCH * DEC_SEQ
T = T_P + T_S
TM = 256
NT = T // TM
NT_P = T_P // TM
TILES_PER_DEC = DEC_SEQ // TM
GROUPS = 8

ROWS = DEC_SEQ // GRID_W
QROWS = TM // GRID_W
BAND_TILES = 3
BAND = BAND_TILES * TM

FF_TILE = 1024
FF_SUB = 256
FF_CH = 256
N_FF_CH = D_FF // FF_CH
R_MOE = 2 * T + N_EXPERTS * FF_TILE
NT_MOE = R_MOE // FF_TILE
NT_DENSE = T // FF_TILE
GATHER_CH = 512

VMEM_LIMIT = 56 * 1024 * 1024


def _group_of_tile(i):
    return jnp.where(i < NT_P, 0, 1 + (i - NT_P) // TILES_PER_DEC)


def _rms(x, g):
    return x * lax.rsqrt(jnp.mean(x * x, axis=-1, keepdims=True) + EPS) * g


def _silu(x):
    return x / (1.0 + jnp.exp(-x))


def _gelu_tanh(x):
    c = np.float32(np.sqrt(2.0 / np.pi))
    return 0.5 * x * (1.0 + jnp.tanh(c * (x + np.float32(0.044715) * (x * x * x))))


def _mod_kernel(cv_ref, w_ref, b_ref, o_ref):
    a = _silu(cv_ref[...])
    o_ref[...] = jnp.dot(a, w_ref[...], preferred_element_type=F32,
                         precision=lax.Precision.HIGHEST) + b_ref[...]


def _modulation(cvec, w_ada, b_ada):
    tn = 1536
    nn = (N_MOD * D_MODEL) // tn
    out = pl.pallas_call(
        _mod_kernel,
        grid=(DEPTH, nn),
        in_specs=[
            pl.BlockSpec((GROUPS, D_MODEL), lambda l, n: (0, 0)),
            pl.BlockSpec((None, D_MODEL, tn), lambda l, n: (l, 0, n)),
            pl.BlockSpec((None, 1, tn), lambda l, n: (l, 0, n)),
        ],
        out_specs=pl.BlockSpec((None, GROUPS, tn), lambda l, n: (l, 0, n)),
        out_shape=jax.ShapeDtypeStruct((DEPTH, GROUPS, N_MOD * D_MODEL), F32),
        compiler_params=pltpu.CompilerParams(
            dimension_semantics=("parallel", "parallel"), vmem_limit_bytes=VMEM_LIMIT),
        name="adaln_mod",
    )(cvec, w_ada, b_ada.reshape(DEPTH, 1, N_MOD * D_MODEL))
    return out.reshape(DEPTH, GROUPS, N_MOD, D_MODEL)


def _inproj_kernel(x_ref, mod_ref, g_ref, w_ref, p_ref):
    h = _rms(x_ref[...], g_ref[0:1, :]) * (1.0 + mod_ref[1:2, :]) + mod_ref[0:1, :]
    p_ref[...] = jnp.dot(h.astype(BF16), w_ref[...], preferred_element_type=F32)


def _inproj(x, mod, norm_g, w_in_bf, l):
    return pl.pallas_call(
        _inproj_kernel,
        grid=(NT,),
        in_specs=[
            pl.BlockSpec((TM, D_MODEL), lambda i: (i, 0)),
            pl.BlockSpec((None, None, N_MOD, D_MODEL), lambda i: (l, _group_of_tile(i), 0, 0)),
            pl.BlockSpec((None, 4, D_MODEL), lambda i: (l, 0, 0)),
            pl.BlockSpec((None, D_MODEL, D_IN), lambda i: (l, 0, 0)),
        ],
        out_specs=pl.BlockSpec((TM, D_IN), lambda i: (i, 0)),
        out_shape=jax.ShapeDtypeStruct((T, D_IN), F32),
        compiler_params=pltpu.CompilerParams(
            dimension_semantics=("parallel",), vmem_limit_bytes=VMEM_LIMIT),
        name="inproj",
    )(x, mod, norm_g, w_in_bf)


def _dot_nt(a, b):
    return lax.dot_general(a, b, (((1,), (1,)), ((), ())), preferred_element_type=F32)


def _ctx_attn_kernel(q_ref, k_ref, v_ref, o_ref):
    scale = np.float32(HEAD_DIM ** -0.5)
    for h in range(H_ATT):
        sl = slice(h * HEAD_DIM, (h + 1) * HEAD_DIM)
        q = (q_ref[:, sl] * scale).astype(BF16)
        k = k_ref[:, sl].astype(BF16)
        v = v_ref[:, sl].astype(BF16)
        s = _dot_nt(q, k)
        m = jnp.max(s, axis=-1, keepdims=True)
        e = jnp.exp(s - m)
        den = jnp.sum(e, axis=-1, keepdims=True)
        o = jnp.dot(e.astype(BF16), v, preferred_element_type=F32) / den
        o_ref[:, sl] = o.astype(BF16)


def _na_attn_kernel(q_ref, k0_ref, k1_ref, k2_ref, v0_ref, v1_ref, v2_ref,
                    ck_ref, cv_ref, bias_ref, o_ref):
    scale = np.float32(HEAD_DIM ** -0.5)
    k_refs = (k0_ref, k1_ref, k2_ref)
    v_refs = (v0_ref, v1_ref, v2_ref)
    for h in range(H_ATT):
        sl = slice(h * HEAD_DIM, (h + 1) * HEAD_DIM)
        q = (q_ref[:, sl] * scale).astype(BF16)
        s_loc = [_dot_nt(q, k_refs[j][:, sl].astype(BF16)) + bias_ref[h, :, j * TM:(j + 1) * TM]
                 for j in range(BAND_TILES)]
        s_ctx = _dot_nt(q, ck_ref[:, sl].astype(BF16))
        m = jnp.max(s_ctx, axis=-1, keepdims=True)
        for s in s_loc:
            m = jnp.maximum(m, jnp.max(s, axis=-1, keepdims=True))
        e_ctx = jnp.exp(s_ctx - m)
        den = jnp.sum(e_ctx, axis=-1, keepdims=True)
        acc = jnp.dot(e_ctx.astype(BF16), cv_ref[:, sl].astype(BF16), preferred_element_type=F32)
        for j in range(BAND_TILES):
            e = jnp.exp(s_loc[j] - m)
            den = den + jnp.sum(e, axis=-1, keepdims=True)
            acc = acc + jnp.dot(e.astype(BF16), v_refs[j][:, sl].astype(BF16),
                                preferred_element_type=F32)
        o_ref[:, sl] = (acc / den).astype(BF16)


def _na_variant_tables():
    kh = min(NA_KH, ROWS)
    per_tile = []
    for rb in range(ROWS // QROWS):
        r0 = rb * QROWS
        bs = int(np.clip(rb - 1, 0, ROWS // QROWS - BAND_TILES)) * QROWS
        tab = -np.ones((QROWS, BAND_TILES * QROWS), np.int32)
        for qr in range(QROWS):
            r = r0 + qr
            rs = int(np.clip(r - kh // 2, 0, ROWS - kh))
            for kr in range(BAND_TILES * QROWS):
                ka = bs + kr
                if rs <= ka < rs + kh:
                    tab[qr, kr] = ka - r + (NA_KH - 1)
            assert (tab[qr] >= 0).sum() == kh
        per_tile.append(tab)
    variants, variant_of_tile = [], []
    for tab in per_tile:
        for vi, v in enumerate(variants):
            if np.array_equal(v, tab):
                variant_of_tile.append(vi)
                break
        else:
            variants.append(tab)
            variant_of_tile.append(len(variants) - 1)
    qc = np.arange(GRID_W)[:, None]
    kc = np.arange(GRID_W)[None, :]
    cs = np.clip(qc - NA_KW // 2, 0, GRID_W - NA_KW)
    col_valid = (kc >= cs) & (kc < cs + NA_KW)
    dc_idx = np.clip(kc - qc + (NA_KW - 1), 0, 2 * NA_KW - 2).astype(np.int32)
    return np.asarray(variant_of_tile, np.int32), np.stack(variants), dc_idx, col_valid


_NA_VARIANT_OF_TILE, _NA_DR_IDX, _NA_DC_IDX, _NA_COL_VALID = _na_variant_tables()
_NA_NVAR = _NA_DR_IDX.shape[0]


def _na_bias_tables(rpb):
    small = jnp.where(_NA_COL_VALID[None, None, None],
                      jnp.take(rpb, jnp.asarray(_NA_DC_IDX), axis=3), NEG_INF)
    neg = jnp.full((DEPTH, H_ATT, GRID_W, GRID_W), NEG_INF, F32)
    per_var = []
    for vi in range(_NA_NVAR):
        q_rows = []
        for qr in range(QROWS):
            blocks = [small[:, :, int(d)] if d >= 0 else neg for d in _NA_DR_IDX[vi, qr]]
            q_rows.append(jnp.concatenate(blocks, axis=-1))
        per_var.append(jnp.concatenate(q_rows, axis=-2))
    return jnp.stack(per_var, axis=1)


def _attention(p, ck_all, cv_all, bias_all, l):
    o_ctx = pl.pallas_call(
        _ctx_attn_kernel,
        grid=(NT_P,),
        in_specs=[pl.BlockSpec((TM, D_ATT), lambda b: (b, 0)),
                  pl.BlockSpec((TM, D_ATT), lambda b: (b, 1)),
                  pl.BlockSpec((TM, D_ATT), lambda b: (b, 2))],
        out_specs=pl.BlockSpec((TM, D_ATT), lambda b: (b, 0)),
        out_shape=jax.ShapeDtypeStruct((T_P, D_ATT), BF16),
        compiler_params=pltpu.CompilerParams(
            dimension_semantics=("parallel",), vmem_limit_bytes=VMEM_LIMIT),
        name="ctx_attn",
    )(p, p, p)

    nrb = TILES_PER_DEC
    var_of_tile = [int(v) for v in _NA_VARIANT_OF_TILE]

    def q_tile(rb, b):
        return NT_P + b * nrb + rb

    def band_tile(rb, b, j):
        return NT_P + b * nrb + jnp.clip(rb - 1, 0, nrb - BAND_TILES) + j

    def variant(rb):
        v = jnp.int32(var_of_tile[0])
        for t in range(1, nrb):
            v = jnp.where(rb >= t, jnp.int32(var_of_tile[t]), v)
        return v

    kv_specs = [pl.BlockSpec((TM, D_ATT), functools.partial(
        lambda rb, b, j, col: (band_tile(rb, b, j), col), j=j, col=col))
        for col in (1, 2) for j in range(BAND_TILES)]
    o_na = pl.pallas_call(
        _na_attn_kernel,
        grid=(nrb, DEC_BATCH),
        in_specs=[pl.BlockSpec((TM, D_ATT), lambda rb, b: (q_tile(rb, b), 0))] + kv_specs + [
            pl.BlockSpec((None, None, PAST_LEN, D_ATT), lambda rb, b: (b, l, 0, 0)),
            pl.BlockSpec((None, None, PAST_LEN, D_ATT), lambda rb, b: (b, l, 0, 0)),
            pl.BlockSpec((None, None, H_ATT, TM, BAND), lambda rb, b: (l, variant(rb), 0, 0, 0)),
        ],
        out_specs=pl.BlockSpec((TM, D_ATT), lambda rb, b: (b * nrb + rb, 0)),
        out_shape=jax.ShapeDtypeStruct((T_S, D_ATT), BF16),
        compiler_params=pltpu.CompilerParams(
            dimension_semantics=("arbitrary", "arbitrary"), vmem_limit_bytes=VMEM_LIMIT),
        name="na_attn",
    )(p, p, p, p, p, p, p, ck_all, cv_all, bias_all)
    return jnp.concatenate([o_ctx, o_na], axis=0)


def _mixer_out_kernel(with_router, *refs):
    if with_router:
        (u_ref, vm_ref, bg_ref, cg_ref, hx_ref, cgp_ref, hxp_ref, cgn_ref, hxn_ref,
         oatt_ref, x_ref, mod_ref, g_ref, gv_ref, ws_ref, bs_ref, wc_ref, wo_ref, wr_ref,
         x1_ref, h2_ref, route_ref) = refs
    else:
        (u_ref, vm_ref, bg_ref, cg_ref, hx_ref, cgp_ref, hxp_ref, cgn_ref, hxn_ref,
         oatt_ref, x_ref, mod_ref, g_ref, gv_ref, ws_ref, bs_ref, wc_ref, wo_ref,
         x1_ref, h2_ref) = refs
    i = pl.program_id(0)

    u = _gelu_tanh(u_ref[...])
    vm = _rms(_gelu_tanh(vm_ref[...]), gv_ref[...]).astype(BF16)
    chunks = []
    for c in range(TM // CHUNK):
        rows = slice(c * CHUNK, (c + 1) * CHUNK)
        heads = [jnp.dot(ws_ref[h].astype(BF16), vm[rows, h * HEAD_DIM:(h + 1) * HEAD_DIM],
                         preferred_element_type=F32) for h in range(H_CMLP)]
        chunks.append(jnp.concatenate(heads, axis=1) + bs_ref[...])
    o_mlp = u * jnp.concatenate(chunks, axis=0)

    j = (i - NT_P) % TILES_PER_DEC
    has_prev = jnp.logical_and(i >= NT_P, j > 0)
    has_next = jnp.logical_and(i >= NT_P, j < TILES_PER_DEC - 1)
    z = cg_ref[...] * hx_ref[...]
    z_halo_prev = jnp.where(has_prev, cgp_ref[7:8, :] * hxp_ref[7:8, :], 0.0)
    z_halo_next = jnp.where(has_next, cgn_ref[0:1, :] * hxn_ref[0:1, :], 0.0)
    row = lax.broadcasted_iota(jnp.int32, (TM, D_CONV), 0)
    z_prev = jnp.where(row == 0, z_halo_prev, pltpu.roll(z, 1, 0))
    z_next = jnp.where(row == TM - 1, z_halo_next, pltpu.roll(z, TM - 1, 0))
    y_conv = z_prev * wc_ref[0:1, :] + z * wc_ref[1:2, :] + z_next * wc_ref[2:3, :]
    o_conv = bg_ref[...] * y_conv

    y = jnp.dot(oatt_ref[...], wo_ref[0:D_ATT, :], preferred_element_type=F32)
    y = y + jnp.dot(o_mlp.astype(BF16), wo_ref[D_ATT:D_ATT + D_CMLP, :],
                    preferred_element_type=F32)
    y = y + jnp.dot(o_conv.astype(BF16), wo_ref[D_ATT + D_CMLP:, :],
                    preferred_element_type=F32)

    x1 = x_ref[...] + mod_ref[2:3, :] * _rms(y, g_ref[1:2, :])
    h2 = _rms(x1, g_ref[2:3, :]) * (1.0 + mod_ref[4:5, :]) + mod_ref[3:4, :]
    x1_ref[...] = x1
    h2_ref[...] = h2

    if with_router:
        logits = jnp.dot(h2, wr_ref[...], preferred_element_type=F32,
                         precision=lax.Precision.HIGHEST)
        lane = lax.broadcasted_iota(jnp.int32, logits.shape, 1).astype(F32)
        big = np.float32(1 << 20)
        lg = jnp.where(lane < N_EXPERTS, logits, -jnp.inf)
        m1 = jnp.max(lg, axis=-1, keepdims=True)
        i1 = jnp.min(jnp.where(lg == m1, lane, big), axis=-1, keepdims=True)
        lg2 = jnp.where(lane == i1, -jnp.inf, lg)
        m2 = jnp.max(lg2, axis=-1, keepdims=True)
        i2 = jnp.min(jnp.where(lg2 == m2, lane, big), axis=-1, keepdims=True)
        e2 = jnp.exp(m2 - m1)
        den = 1.0 + e2
        gate1 = 1.0 / den
        gate2 = e2 / den
        route = jnp.where(lane == 0, i1,
                          jnp.where(lane == 1, i2,
                                    jnp.where(lane == 2, gate1,
                                              jnp.where(lane == 3, gate2, 0.0))))
        route_ref[...] = route


def _mixer_out(p, o_att, x, mod, norm_g, g_v, w_s, bs_b, wc_t, w_out_bf, w_router_pad, l, moe_idx):
    with_router = moe_idx is not None
    hb = TM // 8

    def col(cb):
        return pl.BlockSpec((TM, D_CMLP), lambda i: (i, cb))

    def halo_prev(cb):
        return pl.BlockSpec((8, D_CONV), lambda i: (jnp.maximum(i * hb - 1, 0), cb))

    def halo_next(cb):
        return pl.BlockSpec((8, D_CONV), lambda i: (jnp.minimum((i + 1) * hb, T // 8 - 1), cb))

    in_specs = [col(6), col(7), col(8), col(9), col(10),
                halo_prev(9), halo_prev(10), halo_next(9), halo_next(10),
                pl.BlockSpec((TM, D_ATT), lambda i: (i, 0)),
                pl.BlockSpec((TM, D_MODEL), lambda i: (i, 0)),
                pl.BlockSpec((None, None, N_MOD, D_MODEL), lambda i: (l, _group_of_tile(i), 0, 0)),
                pl.BlockSpec((None, 4, D_MODEL), lambda i: (l, 0, 0)),
                pl.BlockSpec((None, 1, D_CMLP), lambda i: (l, 0, 0)),
                pl.BlockSpec((None, H_CMLP, CHUNK, CHUNK), lambda i: (l, 0, 0, 0)),
                pl.BlockSpec((None, CHUNK, D_CMLP), lambda i: (l, 0, 0)),
                pl.BlockSpec((None, 3, D_CONV), lambda i: (l, 0, 0)),
                pl.BlockSpec((None, D_MODEL, D_MODEL), lambda i: (l, 0, 0))]
    args = [p, p, p, p, p, p, p, p, p, o_att, x, mod, norm_g, g_v, w_s, bs_b, wc_t, w_out_bf]
    out_specs = [pl.BlockSpec((TM, D_MODEL), lambda i: (i, 0)),
                 pl.BlockSpec((TM, D_MODEL), lambda i: (i, 0))]
    out_shape = [jax.ShapeDtypeStruct((T, D_MODEL), F32), jax.ShapeDtypeStruct((T, D_MODEL), F32)]
    if with_router:
        in_specs.append(pl.BlockSpec((None, D_MODEL, 128), lambda i: (moe_idx, 0, 0)))
        args.append(w_router_pad)
        out_specs.append(pl.BlockSpec((TM, 128), lambda i: (i, 0)))
        out_shape.append(jax.ShapeDtypeStruct((T, 128), F32))
    return pl.pallas_call(
        functools.partial(_mixer_out_kernel, with_router),
        grid=(NT,),
        in_specs=in_specs,
        out_specs=out_specs,
        out_shape=out_shape,
        compiler_params=pltpu.CompilerParams(
            dimension_semantics=("parallel",), vmem_limit_bytes=VMEM_LIMIT),
        name="mixer_out_router" if with_router else "mixer_out",
    )(*args)


def _ffn_kernel(te_ref, nv_ref, et_ref, x_ref, wg_ref, wu_ref, wd_ref, o_ref, xb_ref):
    del te_ref, et_ref
    t = pl.program_id(0)
    j = pl.program_id(1)
    nvalid = nv_ref[t]

    @pl.when(jnp.logical_and(j == 0, nvalid >= 0))
    def _():
        xb_ref[...] = x_ref[...].astype(BF16)

    wg = wg_ref[...].astype(BF16)
    wu = wu_ref[...].astype(BF16)
    wd = wd_ref[...].astype(BF16)
    for s in range(FF_TILE // FF_SUB):
        rows = slice(s * FF_SUB, (s + 1) * FF_SUB)

        @pl.when(s * FF_SUB < nvalid)
        def _():
            xs = xb_ref[rows, :]
            g = jnp.dot(xs, wg, preferred_element_type=F32)
            u = jnp.dot(xs, wu, preferred_element_type=F32)
            a = (_silu(g) * u).astype(BF16)
            part = jnp.dot(a, wd, preferred_element_type=F32)

            @pl.when(j == 0)
            def _():
                o_ref[rows, :] = part

            @pl.when(j > 0)
            def _():
                o_ref[rows, :] += part

        @pl.when(jnp.logical_and(j == 0, s * FF_SUB >= nvalid))
        def _():
            o_ref[rows, :] = jnp.zeros((FF_SUB, D_MODEL), F32)


def _ffn(x_rows, w_gu, w_d, tile_expert, tile_nvalid, tile_eff, n_tiles, name):
    def chunk(t, j, nv):
        return jnp.where(nv[t] >= 0, j, N_FF_CH - 1)

    grid_spec = pltpu.PrefetchScalarGridSpec(
        num_scalar_prefetch=3,
        grid=(n_tiles, N_FF_CH),
        in_specs=[
            pl.BlockSpec((FF_TILE, D_MODEL), lambda t, j, te, nv, et: (et[t], 0)),
            pl.BlockSpec((None, D_MODEL, FF_CH),
                         lambda t, j, te, nv, et: (te[t], 0, chunk(t, j, nv))),
            pl.BlockSpec((None, D_MODEL, FF_CH),
                         lambda t, j, te, nv, et: (te[t], 0, N_FF_CH + chunk(t, j, nv))),
            pl.BlockSpec((None, FF_CH, D_MODEL),
                         lambda t, j, te, nv, et: (te[t], chunk(t, j, nv), 0)),
        ],
        out_specs=pl.BlockSpec((FF_TILE, D_MODEL), lambda t, j, te, nv, et: (t, 0)),
        scratch_shapes=[pltpu.VMEM((FF_TILE, D_MODEL), BF16)],
    )
    return pl.pallas_call(
        _ffn_kernel,
        grid_spec=grid_spec,
        out_shape=jax.ShapeDtypeStruct((n_tiles * FF_TILE, D_MODEL), F32),
        compiler_params=pltpu.CompilerParams(
            dimension_semantics=("arbitrary", "arbitrary"), vmem_limit_bytes=VMEM_LIMIT),
        name=name,
    )(tile_expert, tile_nvalid, tile_eff, x_rows, w_gu, w_gu, w_d)


def _gather_kernel(idx_ref, src_ref, out_ref, sem):
    i = pl.program_id(0)
    base = i * GATHER_CH

    def row_copy(r):
        return pltpu.make_async_copy(src_ref.at[pl.ds(idx_ref[0, r], 1)],
                                     out_ref.at[pl.ds(base + r, 1)], sem)

    def start(r, carry):
        row_copy(r).start()
        return carry

    def wait(r, carry):
        row_copy(r).wait()
        return carry

    lax.fori_loop(0, GATHER_CH, start, 0)
    lax.fori_loop(0, GATHER_CH, wait, 0)


def _gather_rows(src, idx, name):
    n = idx.shape[0]
    steps = n // GATHER_CH
    return pl.pallas_call(
        _gather_kernel,
        grid=(steps,),
        in_specs=[pl.BlockSpec((None, 1, GATHER_CH), lambda i: (i, 0, 0),
                               memory_space=pltpu.SMEM),
                  pl.BlockSpec(memory_space=pl.ANY)],
        out_specs=pl.BlockSpec(memory_space=pl.ANY),
        out_shape=jax.ShapeDtypeStruct((n, src.shape[1]), src.dtype),
        scratch_shapes=[pltpu.SemaphoreType.DMA(())],
        compiler_params=pltpu.CompilerParams(dimension_semantics=("arbitrary",)),
        name=name,
    )(idx.reshape(steps, 1, GATHER_CH), src)


def _ffn_out_dense_kernel(x1_ref, f_ref, mod_ref, g_ref, o_ref):
    o_ref[...] = x1_ref[...] + mod_ref[5:6, :] * _rms(f_ref[...], g_ref[3:4, :])


def _ffn_out_moe_kernel(x1_ref, y1_ref, y2_ref, route_ref, mod_ref, g_ref, o_ref):
    f = route_ref[:, 2:3] * y1_ref[...] + route_ref[:, 3:4] * y2_ref[...]
    o_ref[...] = x1_ref[...] + mod_ref[5:6, :] * _rms(f, g_ref[3:4, :])


def _ffn_out(x1, ys, route, mod, norm_g, l):
    row = pl.BlockSpec((TM, D_MODEL), lambda i: (i, 0))
    mod_spec = pl.BlockSpec((None, None, N_MOD, D_MODEL), lambda i: (l, _group_of_tile(i), 0, 0))
    g_spec = pl.BlockSpec((None, 4, D_MODEL), lambda i: (l, 0, 0))
    if route is None:
        kern, name = _ffn_out_dense_kernel, "ffn_out_dense"
        in_specs = [row, row, mod_spec, g_spec]
        args = (x1, ys, mod, norm_g)
    else:
        kern, name = _ffn_out_moe_kernel, "ffn_out_moe"
        in_specs = [row, row, pl.BlockSpec((TM, D_MODEL), lambda i: (i + NT, 0)),
                    pl.BlockSpec((TM, 128), lambda i: (i, 0)), mod_spec, g_spec]
        args = (x1, ys, ys, route, mod, norm_g)
    return pl.pallas_call(
        kern,
        grid=(NT,),
        in_specs=in_specs,
        out_specs=row,
        out_shape=jax.ShapeDtypeStruct((T, D_MODEL), F32),
        compiler_params=pltpu.CompilerParams(
            dimension_semantics=("parallel",), vmem_limit_bytes=VMEM_LIMIT),
        name=name,
    )(*args)


def _route_plan(route):
    e1 = route[:, 0].astype(jnp.int32)
    e2 = route[:, 1].astype(jnp.int32)
    ar = jnp.arange(N_EXPERTS, dtype=jnp.int32)
    oh1 = (e1[:, None] == ar[None, :]).astype(jnp.int32)
    oh2 = (e2[:, None] == ar[None, :]).astype(jnp.int32)
    oh = oh1 + oh2
    csum_incl = jnp.cumsum(oh, axis=0)
    csum = csum_incl - oh
    counts = csum_incl[-1]
    ntile_e = (counts + FF_TILE - 1) // FF_TILE
    tile_end_e = jnp.cumsum(ntile_e)
    tile_start_e = tile_end_e - ntile_e
    gstart = tile_start_e * FF_TILE
    pos1 = jnp.sum((gstart[None, :] + csum) * oh1, axis=1)
    pos2 = jnp.sum((gstart[None, :] + csum) * oh2, axis=1)
    tok = jnp.arange(T, dtype=jnp.int32)
    src = jnp.zeros((R_MOE,), jnp.int32).at[pos1].set(tok).at[pos2].set(tok)
    n_used = tile_end_e[-1]
    tiles = jnp.arange(NT_MOE, dtype=jnp.int32)
    eff = jnp.minimum(tiles, n_used - 1)
    te = jnp.sum((eff[:, None] >= tile_end_e[None, :]).astype(jnp.int32), axis=1)
    te = jnp.minimum(te, N_EXPERTS - 1)
    nvalid = jnp.clip(counts[te] - (eff - tile_start_e[te]) * FF_TILE, 0, FF_TILE)
    nvalid = jnp.where(tiles < n_used, nvalid, -1)
    return src, pos1, pos2, te.astype(jnp.int32), nvalid.astype(jnp.int32), eff.astype(jnp.int32)


def kernel(x_prompt, x_sample, cache_k, cache_v, c, c_ctx, w_ada, b_ada, norm_g, w_in, w_out,
           rpb, g_v, w_s, b_s, w_conv, w_ffn_gu, w_ffn_d, w_router, w_moe_gu, w_moe_d):
    x = jnp.concatenate([x_prompt.reshape(T_P, D_MODEL), x_sample.reshape(T_S, D_MODEL)], axis=0)
    cvec = jnp.concatenate([c_ctx[None], c, jnp.zeros((GROUPS - 1 - DEC_BATCH, D_MODEL), F32)],
                           axis=0)
    mod = _modulation(cvec, w_ada, b_ada)

    w_in_bf = w_in.astype(BF16)
    w_out_bf = w_out.astype(BF16)
    ck_all = cache_k.reshape(DEC_BATCH, DEPTH, PAST_LEN, D_ATT)
    cv_all = cache_v.reshape(DEC_BATCH, DEPTH, PAST_LEN, D_ATT)
    bias_all = _na_bias_tables(rpb)
    g_v3 = g_v.reshape(DEPTH, 1, D_CMLP)
    bs_b = jnp.repeat(jnp.swapaxes(b_s, 1, 2), HEAD_DIM, axis=2)
    wc_t = jnp.swapaxes(w_conv, 1, 2)
    w_router_pad = jnp.pad(w_router, ((0, 0), (0, 0), (0, 128 - N_EXPERTS)))
    w_moe_gu_all = w_moe_gu.reshape(-1, D_MODEL, 2 * D_FF)
    w_moe_d_all = w_moe_d.reshape(-1, D_FF, D_MODEL)

    dense_te = jnp.zeros((NT_DENSE,), jnp.int32)
    dense_nv = jnp.full((NT_DENSE,), FF_TILE, jnp.int32)
    dense_eff = jnp.arange(NT_DENSE, dtype=jnp.int32)

    ks, vs = [], []
    for l in range(DEPTH):
        p = _inproj(x, mod, norm_g, w_in_bf, l)
        ks.append(p[:T_P, D_ATT:2 * D_ATT])
        vs.append(p[:T_P, 2 * D_ATT:3 * D_ATT])
        o_att = _attention(p, ck_all, cv_all, bias_all, l)
        if l % 2 == 0:
            x1, h2 = _mixer_out(p, o_att, x, mod, norm_g, g_v3, w_s, bs_b, wc_t, w_out_bf,
                                None, l, None)
            f = _ffn(h2, w_ffn_gu, w_ffn_d, dense_te + l // 2, dense_nv, dense_eff, NT_DENSE,
                     "ffn_dense")
            x = _ffn_out(x1, f, None, mod, norm_g, l)
        else:
            x1, h2, route = _mixer_out(p, o_att, x, mod, norm_g, g_v3, w_s, bs_b, wc_t, w_out_bf,
                                       w_router_pad, l, l // 2)
            src, pos1, pos2, te, nv, eff = _route_plan(route)
            xs = _gather_rows(h2, src, "moe_dispatch")
            ys = _ffn(xs, w_moe_gu_all, w_moe_d_all, te + (l // 2) * N_EXPERTS, nv, eff, NT_MOE,
                      "ffn_moe")
            y12 = _gather_rows(ys, jnp.concatenate([pos1, pos2]), "moe_combine")
            x = _ffn_out(x1, y12, route, mod, norm_g, l)

    new_k = jnp.stack(ks, axis=0).reshape(DEPTH, BATCH, SEQ, H_ATT, HEAD_DIM).transpose(1, 0, 2, 3, 4)
    new_v = jnp.stack(vs, axis=0).reshape(DEPTH, BATCH, SEQ, H_ATT, HEAD_DIM).transpose(1, 0, 2, 3, 4)
    y_prompt = x[:T_P].reshape(BATCH, SEQ, D_MODEL)
    y_sample = x[T_P:].reshape(DEC_BATCH, DEC_SEQ, D_MODEL)
    return y_prompt, y_sample, new_k, new_v
```

```python
import functools

import numpy as np
import jax
import jax.numpy as jnp
from jax import lax
from jax.experimental import pallas as pl
from jax.experimental.pallas import tpu as pltpu

F32 = jnp.float32
BF16 = jnp.bfloat16

D_MODEL = 1024
BATCH = 16
SEQ = 256
DEPTH = 4
DEC_BATCH = 4
DEC_SEQ = 2048
PAST_LEN = 256
GRID_W = 64
HEAD_DIM = 64
D_ATT = 512
D_CMLP = 256
D_CONV = 256
H_ATT = 8
H_CMLP = 4
CHUNK = 128
NA_KH = 8
NA_KW = 16
D_IN = 2816
D_FF = 2816
N_EXPERTS = 8
N_MOD = 6
EPS = 1e-6
NEG_INF = -1e30

T_P = BATCH * SEQ
T_S = DEC_BATCH * DEC_SEQ
T = T_P + T_S
TM = 256
NT = T // TM
NT_P = T_P // TM
TILES_PER_DEC = DEC_SEQ // TM
GROUPS = 8

ROWS = DEC_SEQ // GRID_W
QROWS = TM // GRID_W
BAND_TILES = 3
BAND = BAND_TILES * TM

FF_TILE = 1024
FF_SUB = 256
FF_CH = 256
N_FF_CH = D_FF // FF_CH
R_MOE = 2 * T + N_EXPERTS * FF_TILE
NT_MOE = R_MOE // FF_TILE
NT_DENSE = T // FF_TILE
GATHER_CH = 512

VMEM_LIMIT = 56 * 1024 * 1024


def _group_of_tile(i):
    return jnp.where(i < NT_P, 0, 1 + (i - NT_P) // TILES_PER_DEC)


def _rms(x, g):
    return x * lax.rsqrt(jnp.mean(x * x, axis=-1, keepdims=True) + EPS) * g


def _silu(x):
    return x / (1.0 + jnp.exp(-x))


def _gelu_tanh(x):
    c = np.float32(np.sqrt(2.0 / np.pi))
    return 0.5 * x * (1.0 + jnp.tanh(c * (x + np.float32(0.044715) * (x * x * x))))


def _mod_kernel(cv_ref, w_ref, b_ref, o_ref):
    a = _silu(cv_ref[...])
    o_ref[...] = jnp.dot(a, w_ref[...], preferred_element_type=F32,
                         precision=lax.Precision.HIGHEST) + b_ref[...]


def _modulation(cvec, w_ada, b_ada):
    tn = 1536
    nn = (N_MOD * D_MODEL) // tn
    out = pl.pallas_call(
        _mod_kernel,
        grid=(DEPTH, nn),
        in_specs=[
            pl.BlockSpec((GROUPS, D_MODEL), lambda l, n: (0, 0)),
            pl.BlockSpec((None, D_MODEL, tn), lambda l, n: (l, 0, n)),
            pl.BlockSpec((None, 1, tn), lambda l, n: (l, 0, n)),
        ],
        out_specs=pl.BlockSpec((None, GROUPS, tn), lambda l, n: (l, 0, n)),
        out_shape=jax.ShapeDtypeStruct((DEPTH, GROUPS, N_MOD * D_MODEL), F32),
        compiler_params=pltpu.CompilerParams(
            dimension_semantics=("parallel", "parallel"), vmem_limit_bytes=VMEM_LIMIT),
        name="adaln_mod",
    )(cvec, w_ada, b_ada.reshape(DEPTH, 1, N_MOD * D_MODEL))
    return out.reshape(DEPTH, GROUPS, N_MOD, D_MODEL)


def _inproj_kernel(x_ref, mod_ref, g_ref, w_ref, p_ref):
    h = _rms(x_ref[...], g_ref[0:1, :]) * (1.0 + mod_ref[1:2, :]) + mod_ref[0:1, :]
    p_ref[...] = jnp.dot(h.astype(BF16), w_ref[...], preferred_element_type=F32)


def _inproj(x, mod, norm_g, w_in_bf, l):
    return pl.pallas_call(
        _inproj_kernel,
        grid=(NT,),
        in_specs=[
            pl.BlockSpec((TM, D_MODEL), lambda i: (i, 0)),
            pl.BlockSpec((None, None, N_MOD, D_MODEL), lambda i: (l, _group_of_tile(i), 0, 0)),
            pl.BlockSpec((None, 4, D_MODEL), lambda i: (l, 0, 0)),
            pl.BlockSpec((None, D_MODEL, D_IN), lambda i: (l, 0, 0)),
        ],
        out_specs=pl.BlockSpec((TM, D_IN), lambda i: (i, 0)),
        out_shape=jax.ShapeDtypeStruct((T, D_IN), F32),
        compiler_params=pltpu.CompilerParams(
            dimension_semantics=("parallel",), vmem_limit_bytes=VMEM_LIMIT),
        name="inproj",
    )(x, mod, norm_g, w_in_bf)


def _dot_nt(a, b):
    return lax.dot_general(a, b, (((1,), (1,)), ((), ())), preferred_element_type=F32)


def _ctx_attn_kernel(q_ref, k_ref, v_ref, o_ref):
    scale = np.float32(HEAD_DIM ** -0.5)
    for h in range(H_ATT):
        sl = slice(h * HEAD_DIM, (h + 1) * HEAD_DIM)
        q = (q_ref[:, sl] * scale).astype(BF16)
        k = k_ref[:, sl].astype(BF16)
        v = v_ref[:, sl].astype(BF16)
        s = _dot_nt(q, k)
        m = jnp.max(s, axis=-1, keepdims=True)
        e = jnp.exp(s - m)
        den = jnp.sum(e, axis=-1, keepdims=True)
        o = jnp.dot(e.astype(BF16), v, preferred_element_type=F32) / den
        o_ref[:, sl] = o.astype(BF16)


def _na_attn_kernel(q_ref, k0_ref, k1_ref, k2_ref, v0_ref, v1_ref, v2_ref,
                    ck_ref, cv_ref, bias_ref, o_ref):
    scale = np.float32(HEAD_DIM ** -0.5)
    k_refs = (k0_ref, k1_ref, k2_ref)
    v_refs = (v0_ref, v1_ref, v2_ref)
    for h in range(H_ATT):
        sl = slice(h * HEAD_DIM, (h + 1) * HEAD_DIM)
        q = (q_ref[:, sl] * scale).astype(BF16)
        s_loc = [_dot_nt(q, k_refs[j][:, sl].astype(BF16)) + bias_ref[h, :, j * TM:(j + 1) * TM]
                 for j in range(BAND_TILES)]
        s_ctx = _dot_nt(q, ck_ref[:, sl].astype(BF16))
        m = jnp.max(s_ctx, axis=-1, keepdims=True)
        for s in s_loc:
            m = jnp.maximum(m, jnp.max(s, axis=-1, keepdims=True))
        e_ctx = jnp.exp(s_ctx - m)
        den = jnp.sum(e_ctx, axis=-1, keepdims=True)
        acc = jnp.dot(e_ctx.astype(BF16), cv_ref[:, sl].astype(BF16), preferred_element_type=F32)
        for j in range(BAND_TILES):
            e = jnp.exp(s_loc[j] - m)
            den = den + jnp.sum(e, axis=-1, keepdims=True)
            acc = acc + jnp.dot(e.astype(BF16), v_refs[j][:, sl].astype(BF16),
                                preferred_element_type=F32)
        o_ref[:, sl] = (acc / den).astype(BF16)


def _na_variant_tables():
    kh = min(NA_KH, ROWS)
    per_tile = []
    for rb in range(ROWS // QROWS):
        r0 = rb * QROWS
        bs = int(np.clip(rb - 1, 0, ROWS // QROWS - BAND_TILES)) * QROWS
        tab = -np.ones((QROWS, BAND_TILES * QROWS), np.int32)
        for qr in range(QROWS):
            r = r0 + qr
            rs = int(np.clip(r - kh // 2, 0, ROWS - kh))
            for kr in range(BAND_TILES * QROWS):
                ka = bs + kr
                if rs <= ka < rs + kh:
                    tab[qr, kr] = ka - r + (NA_KH - 1)
            assert (tab[qr] >= 0).sum() == kh
        per_tile.append(tab)
    variants, variant_of_tile = [], []
    for tab in per_tile:
        for vi, v in enumerate(variants):
            if np.array_equal(v, tab):
                variant_of_tile.append(vi)
                break
        else:
            variants.append(tab)
            variant_of_tile.append(len(variants) - 1)
    qc = np.arange(GRID_W)[:, None]
    kc = np.arange(GRID_W)[None, :]
    cs = np.clip(qc - NA_KW // 2, 0, GRID_W - NA_KW)
    col_valid = (kc >= cs) & (kc < cs + NA_KW)
    dc_idx = np.clip(kc - qc + (NA_KW - 1), 0, 2 * NA_KW - 2).astype(np.int32)
    return np.asarray(variant_of_tile, np.int32), np.stack(variants), dc_idx, col_valid


_NA_VARIANT_OF_TILE, _NA_DR_IDX, _NA_DC_IDX, _NA_COL_VALID = _na_variant_tables()
_NA_NVAR = _NA_DR_IDX.shape[0]


def _na_bias_tables(rpb):
    small = jnp.where(_NA_COL_VALID[None, None, None],
                      jnp.take(rpb, jnp.asarray(_NA_DC_IDX), axis=3), NEG_INF)
    neg = jnp.full((DEPTH, H_ATT, GRID_W, GRID_W), NEG_INF, F32)
    per_var = []
    for vi in range(_NA_NVAR):
        q_rows = []
        for qr in range(QROWS):
            blocks = [small[:, :, int(d)] if d >= 0 else neg for d in _NA_DR_IDX[vi, qr]]
            q_rows.append(jnp.concatenate(blocks, axis=-1))
        per_var.append(jnp.concatenate(q_rows, axis=-2))
    return jnp.stack(per_var, axis=1)


def _attention(p, ck_all, cv_all, bias_all, l):
    o_ctx = pl.pallas_call(
        _ctx_attn_kernel,
        grid=(NT_P,),
        in_specs=[pl.BlockSpec((TM, D_ATT), lambda b: (b, 0)),
                  pl.BlockSpec((TM, D_ATT), lambda b: (b, 1)),
                  pl.BlockSpec((TM, D_ATT), lambda b: (b, 2))],
        out_specs=pl.BlockSpec((TM, D_ATT), lambda b: (b, 0)),
        out_shape=jax.ShapeDtypeStruct((T_P, D_ATT), BF16),
        compiler_params=pltpu.CompilerParams(
            dimension_semantics=("parallel",), vmem_limit_bytes=VMEM_LIMIT),
        name="ctx_attn",
    )(p, p, p)

    nrb = TILES_PER_DEC
    var_of_tile = [int(v) for v in _NA_VARIANT_OF_TILE]

    def q_tile(rb, b):
        return NT_P + b * nrb + rb

    def band_tile(rb, b, j):
        return NT_P + b * nrb + jnp.clip(rb - 1, 0, nrb - BAND_TILES) + j

    def variant(rb):
        v = jnp.int32(var_of_tile[0])
        for t in range(1, nrb):
            v = jnp.where(rb >= t, jnp.int32(var_of_tile[t]), v)
        return v

    kv_specs = [pl.BlockSpec((TM, D_ATT), functools.partial(
        lambda rb, b, j, col: (band_tile(rb, b, j), col), j=j, col=col))
        for col in (1, 2) for j in range(BAND_TILES)]
    o_na = pl.pallas_call(
        _na_attn_kernel,
        grid=(nrb, DEC_BATCH),
        in_specs=[pl.BlockSpec((TM, D_ATT), lambda rb, b: (q_tile(rb, b), 0))] + kv_specs + [
            pl.BlockSpec((None, None, PAST_LEN, D_ATT), lambda rb, b: (b, l, 0, 0)),
            pl.BlockSpec((None, None, PAST_LEN, D_ATT), lambda rb, b: (b, l, 0, 0)),
            pl.BlockSpec((None, None, H_ATT, TM, BAND), lambda rb, b: (l, variant(rb), 0, 0, 0)),
        ],
        out_specs=pl.BlockSpec((TM, D_ATT), lambda rb, b: (b * nrb + rb, 0)),
        out_shape=jax.ShapeDtypeStruct((T_S, D_ATT), BF16),
        compiler_params=pltpu.CompilerParams(
            dimension_semantics=("arbitrary", "arbitrary"), vmem_limit_bytes=VMEM_LIMIT),
        name="na_attn",
    )(p, p, p, p, p, p, p, ck_all, cv_all, bias_all)
    return jnp.concatenate([o_ctx, o_na], axis=0)


def _mixer_out_kernel(with_router, *refs):
    if with_router:
        (u_ref, vm_ref, bg_ref, cg_ref, hx_ref, cgp_ref, hxp_ref, cgn_ref, hxn_ref,
         oatt_ref, x_ref, mod_ref, g_ref, gv_ref, ws_ref, bs_ref, wc_ref, wo_ref, wr_ref,
         x1_ref, h2_ref, route_ref) = refs
    else:
        (u_ref, vm_ref, bg_ref, cg_ref, hx_ref, cgp_ref, hxp_ref, cgn_ref, hxn_ref,
         oatt_ref, x_ref, mod_ref, g_ref, gv_ref, ws_ref, bs_ref, wc_ref, wo_ref,
         x1_ref, h2_ref) = refs
    i = pl.program_id(0)

    u = _gelu_tanh(u_ref[...])
    vm = _rms(_gelu_tanh(vm_ref[...]), gv_ref[...]).astype(BF16)
    chunks = []
    for c in range(TM // CHUNK):
        rows = slice(c * CHUNK, (c + 1) * CHUNK)
        heads = [jnp.dot(ws_ref[h].astype(BF16), vm[rows, h * HEAD_DIM:(h + 1) * HEAD_DIM],
                         preferred_element_type=F32) for h in range(H_CMLP)]
        chunks.append(jnp.concatenate(heads, axis=1) + bs_ref[...])
    o_mlp = u * jnp.concatenate(chunks, axis=0)

    j = (i - NT_P) % TILES_PER_DEC
    has_prev = jnp.logical_and(i >= NT_P, j > 0)
    has_next = jnp.logical_and(i >= NT_P, j < TILES_PER_DEC - 1)
    z = cg_ref[...] * hx_ref[...]
    z_halo_prev = jnp.where(has_prev, cgp_ref[7:8, :] * hxp_ref[7:8, :], 0.0)
    z_halo_next = jnp.where(has_next, cgn_ref[0:1, :] * hxn_ref[0:1, :], 0.0)
    row = lax.broadcasted_iota(jnp.int32, (TM, D_CONV), 0)
    z_prev = jnp.where(row == 0, z_halo_prev, pltpu.roll(z, 1, 0))
    z_next = jnp.where(row == TM - 1, z_halo_next, pltpu.roll(z, TM - 1, 0))
    y_conv = z_prev * wc_ref[0:1, :] + z * wc_ref[1:2, :] + z_next * wc_ref[2:3, :]
    o_conv = bg_ref[...] * y_conv

    y = jnp.dot(oatt_ref[...], wo_ref[0:D_ATT, :], preferred_element_type=F32)
    y = y + jnp.dot(o_mlp.astype(BF16), wo_ref[D_ATT:D_ATT + D_CMLP, :],
                    preferred_element_type=F32)
    y = y + jnp.dot(o_conv.astype(BF16), wo_ref[D_ATT + D_CMLP:, :],
                    preferred_element_type=F32)

    x1 = x_ref[...] + mod_ref[2:3, :] * _rms(y, g_ref[1:2, :])
    h2 = _rms(x1, g_ref[2:3, :]) * (1.0 + mod_ref[4:5, :]) + mod_ref[3:4, :]
    x1_ref[...] = x1
    if with_router:
        _store_row_tiled(h2_ref, h2, TM)
    else:
        h2_ref[...] = h2

    if with_router:
        logits = jnp.dot(h2, wr_ref[...], preferred_element_type=F32,
                         precision=lax.Precision.HIGHEST)
        lane = lax.broadcasted_iota(jnp.int32, logits.shape, 1).astype(F32)
        big = np.float32(1 << 20)
        lg = jnp.where(lane < N_EXPERTS, logits, -jnp.inf)
        m1 = jnp.max(lg, axis=-1, keepdims=True)
        i1 = jnp.min(jnp.where(lg == m1, lane, big), axis=-1, keepdims=True)
        lg2 = jnp.where(lane == i1, -jnp.inf, lg)
        m2 = jnp.max(lg2, axis=-1, keepdims=True)
        i2 = jnp.min(jnp.where(lg2 == m2, lane, big), axis=-1, keepdims=True)
        e2 = jnp.exp(m2 - m1)
        den = 1.0 + e2
        gate1 = 1.0 / den
        gate2 = e2 / den
        route = jnp.where(lane == 0, i1,
                          jnp.where(lane == 1, i2,
                                    jnp.where(lane == 2, gate1,
                                              jnp.where(lane == 3, gate2, 0.0))))
        route_ref[...] = route


def _mixer_out(p, o_att, x, mod, norm_g, g_v, w_s, bs_b, wc_t, w_out_bf, w_router_pad, l, moe_idx):
    with_router = moe_idx is not None
    hb = TM // 8

    def col(cb):
        return pl.BlockSpec((TM, D_CMLP), lambda i: (i, cb))

    def halo_prev(cb):
        return pl.BlockSpec((8, D_CONV), lambda i: (jnp.maximum(i * hb - 1, 0), cb))

    def halo_next(cb):
        return pl.BlockSpec((8, D_CONV), lambda i: (jnp.minimum((i + 1) * hb, T // 8 - 1), cb))

    in_specs = [col(6), col(7), col(8), col(9), col(10),
                halo_prev(9), halo_prev(10), halo_next(9), halo_next(10),
                pl.BlockSpec((TM, D_ATT), lambda i: (i, 0)),
                pl.BlockSpec((TM, D_MODEL), lambda i: (i, 0)),
                pl.BlockSpec((None, None, N_MOD, D_MODEL), lambda i: (l, _group_of_tile(i), 0, 0)),
                pl.BlockSpec((None, 4, D_MODEL), lambda i: (l, 0, 0)),
                pl.BlockSpec((None, 1, D_CMLP), lambda i: (l, 0, 0)),
                pl.BlockSpec((None, H_CMLP, CHUNK, CHUNK), lambda i: (l, 0, 0, 0)),
                pl.BlockSpec((None, CHUNK, D_CMLP), lambda i: (l, 0, 0)),
                pl.BlockSpec((None, 3, D_CONV), lambda i: (l, 0, 0)),
                pl.BlockSpec((None, D_MODEL, D_MODEL), lambda i: (l, 0, 0))]
    args = [p, p, p, p, p, p, p, p, p, o_att, x, mod, norm_g, g_v, w_s, bs_b, wc_t, w_out_bf]
    out_specs = [pl.BlockSpec((TM, D_MODEL), lambda i: (i, 0)),
                 pl.BlockSpec((TM, D_MODEL), lambda i: (i, 0))]
    out_shape = [jax.ShapeDtypeStruct((T, D_MODEL), F32), jax.ShapeDtypeStruct((T, D_MODEL), F32)]
    if with_router:
        in_specs.append(pl.BlockSpec((None, D_MODEL, 128), lambda i: (moe_idx, 0, 0)))
        args.append(w_router_pad)
        out_specs[1] = pl.BlockSpec((TM * ROW_SUB, LANES), lambda i: (i, 0))
        out_shape[1] = jax.ShapeDtypeStruct((T * ROW_SUB, LANES), F32)
        out_specs.append(pl.BlockSpec((TM, 128), lambda i: (i, 0)))
        out_shape.append(jax.ShapeDtypeStruct((T, 128), F32))
    return pl.pallas_call(
        functools.partial(_mixer_out_kernel, with_router),
        grid=(NT,),
        in_specs=in_specs,
        out_specs=out_specs,
        out_shape=out_shape,
        compiler_params=pltpu.CompilerParams(
            dimension_semantics=("parallel",), vmem_limit_bytes=VMEM_LIMIT),
        name="mixer_out_router" if with_router else "mixer_out",
    )(*args)


LANES = 128
ROW_SUB = D_MODEL // LANES


def _load_row_tiled(ref, n):
    return jnp.concatenate([ref[pl.ds(c, n, stride=ROW_SUB), :] for c in range(ROW_SUB)], axis=1)


def _store_row_tiled(ref, val, n):
    for c in range(ROW_SUB):
        ref[pl.ds(c, n, stride=ROW_SUB), :] = val[:, c * LANES:(c + 1) * LANES]


def _ffn_kernel(row_tiled, te_ref, nv_ref, et_ref, x_ref, wg_ref, wu_ref, wd_ref, o_ref,
                xb_ref, acc_ref):
    del te_ref, et_ref
    t = pl.program_id(0)
    j = pl.program_id(1)
    nvalid = nv_ref[t]
    nsub = (nvalid + (FF_SUB - 1)) // FF_SUB

    @pl.when(j == 0)
    def _():
        acc_ref[...] = jnp.zeros((FF_TILE, D_MODEL), F32)

    @pl.when(jnp.logical_and(j == 0, nvalid >= 0))
    def _():
        if row_tiled:
            xb_ref[...] = _load_row_tiled(x_ref, FF_TILE).astype(BF16)
        else:
            xb_ref[...] = x_ref[...].astype(BF16)

    wg = wg_ref[...].astype(BF16)
    wu = wu_ref[...].astype(BF16)
    wd = wd_ref[...].astype(BF16)

    def swiglu_rows(n):
        xs = xb_ref[0:n, :]
        g = jnp.dot(xs, wg, preferred_element_type=F32)
        u = jnp.dot(xs, wu, preferred_element_type=F32)
        a = (_silu(g) * u).astype(BF16)
        acc_ref[0:n, :] += jnp.dot(a, wd, preferred_element_type=F32)

    if row_tiled:
        for k in range(1, FF_TILE // FF_SUB + 1):
            pl.when(nsub == k)(functools.partial(swiglu_rows, k * FF_SUB))
    else:
        swiglu_rows(FF_TILE)

    @pl.when(j == N_FF_CH - 1)
    def _():
        if row_tiled:
            _store_row_tiled(o_ref, acc_ref[...], FF_TILE)
        else:
            o_ref[...] = acc_ref[...]


def _ffn(x_rows, w_gu, w_d, tile_expert, tile_nvalid, tile_eff, n_tiles, row_tiled, name):
    def chunk(t, j, nv):
        return jnp.where(nv[t] >= 0, j, N_FF_CH - 1)

    if row_tiled:
        x_block = (FF_TILE * ROW_SUB, LANES)
        out_shape = jax.ShapeDtypeStruct((n_tiles * FF_TILE * ROW_SUB, LANES), F32)
    else:
        x_block = (FF_TILE, D_MODEL)
        out_shape = jax.ShapeDtypeStruct((n_tiles * FF_TILE, D_MODEL), F32)
    grid_spec = pltpu.PrefetchScalarGridSpec(
        num_scalar_prefetch=3,
        grid=(n_tiles, N_FF_CH),
        in_specs=[
            pl.BlockSpec(x_block, lambda t, j, te, nv, et: (et[t], 0)),
            pl.BlockSpec((None, D_MODEL, FF_CH),
                         lambda t, j, te, nv, et: (te[t], 0, chunk(t, j, nv))),
            pl.BlockSpec((None, D_MODEL, FF_CH),
                         lambda t, j, te, nv, et: (te[t], 0, N_FF_CH + chunk(t, j, nv))),
            pl.BlockSpec((None, FF_CH, D_MODEL),
                         lambda t, j, te, nv, et: (te[t], chunk(t, j, nv), 0)),
        ],
        out_specs=pl.BlockSpec(x_block, lambda t, j, te, nv, et: (t, 0)),
        scratch_shapes=[pltpu.VMEM((FF_TILE, D_MODEL), BF16),
                        pltpu.VMEM((FF_TILE, D_MODEL), F32)],
    )
    return pl.pallas_call(
        functools.partial(_ffn_kernel, row_tiled),
        grid_spec=grid_spec,
        out_shape=out_shape,
        compiler_params=pltpu.CompilerParams(
            dimension_semantics=("arbitrary", "arbitrary"), vmem_limit_bytes=VMEM_LIMIT),
        name=name,
    )(tile_expert, tile_nvalid, tile_eff, x_rows, w_gu, w_gu, w_d)


def _gather_kernel(idx_ref, src_ref, out_ref, sem):
    i = pl.program_id(0)
    base = i * GATHER_CH

    def row_copy(r):
        src_row = pl.multiple_of(idx_ref[0, r] * ROW_SUB, ROW_SUB)
        dst_row = pl.multiple_of((base + r) * ROW_SUB, ROW_SUB)
        return pltpu.make_async_copy(src_ref.at[pl.ds(src_row, ROW_SUB)],
                                     out_ref.at[pl.ds(dst_row, ROW_SUB)], sem)

    def start(r, carry):
        row_copy(r).start()
        return carry

    def wait(r, carry):
        row_copy(r).wait()
        return carry

    lax.fori_loop(0, GATHER_CH, start, 0)
    lax.fori_loop(0, GATHER_CH, wait, 0)


def _gather_rows(src, idx, name):
    n = idx.shape[0]
    steps = n // GATHER_CH
    return pl.pallas_call(
        _gather_kernel,
        grid=(steps,),
        in_specs=[pl.BlockSpec((None, 1, GATHER_CH), lambda i: (i, 0, 0),
                               memory_space=pltpu.SMEM),
                  pl.BlockSpec(memory_space=pl.ANY)],
        out_specs=pl.BlockSpec(memory_space=pl.ANY),
        out_shape=jax.ShapeDtypeStruct((n * ROW_SUB, LANES), src.dtype),
        scratch_shapes=[pltpu.SemaphoreType.DMA(())],
        compiler_params=pltpu.CompilerParams(dimension_semantics=("arbitrary",)),
        name=name,
    )(idx.reshape(steps, 1, GATHER_CH), src)


def _ffn_out_dense_kernel(x1_ref, f_ref, mod_ref, g_ref, o_ref):
    o_ref[...] = x1_ref[...] + mod_ref[5:6, :] * _rms(f_ref[...], g_ref[3:4, :])


def _ffn_out_moe_kernel(x1_ref, y1_ref, y2_ref, route_ref, mod_ref, g_ref, o_ref):
    f = (route_ref[:, 2:3] * _load_row_tiled(y1_ref, TM)
         + route_ref[:, 3:4] * _load_row_tiled(y2_ref, TM))
    o_ref[...] = x1_ref[...] + mod_ref[5:6, :] * _rms(f, g_ref[3:4, :])


def _ffn_out(x1, ys, route, mod, norm_g, l):
    row = pl.BlockSpec((TM, D_MODEL), lambda i: (i, 0))
    mod_spec = pl.BlockSpec((None, None, N_MOD, D_MODEL), lambda i: (l, _group_of_tile(i), 0, 0))
    g_spec = pl.BlockSpec((None, 4, D_MODEL), lambda i: (l, 0, 0))
    if route is None:
        kern, name = _ffn_out_dense_kernel, "ffn_out_dense"
        in_specs = [row, row, mod_spec, g_spec]
        args = (x1, ys, mod, norm_g)
    else:
        kern, name = _ffn_out_moe_kernel, "ffn_out_moe"
        in_specs = [row, pl.BlockSpec((TM * ROW_SUB, LANES), lambda i: (i, 0)),
                    pl.BlockSpec((TM * ROW_SUB, LANES), lambda i: (i + NT, 0)),
                    pl.BlockSpec((TM, 128), lambda i: (i, 0)), mod_spec, g_spec]
        args = (x1, ys, ys, route, mod, norm_g)
    return pl.pallas_call(
        kern,
        grid=(NT,),
        in_specs=in_specs,
        out_specs=row,
        out_shape=jax.ShapeDtypeStruct((T, D_MODEL), F32),
        compiler_params=pltpu.CompilerParams(
            dimension_semantics=("parallel",), vmem_limit_bytes=VMEM_LIMIT),
        name=name,
    )(*args)


def _route_plan(route):
    e1 = route[:, 0].astype(jnp.int32)
    e2 = route[:, 1].astype(jnp.int32)
    ar = jnp.arange(N_EXPERTS, dtype=jnp.int32)
    oh1 = (e1[:, None] == ar[None, :]).astype(jnp.int32)
    oh2 = (e2[:, None] == ar[None, :]).astype(jnp.int32)
    oh = oh1 + oh2
    csum_incl = jnp.cumsum(oh, axis=0)
    csum = csum_incl - oh
    counts = csum_incl[-1]
    ntile_e = (counts + FF_TILE - 1) // FF_TILE
    tile_end_e = jnp.cumsum(ntile_e)
    tile_start_e = tile_end_e - ntile_e
    gstart = tile_start_e * FF_TILE
    pos1 = jnp.sum((gstart[None, :] + csum) * oh1, axis=1)
    pos2 = jnp.sum((gstart[None, :] + csum) * oh2, axis=1)
    tok = jnp.arange(T, dtype=jnp.int32)
    src = jnp.zeros((R_MOE,), jnp.int32).at[pos1].set(tok).at[pos2].set(tok)
    n_used = tile_end_e[-1]
    tiles = jnp.arange(NT_MOE, dtype=jnp.int32)
    eff = jnp.minimum(tiles, n_used - 1)
    te = jnp.sum((eff[:, None] >= tile_end_e[None, :]).astype(jnp.int32), axis=1)
    te = jnp.minimum(te, N_EXPERTS - 1)
    nvalid = jnp.clip(counts[te] - (eff - tile_start_e[te]) * FF_TILE, 0, FF_TILE)
    nvalid = jnp.where(tiles < n_used, nvalid, -1)
    return src, pos1, pos2, te.astype(jnp.int32), nvalid.astype(jnp.int32), eff.astype(jnp.int32)


def kernel(x_prompt, x_sample, cache_k, cache_v, c, c_ctx, w_ada, b_ada, norm_g, w_in, w_out,
           rpb, g_v, w_s, b_s, w_conv, w_ffn_gu, w_ffn_d, w_router, w_moe_gu, w_moe_d):
    x = jnp.concatenate([x_prompt.reshape(T_P, D_MODEL), x_sample.reshape(T_S, D_MODEL)], axis=0)
    cvec = jnp.concatenate([c_ctx[None], c, jnp.zeros((GROUPS - 1 - DEC_BATCH, D_MODEL), F32)],
                           axis=0)
    mod = _modulation(cvec, w_ada, b_ada)

    w_in_bf = w_in.astype(BF16)
    w_out_bf = w_out.astype(BF16)
    ck_all = cache_k.reshape(DEC_BATCH, DEPTH, PAST_LEN, D_ATT)
    cv_all = cache_v.reshape(DEC_BATCH, DEPTH, PAST_LEN, D_ATT)
    bias_all = _na_bias_tables(rpb)
    g_v3 = g_v.reshape(DEPTH, 1, D_CMLP)
    bs_b = jnp.repeat(jnp.swapaxes(b_s, 1, 2), HEAD_DIM, axis=2)
    wc_t = jnp.swapaxes(w_conv, 1, 2)
    w_router_pad = jnp.pad(w_router, ((0, 0), (0, 0), (0, 128 - N_EXPERTS)))
    w_moe_gu_all = w_moe_gu.reshape(-1, D_MODEL, 2 * D_FF)
    w_moe_d_all = w_moe_d.reshape(-1, D_FF, D_MODEL)

    dense_te = jnp.zeros((NT_DENSE,), jnp.int32)
    dense_nv = jnp.full((NT_DENSE,), FF_TILE, jnp.int32)
    dense_eff = jnp.arange(NT_DENSE, dtype=jnp.int32)

    ks, vs = [], []
    for l in range(DEPTH):
        p = _inproj(x, mod, norm_g, w_in_bf, l)
        ks.append(p[:T_P, D_ATT:2 * D_ATT])
        vs.append(p[:T_P, 2 * D_ATT:3 * D_ATT])
        o_att = _attention(p, ck_all, cv_all, bias_all, l)
        if l % 2 == 0:
            x1, h2 = _mixer_out(p, o_att, x, mod, norm_g, g_v3, w_s, bs_b, wc_t, w_out_bf,
                                None, l, None)
            f = _ffn(h2, w_ffn_gu, w_ffn_d, dense_te + l // 2, dense_nv, dense_eff, NT_DENSE,
                     False, "ffn_dense")
            x = _ffn_out(x1, f, None, mod, norm_g, l)
        else:
            x1, h2, route = _mixer_out(p, o_att, x, mod, norm_g, g_v3, w_s, bs_b, wc_t, w_out_bf,
                                       w_router_pad, l, l // 2)
            src, pos1, pos2, te, nv, eff = _route_plan(route)
            xs = _gather_rows(h2, src, "moe_dispatch")
            ys = _ffn(xs, w_moe_gu_all, w_moe_d_all, te + (l // 2) * N_EXPERTS, nv, eff, NT_MOE,
                      True, "ffn_moe")
            y12 = _gather_rows(ys, jnp.concatenate([pos1, pos2]), "moe_combine")
            x = _ffn_out(x1, y12, route, mod, norm_g, l)

    new_k = jnp.stack(ks, axis=0).reshape(DEPTH, BATCH, SEQ, H_ATT, HEAD_DIM).transpose(1, 0, 2, 3, 4)
    new_v = jnp.stack(vs, axis=0).reshape(DEPTH, BATCH, SEQ, H_ATT, HEAD_DIM).transpose(1, 0, 2, 3, 4)
    y_prompt = x[:T_P].reshape(BATCH, SEQ, D_MODEL)
    y_sample = x[T_P:].reshape(DEC_BATCH, DEC_SEQ, D_MODEL)
    return y_prompt, y_sample, new_k, new_v
```

```python
import functools

import numpy as np
import jax
import jax.numpy as jnp
from jax import lax
from jax.experimental import pallas as pl
from jax.experimental.pallas import tpu as pltpu

F32 = jnp.float32
BF16 = jnp.bfloat16

D_MODEL = 1024
BATCH = 16
SEQ = 256
DEPTH = 4
DEC_BATCH = 4
DEC_SEQ = 2048
PAST_LEN = 256
GRID_W = 64
HEAD_DIM = 64
D_ATT = 512
D_CMLP = 256
D_CONV = 256
H_ATT = 8
H_CMLP = 4
CHUNK = 128
NA_KH = 8
NA_KW = 16
D_IN = 2816
D_FF = 2816
N_EXPERTS = 8
N_MOD = 6
EPS = 1e-6
NEG_INF = -1e30

T_P = BATCH * SEQ
T_S = DEC_BATCH * DEC_SEQ
T = T_P + T_S
TM = 256
NT = T // TM
NT_P = T_P // TM
TILES_PER_DEC = DEC_SEQ // TM
GROUPS = 8

ROWS = DEC_SEQ // GRID_W
QROWS = TM // GRID_W
BAND_TILES = 3
BAND = BAND_TILES * TM

FF_TILE = 1024
FF_SUB = 256
FF_CH = 256
N_FF_CH = D_FF // FF_CH
R_MOE = 2 * T + N_EXPERTS * FF_TILE
NT_MOE = R_MOE // FF_TILE
NT_DENSE = T // FF_TILE
GATHER_CH = 512

VMEM_LIMIT = 56 * 1024 * 1024


def _group_of_tile(i):
    return jnp.where(i < NT_P, 0, 1 + (i - NT_P) // TILES_PER_DEC)


def _rms(x, g):
    return x * lax.rsqrt(jnp.mean(x * x, axis=-1, keepdims=True) + EPS) * g


def _silu(x):
    return x / (1.0 + jnp.exp(-x))


def _gelu_tanh(x):
    c = np.float32(np.sqrt(2.0 / np.pi))
    return 0.5 * x * (1.0 + jnp.tanh(c * (x + np.float32(0.044715) * (x * x * x))))


def _mod_kernel(cv_ref, w_ref, b_ref, o_ref):
    a = _silu(cv_ref[...])
    o_ref[...] = jnp.dot(a, w_ref[...], preferred_element_type=F32,
                         precision=lax.Precision.HIGHEST) + b_ref[...]


def _modulation(cvec, w_ada, b_ada):
    tn = 1536
    nn = (N_MOD * D_MODEL) // tn
    out = pl.pallas_call(
        _mod_kernel,
        grid=(DEPTH, nn),
        in_specs=[
            pl.BlockSpec((GROUPS, D_MODEL), lambda l, n: (0, 0)),
            pl.BlockSpec((None, D_MODEL, tn), lambda l, n: (l, 0, n)),
            pl.BlockSpec((None, 1, tn), lambda l, n: (l, 0, n)),
        ],
        out_specs=pl.BlockSpec((None, GROUPS, tn), lambda l, n: (l, 0, n)),
        out_shape=jax.ShapeDtypeStruct((DEPTH, GROUPS, N_MOD * D_MODEL), F32),
        compiler_params=pltpu.CompilerParams(
            dimension_semantics=("parallel", "parallel"), vmem_limit_bytes=VMEM_LIMIT),
        name="adaln_mod",
    )(cvec, w_ada, b_ada.reshape(DEPTH, 1, N_MOD * D_MODEL))
    return out.reshape(DEPTH, GROUPS, N_MOD, D_MODEL)


def _inproj_kernel(x_ref, mod_ref, g_ref, w_ref, kc_in_ref, vc_in_ref, p_ref, kc_ref, vc_ref):
    del kc_in_ref, vc_in_ref
    h = _rms(x_ref[...], g_ref[0:1, :]) * (1.0 + mod_ref[1:2, :]) + mod_ref[0:1, :]
    p_ref[...] = jnp.dot(h.astype(BF16), w_ref[...], preferred_element_type=F32)

    @pl.when(pl.program_id(0) < NT_P)
    def _():
        for hd in range(H_ATT):
            lo = hd * HEAD_DIM
            kc_ref[:, hd, :] = p_ref[:, D_ATT + lo:D_ATT + lo + HEAD_DIM]
            vc_ref[:, hd, :] = p_ref[:, 2 * D_ATT + lo:2 * D_ATT + lo + HEAD_DIM]


def _inproj(x, mod, norm_g, w_in_bf, new_k, new_v, l):
    cache_spec = pl.BlockSpec((None, None, SEQ, H_ATT, HEAD_DIM),
                              lambda i: (jnp.minimum(i, NT_P - 1), l, 0, 0, 0))
    return pl.pallas_call(
        _inproj_kernel,
        grid=(NT,),
        in_specs=[
            pl.BlockSpec((TM, D_MODEL), lambda i: (i, 0)),
            pl.BlockSpec((None, None, N_MOD, D_MODEL), lambda i: (l, _group_of_tile(i), 0, 0)),
            pl.BlockSpec((None, 4, D_MODEL), lambda i: (l, 0, 0)),
            pl.BlockSpec((None, D_MODEL, D_IN), lambda i: (l, 0, 0)),
            pl.BlockSpec(memory_space=pl.ANY),
            pl.BlockSpec(memory_space=pl.ANY),
        ],
        out_specs=[pl.BlockSpec((TM, D_IN), lambda i: (i, 0)), cache_spec, cache_spec],
        out_shape=[jax.ShapeDtypeStruct((T, D_IN), F32),
                   jax.ShapeDtypeStruct(new_k.shape, F32),
                   jax.ShapeDtypeStruct(new_v.shape, F32)],
        input_output_aliases={4: 1, 5: 2},
        compiler_params=pltpu.CompilerParams(
            dimension_semantics=("arbitrary",), vmem_limit_bytes=VMEM_LIMIT),
        name="inproj",
    )(x, mod, norm_g, w_in_bf, new_k, new_v)


def _dot_nt(a, b):
    return lax.dot_general(a, b, (((1,), (1,)), ((), ())), preferred_element_type=F32)


def _attn_kernel(q_ref, k0_ref, k1_ref, k2_ref, v0_ref, v1_ref, v2_ref,
                 ck_ref, cv_ref, bias_ref, o_ref):
    is_prompt = pl.program_id(0) < NT_P

    @pl.when(is_prompt)
    def _():
        _ctx_attn_body(q_ref, k0_ref, v0_ref, o_ref)

    @pl.when(jnp.logical_not(is_prompt))
    def _():
        _na_attn_body(q_ref, k0_ref, k1_ref, k2_ref, v0_ref, v1_ref, v2_ref,
                      ck_ref, cv_ref, bias_ref, o_ref)


def _ctx_attn_body(q_ref, k_ref, v_ref, o_ref):
    scale = np.float32(HEAD_DIM ** -0.5)
    for h in range(H_ATT):
        sl = slice(h * HEAD_DIM, (h + 1) * HEAD_DIM)
        q = (q_ref[:, sl] * scale).astype(BF16)
        k = k_ref[:, sl].astype(BF16)
        v = v_ref[:, sl].astype(BF16)
        s = _dot_nt(q, k)
        m = jnp.max(s, axis=-1, keepdims=True)
        e = jnp.exp(s - m)
        den = jnp.sum(e, axis=-1, keepdims=True)
        o = jnp.dot(e.astype(BF16), v, preferred_element_type=F32) / den
        o_ref[:, sl] = o.astype(BF16)


def _na_attn_body(q_ref, k0_ref, k1_ref, k2_ref, v0_ref, v1_ref, v2_ref,
                  ck_ref, cv_ref, bias_ref, o_ref):
    scale = np.float32(HEAD_DIM ** -0.5)
    k_refs = (k0_ref, k1_ref, k2_ref)
    v_refs = (v0_ref, v1_ref, v2_ref)
    for h in range(H_ATT):
        sl = slice(h * HEAD_DIM, (h + 1) * HEAD_DIM)
        q = (q_ref[:, sl] * scale).astype(BF16)
        s_loc = [_dot_nt(q, k_refs[j][:, sl].astype(BF16)) + bias_ref[h, :, j * TM:(j + 1) * TM]
                 for j in range(BAND_TILES)]
        s_ctx = _dot_nt(q, ck_ref[:, sl].astype(BF16))
        m = jnp.max(s_ctx, axis=-1, keepdims=True)
        for s in s_loc:
            m = jnp.maximum(m, jnp.max(s, axis=-1, keepdims=True))
        e_ctx = jnp.exp(s_ctx - m)
        den = jnp.sum(e_ctx, axis=-1, keepdims=True)
        acc = jnp.dot(e_ctx.astype(BF16), cv_ref[:, sl].astype(BF16), preferred_element_type=F32)
        for j in range(BAND_TILES):
            e = jnp.exp(s_loc[j] - m)
            den = den + jnp.sum(e, axis=-1, keepdims=True)
            acc = acc + jnp.dot(e.astype(BF16), v_refs[j][:, sl].astype(BF16),
                                preferred_element_type=F32)
        o_ref[:, sl] = (acc / den).astype(BF16)


def _na_variant_tables():
    kh = min(NA_KH, ROWS)
    per_tile = []
    for rb in range(ROWS // QROWS):
        r0 = rb * QROWS
        bs = int(np.clip(rb - 1, 0, ROWS // QROWS - BAND_TILES)) * QROWS
        tab = -np.ones((QROWS, BAND_TILES * QROWS), np.int32)
        for qr in range(QROWS):
            r = r0 + qr
            rs = int(np.clip(r - kh // 2, 0, ROWS - kh))
            for kr in range(BAND_TILES * QROWS):
                ka = bs + kr
                if rs <= ka < rs + kh:
                    tab[qr, kr] = ka - r + (NA_KH - 1)
            assert (tab[qr] >= 0).sum() == kh
        per_tile.append(tab)
    variants, variant_of_tile = [], []
    for tab in per_tile:
        for vi, v in enumerate(variants):
            if np.array_equal(v, tab):
                variant_of_tile.append(vi)
                break
        else:
            variants.append(tab)
            variant_of_tile.append(len(variants) - 1)
    qc = np.arange(GRID_W)[:, None]
    kc = np.arange(GRID_W)[None, :]
    cs = np.clip(qc - NA_KW // 2, 0, GRID_W - NA_KW)
    col_valid = (kc >= cs) & (kc < cs + NA_KW)
    dc_idx = np.clip(kc - qc + (NA_KW - 1), 0, 2 * NA_KW - 2).astype(np.int32)
    return np.asarray(variant_of_tile, np.int32), np.stack(variants), dc_idx, col_valid


_NA_VARIANT_OF_TILE, _NA_DR_IDX, _NA_DC_IDX, _NA_COL_VALID = _na_variant_tables()
_NA_NVAR = _NA_DR_IDX.shape[0]


def _na_bias_tables(rpb):
    small = jnp.where(_NA_COL_VALID[None, None, None],
                      jnp.take(rpb, jnp.asarray(_NA_DC_IDX), axis=3), NEG_INF)
    neg = jnp.full((DEPTH, H_ATT, GRID_W, GRID_W), NEG_INF, F32)
    per_var = []
    for vi in range(_NA_NVAR):
        q_rows = []
        for qr in range(QROWS):
            blocks = [small[:, :, int(d)] if d >= 0 else neg for d in _NA_DR_IDX[vi, qr]]
            q_rows.append(jnp.concatenate(blocks, axis=-1))
        per_var.append(jnp.concatenate(q_rows, axis=-2))
    return jnp.stack(per_var, axis=1)


def _attention(p, ck_all, cv_all, bias_all, l):
    nrb = TILES_PER_DEC
    var_of_tile = [int(v) for v in _NA_VARIANT_OF_TILE]

    def dec_batch(i):
        return jnp.maximum(i - NT_P, 0) // nrb

    def band_tile(i, j):
        rb = (i - NT_P) % nrb
        first = NT_P + dec_batch(i) * nrb + jnp.clip(rb - 1, 0, nrb - BAND_TILES)
        return jnp.where(i < NT_P, i, first + j)

    def variant(i):
        rb = jnp.maximum(i - NT_P, 0) % nrb
        v = jnp.int32(var_of_tile[0])
        for t in range(1, nrb):
            v = jnp.where(rb >= t, jnp.int32(var_of_tile[t]), v)
        return v

    kv_specs = [pl.BlockSpec((TM, D_ATT), functools.partial(
        lambda i, j, col: (band_tile(i, j), col), j=j, col=col))
        for col in (1, 2) for j in range(BAND_TILES)]
    return pl.pallas_call(
        _attn_kernel,
        grid=(NT,),
        in_specs=[pl.BlockSpec((TM, D_ATT), lambda i: (i, 0))] + kv_specs + [
            pl.BlockSpec((None, None, PAST_LEN, D_ATT), lambda i: (dec_batch(i), l, 0, 0)),
            pl.BlockSpec((None, None, PAST_LEN, D_ATT), lambda i: (dec_batch(i), l, 0, 0)),
            pl.BlockSpec((None, None, H_ATT, TM, BAND), lambda i: (l, variant(i), 0, 0, 0)),
        ],
        out_specs=pl.BlockSpec((TM, D_ATT), lambda i: (i, 0)),
        out_shape=jax.ShapeDtypeStruct((T, D_ATT), BF16),
        compiler_params=pltpu.CompilerParams(
            dimension_semantics=("arbitrary",), vmem_limit_bytes=VMEM_LIMIT),
        name="attn",
    )(p, p, p, p, p, p, p, ck_all, cv_all, bias_all)


def _mixer_out_kernel(with_router, *refs):
    if with_router:
        (u_ref, vm_ref, bg_ref, cg_ref, hx_ref, cgp_ref, hxp_ref, cgn_ref, hxn_ref,
         oatt_ref, x_ref, mod_ref, g_ref, gv_ref, ws_ref, bs_ref, wc_ref, wo_ref, wr_ref,
         x1_ref, h2_ref, route_ref) = refs
    else:
        (u_ref, vm_ref, bg_ref, cg_ref, hx_ref, cgp_ref, hxp_ref, cgn_ref, hxn_ref,
         oatt_ref, x_ref, mod_ref, g_ref, gv_ref, ws_ref, bs_ref, wc_ref, wo_ref,
         x1_ref, h2_ref) = refs
    i = pl.program_id(0)

    u = _gelu_tanh(u_ref[...])
    vm = _rms(_gelu_tanh(vm_ref[...]), gv_ref[...]).astype(BF16)
    chunks = []
    for c in range(TM // CHUNK):
        rows = slice(c * CHUNK, (c + 1) * CHUNK)
        heads = [jnp.dot(ws_ref[h].astype(BF16), vm[rows, h * HEAD_DIM:(h + 1) * HEAD_DIM],
                         preferred_element_type=F32) for h in range(H_CMLP)]
        chunks.append(jnp.concatenate(heads, axis=1) + bs_ref[...])
    o_mlp = u * jnp.concatenate(chunks, axis=0)

    j = (i - NT_P) % TILES_PER_DEC
    has_prev = jnp.logical_and(i >= NT_P, j > 0)
    has_next = jnp.logical_and(i >= NT_P, j < TILES_PER_DEC - 1)
    z = cg_ref[...] * hx_ref[...]
    z_halo_prev = jnp.where(has_prev, cgp_ref[7:8, :] * hxp_ref[7:8, :], 0.0)
    z_halo_next = jnp.where(has_next, cgn_ref[0:1, :] * hxn_ref[0:1, :], 0.0)
    row = lax.broadcasted_iota(jnp.int32, (TM, D_CONV), 0)
    z_prev = jnp.where(row == 0, z_halo_prev, pltpu.roll(z, 1, 0))
    z_next = jnp.where(row == TM - 1, z_halo_next, pltpu.roll(z, TM - 1, 0))
    y_conv = z_prev * wc_ref[0:1, :] + z * wc_ref[1:2, :] + z_next * wc_ref[2:3, :]
    o_conv = bg_ref[...] * y_conv

    y = jnp.dot(oatt_ref[...], wo_ref[0:D_ATT, :], preferred_element_type=F32)
    y = y + jnp.dot(o_mlp.astype(BF16), wo_ref[D_ATT:D_ATT + D_CMLP, :],
                    preferred_element_type=F32)
    y = y + jnp.dot(o_conv.astype(BF16), wo_ref[D_ATT + D_CMLP:, :],
                    preferred_element_type=F32)

    x1 = x_ref[...] + mod_ref[2:3, :] * _rms(y, g_ref[1:2, :])
    h2 = _rms(x1, g_ref[2:3, :]) * (1.0 + mod_ref[4:5, :]) + mod_ref[3:4, :]
    x1_ref[...] = x1
    if with_router:
        _store_row_tiled(h2_ref, h2, TM)
    else:
        h2_ref[...] = h2

    if with_router:
        logits = jnp.dot(h2, wr_ref[...], preferred_element_type=F32,
                         precision=lax.Precision.HIGHEST)
        lane = lax.broadcasted_iota(jnp.int32, logits.shape, 1).astype(F32)
        big = np.float32(1 << 20)
        lg = jnp.where(lane < N_EXPERTS, logits, -jnp.inf)
        m1 = jnp.max(lg, axis=-1, keepdims=True)
        i1 = jnp.min(jnp.where(lg == m1, lane, big), axis=-1, keepdims=True)
        lg2 = jnp.where(lane == i1, -jnp.inf, lg)
        m2 = jnp.max(lg2, axis=-1, keepdims=True)
        i2 = jnp.min(jnp.where(lg2 == m2, lane, big), axis=-1, keepdims=True)
        e2 = jnp.exp(m2 - m1)
        den = 1.0 + e2
        gate1 = 1.0 / den
        gate2 = e2 / den
        route = jnp.where(lane == 0, i1,
                          jnp.where(lane == 1, i2,
                                    jnp.where(lane == 2, gate1,
                                              jnp.where(lane == 3, gate2, 0.0))))
        route_ref[...] = route


def _mixer_out(p, o_att, x, mod, norm_g, g_v, w_s, bs_b, wc_t, w_out_bf, w_router_pad, l, moe_idx):
    with_router = moe_idx is not None
    hb = TM // 8

    def col(cb):
        return pl.BlockSpec((TM, D_CMLP), lambda i: (i, cb))

    def halo_prev(cb):
        return pl.BlockSpec((8, D_CONV), lambda i: (jnp.maximum(i * hb - 1, 0), cb))

    def halo_next(cb):
        return pl.BlockSpec((8, D_CONV), lambda i: (jnp.minimum((i + 1) * hb, T // 8 - 1), cb))

    in_specs = [col(6), col(7), col(8), col(9), col(10),
                halo_prev(9), halo_prev(10), halo_next(9), halo_next(10),
                pl.BlockSpec((TM, D_ATT), lambda i: (i, 0)),
                pl.BlockSpec((TM, D_MODEL), lambda i: (i, 0)),
                pl.BlockSpec((None, None, N_MOD, D_MODEL), lambda i: (l, _group_of_tile(i), 0, 0)),
                pl.BlockSpec((None, 4, D_MODEL), lambda i: (l, 0, 0)),
                pl.BlockSpec((None, 1, D_CMLP), lambda i: (l, 0, 0)),
                pl.BlockSpec((None, H_CMLP, CHUNK, CHUNK), lambda i: (l, 0, 0, 0)),
                pl.BlockSpec((None, CHUNK, D_CMLP), lambda i: (l, 0, 0)),
                pl.BlockSpec((None, 3, D_CONV), lambda i: (l, 0, 0)),
                pl.BlockSpec((None, D_MODEL, D_MODEL), lambda i: (l, 0, 0))]
    args = [p, p, p, p, p, p, p, p, p, o_att, x, mod, norm_g, g_v, w_s, bs_b, wc_t, w_out_bf]
    out_specs = [pl.BlockSpec((TM, D_MODEL), lambda i: (i, 0)),
                 pl.BlockSpec((TM, D_MODEL), lambda i: (i, 0))]
    out_shape = [jax.ShapeDtypeStruct((T, D_MODEL), F32), jax.ShapeDtypeStruct((T, D_MODEL), F32)]
    if with_router:
        in_specs.append(pl.BlockSpec((None, D_MODEL, 128), lambda i: (moe_idx, 0, 0)))
        args.append(w_router_pad)
        out_specs[1] = pl.BlockSpec((TM * ROW_SUB, LANES), lambda i: (i, 0))
        out_shape[1] = jax.ShapeDtypeStruct((T * ROW_SUB, LANES), F32)
        out_specs.append(pl.BlockSpec((TM, 128), lambda i: (i, 0)))
        out_shape.append(jax.ShapeDtypeStruct((T, 128), F32))
    return pl.pallas_call(
        functools.partial(_mixer_out_kernel, with_router),
        grid=(NT,),
        in_specs=in_specs,
        out_specs=out_specs,
        out_shape=out_shape,
        compiler_params=pltpu.CompilerParams(
            dimension_semantics=("parallel",), vmem_limit_bytes=VMEM_LIMIT),
        name="mixer_out_router" if with_router else "mixer_out",
    )(*args)


LANES = 128
ROW_SUB = D_MODEL // LANES


def _load_row_tiled(ref, n):
    return jnp.concatenate([ref[pl.ds(c, n, stride=ROW_SUB), :] for c in range(ROW_SUB)], axis=1)


def _store_row_tiled(ref, val, n):
    for c in range(ROW_SUB):
        ref[pl.ds(c, n, stride=ROW_SUB), :] = val[:, c * LANES:(c + 1) * LANES]


def _ffn_kernel(row_tiled, te_ref, nv_ref, et_ref, x_ref, wg_ref, wu_ref, wd_ref, *rest):
    if row_tiled:
        o_ref, xb_ref, acc_ref = rest
    else:
        x1_ref, mod_ref, g_ref, o_ref, xb_ref, acc_ref = rest
    del te_ref, et_ref
    t = pl.program_id(0)
    j = pl.program_id(1)
    nvalid = nv_ref[t]
    nsub = (nvalid + (FF_SUB - 1)) // FF_SUB

    @pl.when(j == 0)
    def _():
        acc_ref[...] = jnp.zeros((FF_TILE, D_MODEL), F32)

    @pl.when(jnp.logical_and(j == 0, nvalid >= 0))
    def _():
        if row_tiled:
            xb_ref[...] = _load_row_tiled(x_ref, FF_TILE).astype(BF16)
        else:
            xb_ref[...] = x_ref[...].astype(BF16)

    wg = wg_ref[...].astype(BF16)
    wu = wu_ref[...].astype(BF16)
    wd = wd_ref[...].astype(BF16)

    def swiglu_rows(n):
        xs = xb_ref[0:n, :]
        g = jnp.dot(xs, wg, preferred_element_type=F32)
        u = jnp.dot(xs, wu, preferred_element_type=F32)
        a = (_silu(g) * u).astype(BF16)
        acc_ref[0:n, :] += jnp.dot(a, wd, preferred_element_type=F32)

    if row_tiled:
        for k in range(1, FF_TILE // FF_SUB + 1):
            pl.when(nsub == k)(functools.partial(swiglu_rows, k * FF_SUB))
    else:
        swiglu_rows(FF_TILE)

    @pl.when(j == N_FF_CH - 1)
    def _():
        if row_tiled:
            _store_row_tiled(o_ref, acc_ref[...], FF_TILE)
        else:
            o_ref[...] = x1_ref[...] + mod_ref[5:6, :] * _rms(acc_ref[...], g_ref[3:4, :])


def _ffn(x_rows, w_gu, w_d, tile_expert, tile_nvalid, tile_eff, n_tiles, residual, name):
    def chunk(t, j, nv):
        return jnp.where(nv[t] >= 0, j, N_FF_CH - 1)

    row_tiled = residual is None
    if row_tiled:
        x_block = (FF_TILE * ROW_SUB, LANES)
        out_shape = jax.ShapeDtypeStruct((n_tiles * FF_TILE * ROW_SUB, LANES), F32)
        extra_specs, extra_args = [], ()
    else:
        x1, mod, norm_g, l = residual
        x_block = (FF_TILE, D_MODEL)
        out_shape = jax.ShapeDtypeStruct((n_tiles * FF_TILE, D_MODEL), F32)
        tiles_per_tm = FF_TILE // TM
        extra_specs = [
            pl.BlockSpec(x_block, lambda t, j, te, nv, et: (t, 0)),
            pl.BlockSpec((None, None, N_MOD, D_MODEL),
                         lambda t, j, te, nv, et: (l, _group_of_tile(t * tiles_per_tm), 0, 0)),
            pl.BlockSpec((None, 4, D_MODEL), lambda t, j, te, nv, et: (l, 0, 0)),
        ]
        extra_args = (x1, mod, norm_g)
    grid_spec = pltpu.PrefetchScalarGridSpec(
        num_scalar_prefetch=3,
        grid=(n_tiles, N_FF_CH),
        in_specs=[
            pl.BlockSpec(x_block, lambda t, j, te, nv, et: (et[t], 0)),
            pl.BlockSpec((None, D_MODEL, FF_CH),
                         lambda t, j, te, nv, et: (te[t], 0, chunk(t, j, nv))),
            pl.BlockSpec((None, D_MODEL, FF_CH),
                         lambda t, j, te, nv, et: (te[t], 0, N_FF_CH + chunk(t, j, nv))),
            pl.BlockSpec((None, FF_CH, D_MODEL),
                         lambda t, j, te, nv, et: (te[t], chunk(t, j, nv), 0)),
        ] + extra_specs,
        out_specs=pl.BlockSpec(x_block, lambda t, j, te, nv, et: (t, 0)),
        scratch_shapes=[pltpu.VMEM((FF_TILE, D_MODEL), BF16),
                        pltpu.VMEM((FF_TILE, D_MODEL), F32)],
    )
    return pl.pallas_call(
        functools.partial(_ffn_kernel, row_tiled),
        grid_spec=grid_spec,
        out_shape=out_shape,
        compiler_params=pltpu.CompilerParams(
            dimension_semantics=("arbitrary", "arbitrary"), vmem_limit_bytes=VMEM_LIMIT),
        name=name,
    )(tile_expert, tile_nvalid, tile_eff, x_rows, w_gu, w_gu, w_d, *extra_args)


def _gather_kernel(idx_ref, src_ref, out_ref, sem):
    def row_copy(r):
        src_row = pl.multiple_of(idx_ref[0, r] * ROW_SUB, ROW_SUB)
        dst_row = pl.multiple_of(r * ROW_SUB, ROW_SUB)
        return pltpu.make_async_copy(src_ref.at[pl.ds(src_row, ROW_SUB)],
                                     out_ref.at[pl.ds(dst_row, ROW_SUB)], sem)

    def start(r, carry):
        row_copy(r).start()
        return carry

    def wait(r, carry):
        row_copy(r).wait()
        return carry

    lax.fori_loop(0, GATHER_CH, start, 0)
    lax.fori_loop(0, GATHER_CH, wait, 0)


def _gather_rows(src, idx, name):
    n = idx.shape[0]
    steps = n // GATHER_CH
    return pl.pallas_call(
        _gather_kernel,
        grid=(steps,),
        in_specs=[pl.BlockSpec((None, 1, GATHER_CH), lambda i: (i, 0, 0),
                               memory_space=pltpu.SMEM),
                  pl.BlockSpec(memory_space=pl.ANY)],
        out_specs=pl.BlockSpec((GATHER_CH * ROW_SUB, LANES), lambda i: (i, 0)),
        out_shape=jax.ShapeDtypeStruct((n * ROW_SUB, LANES), src.dtype),
        scratch_shapes=[pltpu.SemaphoreType.DMA(())],
        compiler_params=pltpu.CompilerParams(
            dimension_semantics=("arbitrary",), vmem_limit_bytes=VMEM_LIMIT),
        name=name,
    )(idx.reshape(steps, 1, GATHER_CH), src)


def _ffn_out_moe_kernel(x1_ref, y1_ref, y2_ref, route_ref, mod_ref, g_ref, o_ref):
    f = (route_ref[:, 2:3] * _load_row_tiled(y1_ref, TM)
         + route_ref[:, 3:4] * _load_row_tiled(y2_ref, TM))
    o_ref[...] = x1_ref[...] + mod_ref[5:6, :] * _rms(f, g_ref[3:4, :])


def _ffn_out_moe(x1, ys, route, mod, norm_g, l):
    row = pl.BlockSpec((TM, D_MODEL), lambda i: (i, 0))
    return pl.pallas_call(
        _ffn_out_moe_kernel,
        grid=(NT,),
        in_specs=[row, pl.BlockSpec((TM * ROW_SUB, LANES), lambda i: (i, 0)),
                  pl.BlockSpec((TM * ROW_SUB, LANES), lambda i: (i + NT, 0)),
                  pl.BlockSpec((TM, 128), lambda i: (i, 0)),
                  pl.BlockSpec((None, None, N_MOD, D_MODEL),
                               lambda i: (l, _group_of_tile(i), 0, 0)),
                  pl.BlockSpec((None, 4, D_MODEL), lambda i: (l, 0, 0))],
        out_specs=row,
        out_shape=jax.ShapeDtypeStruct((T, D_MODEL), F32),
        compiler_params=pltpu.CompilerParams(
            dimension_semantics=("parallel",), vmem_limit_bytes=VMEM_LIMIT),
        name="ffn_out_moe",
    )(x1, ys, ys, route, mod, norm_g)


def _route_plan(route):
    e1 = route[:, 0].astype(jnp.int32)
    e2 = route[:, 1].astype(jnp.int32)
    ar = jnp.arange(N_EXPERTS, dtype=jnp.int32)
    oh1 = (e1[:, None] == ar[None, :]).astype(jnp.int32)
    oh2 = (e2[:, None] == ar[None, :]).astype(jnp.int32)
    oh = oh1 + oh2
    csum_incl = jnp.cumsum(oh, axis=0)
    csum = csum_incl - oh
    counts = csum_incl[-1]
    ntile_e = (counts + FF_TILE - 1) // FF_TILE
    tile_end_e = jnp.cumsum(ntile_e)
    tile_start_e = tile_end_e - ntile_e
    gstart = tile_start_e * FF_TILE
    pos1 = jnp.sum((gstart[None, :] + csum) * oh1, axis=1)
    pos2 = jnp.sum((gstart[None, :] + csum) * oh2, axis=1)
    tok = jnp.arange(T, dtype=jnp.int32)
    src = jnp.zeros((R_MOE,), jnp.int32).at[pos1].set(tok).at[pos2].set(tok)
    n_used = tile_end_e[-1]
    tiles = jnp.arange(NT_MOE, dtype=jnp.int32)
    eff = jnp.minimum(tiles, n_used - 1)
    te = jnp.sum((eff[:, None] >= tile_end_e[None, :]).astype(jnp.int32), axis=1)
    te = jnp.minimum(te, N_EXPERTS - 1)
    nvalid = jnp.clip(counts[te] - (eff - tile_start_e[te]) * FF_TILE, 0, FF_TILE)
    nvalid = jnp.where(tiles < n_used, nvalid, -1)
    return src, pos1, pos2, te.astype(jnp.int32), nvalid.astype(jnp.int32), eff.astype(jnp.int32)


def kernel(x_prompt, x_sample, cache_k, cache_v, c, c_ctx, w_ada, b_ada, norm_g, w_in, w_out,
           rpb, g_v, w_s, b_s, w_conv, w_ffn_gu, w_ffn_d, w_router, w_moe_gu, w_moe_d):
    x = jnp.concatenate([x_prompt.reshape(T_P, D_MODEL), x_sample.reshape(T_S, D_MODEL)], axis=0)
    cvec = jnp.concatenate([c_ctx[None], c, jnp.zeros((GROUPS - 1 - DEC_BATCH, D_MODEL), F32)],
                           axis=0)
    mod = _modulation(cvec, w_ada, b_ada)

    w_in_bf = w_in.astype(BF16)
    w_out_bf = w_out.astype(BF16)
    ck_all = cache_k.reshape(DEC_BATCH, DEPTH, PAST_LEN, D_ATT)
    cv_all = cache_v.reshape(DEC_BATCH, DEPTH, PAST_LEN, D_ATT)
    bias_all = _na_bias_tables(rpb)
    g_v3 = g_v.reshape(DEPTH, 1, D_CMLP)
    bs_b = jnp.repeat(jnp.swapaxes(b_s, 1, 2), HEAD_DIM, axis=2)
    wc_t = jnp.swapaxes(w_conv, 1, 2)
    w_router_pad = jnp.pad(w_router, ((0, 0), (0, 0), (0, 128 - N_EXPERTS)))
    w_moe_gu_all = w_moe_gu.reshape(-1, D_MODEL, 2 * D_FF)
    w_moe_d_all = w_moe_d.reshape(-1, D_FF, D_MODEL)

    dense_te = jnp.zeros((NT_DENSE,), jnp.int32)
    dense_nv = jnp.full((NT_DENSE,), FF_TILE, jnp.int32)
    dense_eff = jnp.arange(NT_DENSE, dtype=jnp.int32)

    new_k = jnp.zeros((BATCH, DEPTH, SEQ, H_ATT, HEAD_DIM), F32)
    new_v = jnp.zeros((BATCH, DEPTH, SEQ, H_ATT, HEAD_DIM), F32)
    for l in range(DEPTH):
        p, new_k, new_v = _inproj(x, mod, norm_g, w_in_bf, new_k, new_v, l)
        o_att = _attention(p, ck_all, cv_all, bias_all, l)
        if l % 2 == 0:
            x1, h2 = _mixer_out(p, o_att, x, mod, norm_g, g_v3, w_s, bs_b, wc_t, w_out_bf,
                                None, l, None)
            x = _ffn(h2, w_ffn_gu, w_ffn_d, dense_te + l // 2, dense_nv, dense_eff, NT_DENSE,
                     (x1, mod, norm_g, l), "ffn_dense")
        else:
            x1, h2, route = _mixer_out(p, o_att, x, mod, norm_g, g_v3, w_s, bs_b, wc_t, w_out_bf,
                                       w_router_pad, l, l // 2)
            src, pos1, pos2, te, nv, eff = _route_plan(route)
            xs = _gather_rows(h2, src, "moe_dispatch")
            ys = _ffn(xs, w_moe_gu_all, w_moe_d_all, te + (l // 2) * N_EXPERTS, nv, eff, NT_MOE,
                      None, "ffn_moe")
            y12 = _gather_rows(ys, jnp.concatenate([pos1, pos2]), "moe_combine")
            x = _ffn_out_moe(x1, y12, route, mod, norm_g, l)

    y_prompt = x[:T_P].reshape(BATCH, SEQ, D_MODEL)
    y_sample = x[T_P:].reshape(DEC_BATCH, DEC_SEQ, D_MODEL)
    return y_prompt, y_sample, new_k, new_v
```

```python
import functools

import numpy as np
import jax
import jax.numpy as jnp
from jax import lax
from jax.experimental import pallas as pl
from jax.experimental.pallas import tpu as pltpu

F32 = jnp.float32
BF16 = jnp.bfloat16

D_MODEL = 1024
BATCH = 16
SEQ = 256
DEPTH = 4
DEC_BATCH = 4
DEC_SEQ = 2048
PAST_LEN = 256
GRID_W = 64
HEAD_DIM = 64
D_ATT = 512
D_CMLP = 256
D_CONV = 256
H_ATT = 8
H_CMLP = 4
CHUNK = 128
NA_KH = 8
NA_KW = 16
D_IN = 2816
D_FF = 2816
N_EXPERTS = 8
N_MOD = 6
EPS = 1e-6
NEG_INF = -1e30

T_P = BATCH * SEQ
T_S = DEC_BATCH * DEC_SEQ
T = T_P + T_S
TM = 256
NT = T // TM
NT_P = T_P // TM
TILES_PER_DEC = DEC_SEQ // TM
GROUPS = 8

ROWS = DEC_SEQ // GRID_W
QROWS = TM // GRID_W
BAND_TILES = 3
BAND = BAND_TILES * TM

FF_TILE = 1024
FF_SUB = 256
FF_CH = 256
N_FF_CH = D_FF // FF_CH
R_MOE = 2 * T + N_EXPERTS * FF_TILE
NT_MOE = R_MOE // FF_TILE
NT_DENSE = T // FF_TILE
GATHER_CH = 512
GATHER_UNROLL = 16

VMEM_LIMIT = 56 * 1024 * 1024


def _group_of_tile(i):
    return jnp.where(i < NT_P, 0, 1 + (i - NT_P) // TILES_PER_DEC)


def _rms(x, g):
    return x * lax.rsqrt(jnp.mean(x * x, axis=-1, keepdims=True) + EPS) * g


def _silu(x):
    return x / (1.0 + jnp.exp(-x))


def _gelu_tanh(x):
    c = np.float32(np.sqrt(2.0 / np.pi))
    return 0.5 * x * (1.0 + jnp.tanh(c * (x + np.float32(0.044715) * (x * x * x))))


def _mod_kernel(cv_ref, w_ref, b_ref, o_ref):
    a = _silu(cv_ref[...])
    o_ref[...] = jnp.dot(a, w_ref[...], preferred_element_type=F32,
                         precision=lax.Precision.HIGHEST) + b_ref[...]


def _modulation(cvec, w_ada, b_ada):
    tn = 1536
    nn = (N_MOD * D_MODEL) // tn
    out = pl.pallas_call(
        _mod_kernel,
        grid=(DEPTH, nn),
        in_specs=[
            pl.BlockSpec((GROUPS, D_MODEL), lambda l, n: (0, 0)),
            pl.BlockSpec((None, D_MODEL, tn), lambda l, n: (l, 0, n)),
            pl.BlockSpec((None, 1, tn), lambda l, n: (l, 0, n)),
        ],
        out_specs=pl.BlockSpec((None, GROUPS, tn), lambda l, n: (l, 0, n)),
        out_shape=jax.ShapeDtypeStruct((DEPTH, GROUPS, N_MOD * D_MODEL), F32),
        compiler_params=pltpu.CompilerParams(
            dimension_semantics=("parallel", "parallel"), vmem_limit_bytes=VMEM_LIMIT),
        name="adaln_mod",
    )(cvec, w_ada, b_ada.reshape(DEPTH, 1, N_MOD * D_MODEL))
    return out.reshape(DEPTH, GROUPS, N_MOD, D_MODEL)


def _inproj_kernel(x_ref, mod_ref, g_ref, w_ref, kc_in_ref, vc_in_ref, p_ref, kc_ref, vc_ref):
    del kc_in_ref, vc_in_ref
    h = _rms(x_ref[...], g_ref[0:1, :]) * (1.0 + mod_ref[1:2, :]) + mod_ref[0:1, :]
    p_ref[...] = jnp.dot(h.astype(BF16), w_ref[...], preferred_element_type=F32)

    @pl.when(pl.program_id(0) < NT_P)
    def _():
        kc_ref[...] = p_ref[:, D_ATT:2 * D_ATT]
        vc_ref[...] = p_ref[:, 2 * D_ATT:3 * D_ATT]


def _inproj(x, mod, norm_g, w_in_bf, new_k, new_v, l):
    cache_spec = pl.BlockSpec((None, None, SEQ, D_ATT),
                              lambda i: (jnp.minimum(i, NT_P - 1), l, 0, 0))
    return pl.pallas_call(
        _inproj_kernel,
        grid=(NT,),
        in_specs=[
            pl.BlockSpec((TM, D_MODEL), lambda i: (i, 0)),
            pl.BlockSpec((None, None, N_MOD, D_MODEL), lambda i: (l, _group_of_tile(i), 0, 0)),
            pl.BlockSpec((None, 4, D_MODEL), lambda i: (l, 0, 0)),
            pl.BlockSpec((None, D_MODEL, D_IN), lambda i: (l, 0, 0)),
            pl.BlockSpec(memory_space=pl.ANY),
            pl.BlockSpec(memory_space=pl.ANY),
        ],
        out_specs=[pl.BlockSpec((TM, D_IN), lambda i: (i, 0)), cache_spec, cache_spec],
        out_shape=[jax.ShapeDtypeStruct((T, D_IN), F32),
                   jax.ShapeDtypeStruct(new_k.shape, F32),
                   jax.ShapeDtypeStruct(new_v.shape, F32)],
        input_output_aliases={4: 1, 5: 2},
        compiler_params=pltpu.CompilerParams(
            dimension_semantics=("arbitrary",), vmem_limit_bytes=VMEM_LIMIT),
        name="inproj",
    )(x, mod, norm_g, w_in_bf, new_k, new_v)


def _dot_nt(a, b):
    return lax.dot_general(a, b, (((1,), (1,)), ((), ())), preferred_element_type=F32)


def _attn_kernel(q_ref, k0_ref, k1_ref, k2_ref, v0_ref, v1_ref, v2_ref,
                 ck_ref, cv_ref, bias_ref, o_ref):
    is_prompt = pl.program_id(0) < NT_P

    @pl.when(is_prompt)
    def _():
        _ctx_attn_body(q_ref, k0_ref, v0_ref, o_ref)

    @pl.when(jnp.logical_not(is_prompt))
    def _():
        _na_attn_body(q_ref, k0_ref, k1_ref, k2_ref, v0_ref, v1_ref, v2_ref,
                      ck_ref, cv_ref, bias_ref, o_ref)


def _ctx_attn_body(q_ref, k_ref, v_ref, o_ref):
    scale = np.float32(HEAD_DIM ** -0.5)
    for h in range(H_ATT):
        sl = slice(h * HEAD_DIM, (h + 1) * HEAD_DIM)
        q = (q_ref[:, sl] * scale).astype(BF16)
        k = k_ref[:, sl].astype(BF16)
        v = v_ref[:, sl].astype(BF16)
        s = _dot_nt(q, k)
        m = jnp.max(s, axis=-1, keepdims=True)
        e = jnp.exp(s - m)
        den = jnp.sum(e, axis=-1, keepdims=True)
        o = jnp.dot(e.astype(BF16), v, preferred_element_type=F32) / den
        o_ref[:, sl] = o.astype(BF16)


def _na_attn_body(q_ref, k0_ref, k1_ref, k2_ref, v0_ref, v1_ref, v2_ref,
                  ck_ref, cv_ref, bias_ref, o_ref):
    scale = np.float32(HEAD_DIM ** -0.5)
    k_refs = (k0_ref, k1_ref, k2_ref)
    v_refs = (v0_ref, v1_ref, v2_ref)
    for h in range(H_ATT):
        sl = slice(h * HEAD_DIM, (h + 1) * HEAD_DIM)
        q = (q_ref[:, sl] * scale).astype(BF16)
        s_loc = [_dot_nt(q, k_refs[j][:, sl].astype(BF16)) + bias_ref[h, :, j * TM:(j + 1) * TM]
                 for j in range(BAND_TILES)]
        s_ctx = _dot_nt(q, ck_ref[:, sl].astype(BF16))
        m = jnp.max(s_ctx, axis=-1, keepdims=True)
        for s in s_loc:
            m = jnp.maximum(m, jnp.max(s, axis=-1, keepdims=True))
        e_ctx = jnp.exp(s_ctx - m)
        den = jnp.sum(e_ctx, axis=-1, keepdims=True)
        acc = jnp.dot(e_ctx.astype(BF16), cv_ref[:, sl].astype(BF16), preferred_element_type=F32)
        for j in range(BAND_TILES):
            e = jnp.exp(s_loc[j] - m)
            den = den + jnp.sum(e, axis=-1, keepdims=True)
            acc = acc + jnp.dot(e.astype(BF16), v_refs[j][:, sl].astype(BF16),
                                preferred_element_type=F32)
        o_ref[:, sl] = (acc / den).astype(BF16)


def _na_variant_tables():
    kh = min(NA_KH, ROWS)
    per_tile = []
    for rb in range(ROWS // QROWS):
        r0 = rb * QROWS
        bs = int(np.clip(rb - 1, 0, ROWS // QROWS - BAND_TILES)) * QROWS
        tab = -np.ones((QROWS, BAND_TILES * QROWS), np.int32)
        for qr in range(QROWS):
            r = r0 + qr
            rs = int(np.clip(r - kh // 2, 0, ROWS - kh))
            for kr in range(BAND_TILES * QROWS):
                ka = bs + kr
                if rs <= ka < rs + kh:
                    tab[qr, kr] = ka - r + (NA_KH - 1)
            assert (tab[qr] >= 0).sum() == kh
        per_tile.append(tab)
    variants, variant_of_tile = [], []
    for tab in per_tile:
        for vi, v in enumerate(variants):
            if np.array_equal(v, tab):
                variant_of_tile.append(vi)
                break
        else:
            variants.append(tab)
            variant_of_tile.append(len(variants) - 1)
    qc = np.arange(GRID_W)[:, None]
    kc = np.arange(GRID_W)[None, :]
    cs = np.clip(qc - NA_KW // 2, 0, GRID_W - NA_KW)
    col_valid = (kc >= cs) & (kc < cs + NA_KW)
    dc_idx = np.clip(kc - qc + (NA_KW - 1), 0, 2 * NA_KW - 2).astype(np.int32)
    return np.asarray(variant_of_tile, np.int32), np.stack(variants), dc_idx, col_valid


_NA_VARIANT_OF_TILE, _NA_DR_IDX, _NA_DC_IDX, _NA_COL_VALID = _na_variant_tables()
_NA_NVAR = _NA_DR_IDX.shape[0]


def _na_bias_tables(rpb):
    small = jnp.where(_NA_COL_VALID[None, None, None],
                      jnp.take(rpb, jnp.asarray(_NA_DC_IDX), axis=3), NEG_INF)
    neg = jnp.full((DEPTH, H_ATT, GRID_W, GRID_W), NEG_INF, F32)
    per_var = []
    for vi in range(_NA_NVAR):
        q_rows = []
        for qr in range(QROWS):
            blocks = [small[:, :, int(d)] if d >= 0 else neg for d in _NA_DR_IDX[vi, qr]]
            q_rows.append(jnp.stack(blocks, axis=2))
        per_var.append(jnp.stack(q_rows, axis=2))
    table = jnp.stack(per_var, axis=1)
    return table.transpose(0, 1, 2, 3, 5, 4, 6).reshape(DEPTH, _NA_NVAR, H_ATT, TM, BAND)


def _attention(p, ck_all, cv_all, bias_all, l):
    nrb = TILES_PER_DEC
    var_of_tile = [int(v) for v in _NA_VARIANT_OF_TILE]

    def dec_batch(i):
        return jnp.maximum(i - NT_P, 0) // nrb

    def band_tile(i, j):
        rb = (i - NT_P) % nrb
        first = NT_P + dec_batch(i) * nrb + jnp.clip(rb - 1, 0, nrb - BAND_TILES)
        return jnp.where(i < NT_P, i, first + j)

    def variant(i):
        rb = jnp.maximum(i - NT_P, 0) % nrb
        v = jnp.int32(var_of_tile[0])
        for t in range(1, nrb):
            v = jnp.where(rb >= t, jnp.int32(var_of_tile[t]), v)
        return v

    kv_specs = [pl.BlockSpec((TM, D_ATT), functools.partial(
        lambda i, j, col: (band_tile(i, j), col), j=j, col=col))
        for col in (1, 2) for j in range(BAND_TILES)]
    return pl.pallas_call(
        _attn_kernel,
        grid=(NT,),
        in_specs=[pl.BlockSpec((TM, D_ATT), lambda i: (i, 0))] + kv_specs + [
            pl.BlockSpec((None, None, PAST_LEN, D_ATT), lambda i: (dec_batch(i), l, 0, 0)),
            pl.BlockSpec((None, None, PAST_LEN, D_ATT), lambda i: (dec_batch(i), l, 0, 0)),
            pl.BlockSpec((None, None, H_ATT, TM, BAND), lambda i: (l, variant(i), 0, 0, 0)),
        ],
        out_specs=pl.BlockSpec((TM, D_ATT), lambda i: (i, 0)),
        out_shape=jax.ShapeDtypeStruct((T, D_ATT), BF16),
        compiler_params=pltpu.CompilerParams(
            dimension_semantics=("arbitrary",), vmem_limit_bytes=VMEM_LIMIT),
        name="attn",
    )(p, p, p, p, p, p, p, ck_all, cv_all, bias_all)


def _mixer_out_kernel(with_router, *refs):
    if with_router:
        (u_ref, vm_ref, bg_ref, cg_ref, hx_ref, cgp_ref, hxp_ref, cgn_ref, hxn_ref,
         oatt_ref, x_ref, mod_ref, g_ref, gv_ref, ws_ref, bs_ref, wc_ref, wo_ref, wr_ref,
         x1_ref, h2_ref, route_ref) = refs
    else:
        (u_ref, vm_ref, bg_ref, cg_ref, hx_ref, cgp_ref, hxp_ref, cgn_ref, hxn_ref,
         oatt_ref, x_ref, mod_ref, g_ref, gv_ref, ws_ref, bs_ref, wc_ref, wo_ref,
         x1_ref, h2_ref) = refs
    i = pl.program_id(0)

    u = _gelu_tanh(u_ref[...])
    vm = _rms(_gelu_tanh(vm_ref[...]), gv_ref[...]).astype(BF16)
    chunks = []
    for c in range(TM // CHUNK):
        rows = slice(c * CHUNK, (c + 1) * CHUNK)
        heads = [jnp.dot(ws_ref[h].astype(BF16), vm[rows, h * HEAD_DIM:(h + 1) * HEAD_DIM],
                         preferred_element_type=F32) for h in range(H_CMLP)]
        chunks.append(jnp.concatenate(heads, axis=1) + bs_ref[...])
    o_mlp = u * jnp.concatenate(chunks, axis=0)

    j = (i - NT_P) % TILES_PER_DEC
    has_prev = jnp.logical_and(i >= NT_P, j > 0)
    has_next = jnp.logical_and(i >= NT_P, j < TILES_PER_DEC - 1)
    z = cg_ref[...] * hx_ref[...]
    z_halo_prev = jnp.where(has_prev, cgp_ref[7:8, :] * hxp_ref[7:8, :], 0.0)
    z_halo_next = jnp.where(has_next, cgn_ref[0:1, :] * hxn_ref[0:1, :], 0.0)
    row = lax.broadcasted_iota(jnp.int32, (TM, D_CONV), 0)
    z_prev = jnp.where(row == 0, z_halo_prev, pltpu.roll(z, 1, 0))
    z_next = jnp.where(row == TM - 1, z_halo_next, pltpu.roll(z, TM - 1, 0))
    y_conv = z_prev * wc_ref[0:1, :] + z * wc_ref[1:2, :] + z_next * wc_ref[2:3, :]
    o_conv = bg_ref[...] * y_conv

    y = jnp.dot(oatt_ref[...], wo_ref[0:D_ATT, :], preferred_element_type=F32)
    y = y + jnp.dot(o_mlp.astype(BF16), wo_ref[D_ATT:D_ATT + D_CMLP, :],
                    preferred_element_type=F32)
    y = y + jnp.dot(o_conv.astype(BF16), wo_ref[D_ATT + D_CMLP:, :],
                    preferred_element_type=F32)

    x1 = x_ref[...] + mod_ref[2:3, :] * _rms(y, g_ref[1:2, :])
    h2 = _rms(x1, g_ref[2:3, :]) * (1.0 + mod_ref[4:5, :]) + mod_ref[3:4, :]
    x1_ref[...] = x1
    if with_router:
        _store_row_tiled(h2_ref, h2, TM)
    else:
        h2_ref[...] = h2

    if with_router:
        lg = [jnp.sum(h2 * wr_ref[e:e + 1, :], axis=-1, keepdims=True) for e in range(N_EXPERTS)]

        def top1(cols):
            m = cols[0]
            for col in cols[1:]:
                m = jnp.maximum(m, col)
            idx = jnp.full_like(m, N_EXPERTS - 1)
            for e in range(N_EXPERTS - 2, -1, -1):
                idx = jnp.where(cols[e] == m, np.float32(e), idx)
            return m, idx

        m1, i1 = top1(lg)
        m2, i2 = top1([jnp.where(i1 == np.float32(e), -jnp.inf, lg[e]) for e in range(N_EXPERTS)])
        lane = lax.broadcasted_iota(jnp.int32, (TM, 128), 1)
        e2 = jnp.exp(m2 - m1)
        den = 1.0 + e2
        gate1 = 1.0 / den
        gate2 = e2 / den
        route = jnp.where(lane == 0, i1,
                          jnp.where(lane == 1, i2,
                                    jnp.where(lane == 2, gate1,
                                              jnp.where(lane == 3, gate2, 0.0))))
        route_ref[...] = route


def _mixer_out(p, o_att, x, mod, norm_g, g_v, w_s, bs_b, wc_t, w_out_bf, w_router_t, l, moe_idx):
    with_router = moe_idx is not None
    hb = TM // 8

    def col(cb):
        return pl.BlockSpec((TM, D_CMLP), lambda i: (i, cb))

    def halo_prev(cb):
        return pl.BlockSpec((8, D_CONV), lambda i: (jnp.maximum(i * hb - 1, 0), cb))

    def halo_next(cb):
        return pl.BlockSpec((8, D_CONV), lambda i: (jnp.minimum((i + 1) * hb, T // 8 - 1), cb))

    in_specs = [col(6), col(7), col(8), col(9), col(10),
                halo_prev(9), halo_prev(10), halo_next(9), halo_next(10),
                pl.BlockSpec((TM, D_ATT), lambda i: (i, 0)),
                pl.BlockSpec((TM, D_MODEL), lambda i: (i, 0)),
                pl.BlockSpec((None, None, N_MOD, D_MODEL), lambda i: (l, _group_of_tile(i), 0, 0)),
                pl.BlockSpec((None, 4, D_MODEL), lambda i: (l, 0, 0)),
                pl.BlockSpec((None, 1, D_CMLP), lambda i: (l, 0, 0)),
                pl.BlockSpec((None, H_CMLP, CHUNK, CHUNK), lambda i: (l, 0, 0, 0)),
                pl.BlockSpec((None, CHUNK, D_CMLP), lambda i: (l, 0, 0)),
                pl.BlockSpec((None, 3, D_CONV), lambda i: (l, 0, 0)),
                pl.BlockSpec((None, D_MODEL, D_MODEL), lambda i: (l, 0, 0))]
    args = [p, p, p, p, p, p, p, p, p, o_att, x, mod, norm_g, g_v, w_s, bs_b, wc_t, w_out_bf]
    out_specs = [pl.BlockSpec((TM, D_MODEL), lambda i: (i, 0)),
                 pl.BlockSpec((TM, D_MODEL), lambda i: (i, 0))]
    out_shape = [jax.ShapeDtypeStruct((T, D_MODEL), F32), jax.ShapeDtypeStruct((T, D_MODEL), F32)]
    if with_router:
        in_specs.append(pl.BlockSpec((None, N_EXPERTS, D_MODEL), lambda i: (moe_idx, 0, 0)))
        args.append(w_router_t)
        out_specs[1] = pl.BlockSpec((TM * ROW_SUB, LANES), lambda i: (i, 0))
        out_shape[1] = jax.ShapeDtypeStruct((T * ROW_SUB, LANES), F32)
        out_specs.append(pl.BlockSpec((TM, 128), lambda i: (i, 0)))
        out_shape.append(jax.ShapeDtypeStruct((T, 128), F32))
    return pl.pallas_call(
        functools.partial(_mixer_out_kernel, with_router),
        grid=(NT,),
        in_specs=in_specs,
        out_specs=out_specs,
        out_shape=out_shape,
        compiler_params=pltpu.CompilerParams(
            dimension_semantics=("parallel",), vmem_limit_bytes=VMEM_LIMIT),
        name="mixer_out_router" if with_router else "mixer_out",
    )(*args)


LANES = 128
ROW_SUB = D_MODEL // LANES


def _load_row_tiled(ref, n):
    return jnp.concatenate([ref[pl.ds(c, n, stride=ROW_SUB), :] for c in range(ROW_SUB)], axis=1)


def _store_row_tiled(ref, val, n):
    for c in range(ROW_SUB):
        ref[pl.ds(c, n, stride=ROW_SUB), :] = val[:, c * LANES:(c + 1) * LANES]


def _ffn_kernel(row_tiled, te_ref, nv_ref, et_ref, x_ref, wg_ref, wu_ref, wd_ref, *rest):
    if row_tiled:
        o_ref, xb_ref, acc_ref = rest
    else:
        x1_ref, mod_ref, g_ref, o_ref, xb_ref, acc_ref = rest
    del te_ref, et_ref
    t = pl.program_id(0)
    j = pl.program_id(1)
    nvalid = nv_ref[t]
    nsub = (nvalid + (FF_SUB - 1)) // FF_SUB

    @pl.when(j == 0)
    def _():
        acc_ref[...] = jnp.zeros((FF_TILE, D_MODEL), F32)

    @pl.when(jnp.logical_and(j == 0, nvalid >= 0))
    def _():
        if row_tiled:
            xb_ref[...] = _load_row_tiled(x_ref, FF_TILE).astype(BF16)
        else:
            xb_ref[...] = x_ref[...].astype(BF16)

    def swiglu_rows(n):
        wg = wg_ref[...].astype(BF16)
        wu = wu_ref[...].astype(BF16)
        wd = wd_ref[...].astype(BF16)
        xs = xb_ref[0:n, :]
        g = jnp.dot(xs, wg, preferred_element_type=F32)
        u = jnp.dot(xs, wu, preferred_element_type=F32)
        a = (_silu(g) * u).astype(BF16)
        acc_ref[0:n, :] += jnp.dot(a, wd, preferred_element_type=F32)

    if row_tiled:
        for k in range(1, FF_TILE // FF_SUB + 1):
            pl.when(nsub == k)(functools.partial(swiglu_rows, k * FF_SUB))
    else:
        swiglu_rows(FF_TILE)

    @pl.when(j == N_FF_CH - 1)
    def _():
        if row_tiled:
            _store_row_tiled(o_ref, acc_ref[...], FF_TILE)
        else:
            o_ref[...] = x1_ref[...] + mod_ref[5:6, :] * _rms(acc_ref[...], g_ref[3:4, :])


def _ffn(x_rows, w_gu, w_d, tile_expert, tile_nvalid, tile_eff, n_tiles, residual, name):
    def chunk(t, j, nv):
        return jnp.where(nv[t] >= 0, j, N_FF_CH - 1)

    row_tiled = residual is None
    if row_tiled:
        x_block = (FF_TILE * ROW_SUB, LANES)
        out_shape = jax.ShapeDtypeStruct((n_tiles * FF_TILE * ROW_SUB, LANES), F32)
        extra_specs, extra_args = [], ()
    else:
        x1, mod, norm_g, l = residual
        x_block = (FF_TILE, D_MODEL)
        out_shape = jax.ShapeDtypeStruct((n_tiles * FF_TILE, D_MODEL), F32)
        tiles_per_tm = FF_TILE // TM
        extra_specs = [
            pl.BlockSpec(x_block, lambda t, j, te, nv, et: (t, 0)),
            pl.BlockSpec((None, None, N_MOD, D_MODEL),
                         lambda t, j, te, nv, et: (l, _group_of_tile(t * tiles_per_tm), 0, 0)),
            pl.BlockSpec((None, 4, D_MODEL), lambda t, j, te, nv, et: (l, 0, 0)),
        ]
        extra_args = (x1, mod, norm_g)
    grid_spec = pltpu.PrefetchScalarGridSpec(
        num_scalar_prefetch=3,
        grid=(n_tiles, N_FF_CH),
        in_specs=[
            pl.BlockSpec(x_block, lambda t, j, te, nv, et: (et[t], 0)),
            pl.BlockSpec((None, D_MODEL, FF_CH),
                         lambda t, j, te, nv, et: (te[t], 0, chunk(t, j, nv))),
            pl.BlockSpec((None, D_MODEL, FF_CH),
                         lambda t, j, te, nv, et: (te[t], 0, N_FF_CH + chunk(t, j, nv))),
            pl.BlockSpec((None, FF_CH, D_MODEL),
                         lambda t, j, te, nv, et: (te[t], chunk(t, j, nv), 0)),
        ] + extra_specs,
        out_specs=pl.BlockSpec(x_block, lambda t, j, te, nv, et: (t, 0)),
        scratch_shapes=[pltpu.VMEM((FF_TILE, D_MODEL), BF16),
                        pltpu.VMEM((FF_TILE, D_MODEL), F32)],
    )
    return pl.pallas_call(
        functools.partial(_ffn_kernel, row_tiled),
        grid_spec=grid_spec,
        out_shape=out_shape,
        compiler_params=pltpu.CompilerParams(
            dimension_semantics=("arbitrary", "arbitrary"), vmem_limit_bytes=VMEM_LIMIT),
        name=name,
    )(tile_expert, tile_nvalid, tile_eff, x_rows, w_gu, w_gu, w_d, *extra_args)


def _gather_kernel(idx_ref, src_ref, out_ref, sem):
    def row_copy(r):
        src_row = pl.multiple_of(idx_ref[0, r] * ROW_SUB, ROW_SUB)
        dst_row = pl.multiple_of(r * ROW_SUB, ROW_SUB)
        return pltpu.make_async_copy(src_ref.at[pl.ds(src_row, ROW_SUB)],
                                     out_ref.at[pl.ds(dst_row, ROW_SUB)], sem)

    def start_group(g, carry):
        for u in range(GATHER_UNROLL):
            row_copy(g * GATHER_UNROLL + u).start(priority=u % 2)
        return carry

    lax.fori_loop(0, GATHER_CH // GATHER_UNROLL, start_group, 0)
    pltpu.make_async_copy(src_ref.at[pl.ds(0, GATHER_CH * ROW_SUB)], out_ref, sem).wait()


def _gather_rows(src, idx, name):
    n = idx.shape[0]
    steps = n // GATHER_CH
    return pl.pallas_call(
        _gather_kernel,
        grid=(steps,),
        in_specs=[pl.BlockSpec((None, 1, GATHER_CH), lambda i: (i, 0, 0),
                               memory_space=pltpu.SMEM),
                  pl.BlockSpec(memory_space=pl.ANY)],
        out_specs=pl.BlockSpec((GATHER_CH * ROW_SUB, LANES), lambda i: (i, 0)),
        out_shape=jax.ShapeDtypeStruct((n * ROW_SUB, LANES), src.dtype),
        scratch_shapes=[pltpu.SemaphoreType.DMA(())],
        compiler_params=pltpu.CompilerParams(
            dimension_semantics=("arbitrary",), vmem_limit_bytes=VMEM_LIMIT),
        name=name,
    )(idx.reshape(steps, 1, GATHER_CH), src)


def _ffn_out_moe_kernel(x1_ref, y1_ref, y2_ref, route_ref, mod_ref, g_ref, o_ref):
    f = (route_ref[:, 2:3] * _load_row_tiled(y1_ref, TM)
         + route_ref[:, 3:4] * _load_row_tiled(y2_ref, TM))
    o_ref[...] = x1_ref[...] + mod_ref[5:6, :] * _rms(f, g_ref[3:4, :])


def _ffn_out_moe(x1, ys, route, mod, norm_g, l):
    row = pl.BlockSpec((TM, D_MODEL), lambda i: (i, 0))
    return pl.pallas_call(
        _ffn_out_moe_kernel,
        grid=(NT,),
        in_specs=[row, pl.BlockSpec((TM * ROW_SUB, LANES), lambda i: (i, 0)),
                  pl.BlockSpec((TM * ROW_SUB, LANES), lambda i: (i + NT, 0)),
                  pl.BlockSpec((TM, 128), lambda i: (i, 0)),
                  pl.BlockSpec((None, None, N_MOD, D_MODEL),
                               lambda i: (l, _group_of_tile(i), 0, 0)),
                  pl.BlockSpec((None, 4, D_MODEL), lambda i: (l, 0, 0))],
        out_specs=row,
        out_shape=jax.ShapeDtypeStruct((T, D_MODEL), F32),
        compiler_params=pltpu.CompilerParams(
            dimension_semantics=("parallel",), vmem_limit_bytes=VMEM_LIMIT),
        name="ffn_out_moe",
    )(x1, ys, ys, route, mod, norm_g)


def _route_plan(route):
    e1 = route[:, 0].astype(jnp.int32)
    e2 = route[:, 1].astype(jnp.int32)
    ar = jnp.arange(N_EXPERTS, dtype=jnp.int32)
    oh1 = (e1[:, None] == ar[None, :]).astype(jnp.int32)
    oh2 = (e2[:, None] == ar[None, :]).astype(jnp.int32)
    oh = oh1 + oh2
    csum_incl = jnp.cumsum(oh, axis=0)
    csum = csum_incl - oh
    counts = csum_incl[-1]
    ntile_e = (counts + FF_TILE - 1) // FF_TILE
    tile_end_e = jnp.cumsum(ntile_e)
    tile_start_e = tile_end_e - ntile_e
    gstart = tile_start_e * FF_TILE
    pos1 = jnp.sum((gstart[None, :] + csum) * oh1, axis=1)
    pos2 = jnp.sum((gstart[None, :] + csum) * oh2, axis=1)
    tok = jnp.arange(T, dtype=jnp.int32)
    src = (jnp.arange(R_MOE, dtype=jnp.int32) % T).at[pos1].set(tok).at[pos2].set(tok)
    n_used = tile_end_e[-1]
    tiles = jnp.arange(NT_MOE, dtype=jnp.int32)
    eff = jnp.minimum(tiles, n_used - 1)
    te = jnp.sum((eff[:, None] >= tile_end_e[None, :]).astype(jnp.int32), axis=1)
    te = jnp.minimum(te, N_EXPERTS - 1)
    nvalid = jnp.clip(counts[te] - (eff - tile_start_e[te]) * FF_TILE, 0, FF_TILE)
    nvalid = jnp.where(tiles < n_used, nvalid, -1)
    return src, pos1, pos2, te.astype(jnp.int32), nvalid.astype(jnp.int32), eff.astype(jnp.int32)


def kernel(x_prompt, x_sample, cache_k, cache_v, c, c_ctx, w_ada, b_ada, norm_g, w_in, w_out,
           rpb, g_v, w_s, b_s, w_conv, w_ffn_gu, w_ffn_d, w_router, w_moe_gu, w_moe_d):
    x = jnp.concatenate([x_prompt.reshape(T_P, D_MODEL), x_sample.reshape(T_S, D_MODEL)], axis=0)
    cvec = jnp.concatenate([c_ctx[None], c, jnp.zeros((GROUPS - 1 - DEC_BATCH, D_MODEL), F32)],
                           axis=0)
    mod = _modulation(cvec, w_ada, b_ada)

    w_in_bf = w_in.astype(BF16)
    w_out_bf = w_out.astype(BF16)
    ck_all = cache_k.reshape(DEC_BATCH, DEPTH, PAST_LEN, D_ATT)
    cv_all = cache_v.reshape(DEC_BATCH, DEPTH, PAST_LEN, D_ATT)
    bias_all = _na_bias_tables(rpb)
    g_v3 = g_v.reshape(DEPTH, 1, D_CMLP)
    bs_b = jnp.repeat(jnp.swapaxes(b_s, 1, 2), HEAD_DIM, axis=2)
    wc_t = jnp.swapaxes(w_conv, 1, 2)
    w_router_t = jnp.swapaxes(w_router, 1, 2)
    w_moe_gu_all = w_moe_gu.reshape(-1, D_MODEL, 2 * D_FF)
    w_moe_d_all = w_moe_d.reshape(-1, D_FF, D_MODEL)

    dense_te = jnp.zeros((NT_DENSE,), jnp.int32)
    dense_nv = jnp.full((NT_DENSE,), FF_TILE, jnp.int32)
    dense_eff = jnp.arange(NT_DENSE, dtype=jnp.int32)

    new_k = jnp.zeros((BATCH, DEPTH, SEQ, D_ATT), F32)
    new_v = jnp.zeros((BATCH, DEPTH, SEQ, D_ATT), F32)
    for l in range(DEPTH):
        p, new_k, new_v = _inproj(x, mod, norm_g, w_in_bf, new_k, new_v, l)
        o_att = _attention(p, ck_all, cv_all, bias_all, l)
        if l % 2 == 0:
            x1, h2 = _mixer_out(p, o_att, x, mod, norm_g, g_v3, w_s, bs_b, wc_t, w_out_bf,
                                None, l, None)
            x = _ffn(h2, w_ffn_gu, w_ffn_d, dense_te + l // 2, dense_nv, dense_eff, NT_DENSE,
                     (x1, mod, norm_g, l), "ffn_dense")
        else:
            x1, h2, route = _mixer_out(p, o_att, x, mod, norm_g, g_v3, w_s, bs_b, wc_t, w_out_bf,
                                       w_router_t, l, l // 2)
            src, pos1, pos2, te, nv, eff = _route_plan(route)
            xs = _gather_rows(h2, src, "moe_dispatch")
            ys = _ffn(xs, w_moe_gu_all, w_moe_d_all, te + (l // 2) * N_EXPERTS, nv, eff, NT_MOE,
                      None, "ffn_moe")
            y12 = _gather_rows(ys, jnp.concatenate([pos1, pos2]), "moe_combine")
            x = _ffn_out_moe(x1, y12, route, mod, norm_g, l)

    y_prompt = x[:T_P].reshape(BATCH, SEQ, D_MODEL)
    y_sample = x[T_P:].reshape(DEC_BATCH, DEC_SEQ, D_MODEL)
    cache_shape = (BATCH, DEPTH, SEQ, H_ATT, HEAD_DIM)
    return y_prompt, y_sample, new_k.reshape(cache_shape), new_v.reshape(cache_shape)
```

```python
import functools

import numpy as np
import jax
import jax.numpy as jnp
from jax import lax
from jax.experimental import pallas as pl
from jax.experimental.pallas import tpu as pltpu

F32 = jnp.float32
BF16 = jnp.bfloat16

D_MODEL = 1024
BATCH = 16
SEQ = 256
DEPTH = 4
DEC_BATCH = 4
DEC_SEQ = 2048
PAST_LEN = 256
GRID_W = 64
HEAD_DIM = 64
D_ATT = 512
D_CMLP = 256
D_CONV = 256
H_ATT = 8
H_CMLP = 4
CHUNK = 128
NA_KH = 8
NA_KW = 16
D_IN = 2816
D_FF = 2816
N_EXPERTS = 8
N_MOD = 6
EPS = 1e-6
NEG_INF = -1e30

T_P = BATCH * SEQ
T_S = DEC_BATCH * DEC_SEQ
T = T_P + T_S
TM = 256
NT = T // TM
NT_P = T_P // TM
TM_IN = 512
TILES_PER_DEC = DEC_SEQ // TM
GROUPS = 8

ROWS = DEC_SEQ // GRID_W
QROWS = TM // GRID_W
BAND_TILES = 3
BAND = BAND_TILES * TM

FF_TILE = 1024
FF_SUB = 256
FF_CH = 256
N_FF_CH = D_FF // FF_CH
R_MOE = 2 * T + N_EXPERTS * FF_TILE
NT_MOE = R_MOE // FF_TILE
NT_DENSE = T // FF_TILE
GATHER_CH = 512
GATHER_UNROLL = 16

VMEM_LIMIT = 56 * 1024 * 1024


def _group_of_tile(i):
    return jnp.where(i < NT_P, 0, 1 + (i - NT_P) // TILES_PER_DEC)


def _rms(x, g):
    return x * lax.rsqrt(jnp.mean(x * x, axis=-1, keepdims=True) + EPS) * g


def _silu(x):
    return x / (1.0 + jnp.exp(-x))


def _gelu_tanh(x):
    c = np.float32(np.sqrt(2.0 / np.pi))
    return 0.5 * x * (1.0 + jnp.tanh(c * (x + np.float32(0.044715) * (x * x * x))))


def _mod_kernel(cv_ref, w_ref, b_ref, o_ref):
    a = _silu(cv_ref[...])
    o_ref[...] = jnp.dot(a, w_ref[...], preferred_element_type=F32,
                         precision=lax.Precision.HIGHEST) + b_ref[...]


def _modulation(cvec, w_ada, b_ada):
    tn = 1536
    nn = (N_MOD * D_MODEL) // tn
    out = pl.pallas_call(
        _mod_kernel,
        grid=(DEPTH, nn),
        in_specs=[
            pl.BlockSpec((GROUPS, D_MODEL), lambda l, n: (0, 0)),
            pl.BlockSpec((None, D_MODEL, tn), lambda l, n: (l, 0, n)),
            pl.BlockSpec((None, 1, tn), lambda l, n: (l, 0, n)),
        ],
        out_specs=pl.BlockSpec((None, GROUPS, tn), lambda l, n: (l, 0, n)),
        out_shape=jax.ShapeDtypeStruct((DEPTH, GROUPS, N_MOD * D_MODEL), F32),
        compiler_params=pltpu.CompilerParams(
            dimension_semantics=("parallel", "parallel"), vmem_limit_bytes=VMEM_LIMIT),
        name="adaln_mod",
    )(cvec, w_ada, b_ada.reshape(DEPTH, 1, N_MOD * D_MODEL))
    return out.reshape(DEPTH, GROUPS, N_MOD, D_MODEL)


def _inproj_kernel(x_ref, mod_ref, g_ref, w_ref, kc_in_ref, vc_in_ref, p_ref, kc_ref, vc_ref):
    del kc_in_ref, vc_in_ref
    h = _rms(x_ref[...], g_ref[0:1, :]) * (1.0 + mod_ref[1:2, :]) + mod_ref[0:1, :]
    p_ref[...] = jnp.dot(h.astype(BF16), w_ref[...], preferred_element_type=F32)

    @pl.when(pl.program_id(0) < T_P // TM_IN)
    def _():
        for r in range(TM_IN // SEQ):
            rows = slice(r * SEQ, (r + 1) * SEQ)
            kc_ref[r] = p_ref[rows, D_ATT:2 * D_ATT]
            vc_ref[r] = p_ref[rows, 2 * D_ATT:3 * D_ATT]


def _inproj(x, mod, norm_g, w_in_bf, new_k, new_v, l):
    cache_spec = pl.BlockSpec((TM_IN // SEQ, None, SEQ, D_ATT),
                              lambda i: (jnp.minimum(i, T_P // TM_IN - 1), l, 0, 0))
    return pl.pallas_call(
        _inproj_kernel,
        grid=(T // TM_IN,),
        in_specs=[
            pl.BlockSpec((TM_IN, D_MODEL), lambda i: (i, 0)),
            pl.BlockSpec((None, None, N_MOD, D_MODEL),
                         lambda i: (l, _group_of_tile(i * (TM_IN // TM)), 0, 0)),
            pl.BlockSpec((None, 4, D_MODEL), lambda i: (l, 0, 0)),
            pl.BlockSpec((None, D_MODEL, D_IN), lambda i: (l, 0, 0)),
            pl.BlockSpec(memory_space=pl.ANY),
            pl.BlockSpec(memory_space=pl.ANY),
        ],
        out_specs=[pl.BlockSpec((TM_IN, D_IN), lambda i: (i, 0)), cache_spec, cache_spec],
        out_shape=[jax.ShapeDtypeStruct((T, D_IN), F32),
                   jax.ShapeDtypeStruct(new_k.shape, F32),
                   jax.ShapeDtypeStruct(new_v.shape, F32)],
        input_output_aliases={4: 1, 5: 2},
        compiler_params=pltpu.CompilerParams(
            dimension_semantics=("arbitrary",), vmem_limit_bytes=VMEM_LIMIT),
        name="inproj",
    )(x, mod, norm_g, w_in_bf, new_k, new_v)


def _dot_nt(a, b):
    return lax.dot_general(a, b, (((1,), (1,)), ((), ())), preferred_element_type=F32)


def _attn_kernel(q_ref, k0_ref, k1_ref, k2_ref, v0_ref, v1_ref, v2_ref,
                 ck_ref, cv_ref, bias_ref, o_ref):
    is_prompt = pl.program_id(0) < NT_P

    @pl.when(is_prompt)
    def _():
        _ctx_attn_body(q_ref, k0_ref, v0_ref, o_ref)

    @pl.when(jnp.logical_not(is_prompt))
    def _():
        _na_attn_body(q_ref, k0_ref, k1_ref, k2_ref, v0_ref, v1_ref, v2_ref,
                      ck_ref, cv_ref, bias_ref, o_ref)


def _ctx_attn_body(q_ref, k_ref, v_ref, o_ref):
    scale = np.float32(HEAD_DIM ** -0.5)
    for h in range(H_ATT):
        sl = slice(h * HEAD_DIM, (h + 1) * HEAD_DIM)
        q = (q_ref[:, sl] * scale).astype(BF16)
        k = k_ref[:, sl].astype(BF16)
        v = v_ref[:, sl].astype(BF16)
        s = _dot_nt(q, k)
        m = jnp.max(s, axis=-1, keepdims=True)
        e = jnp.exp(s - m)
        den = jnp.sum(e, axis=-1, keepdims=True)
        o = jnp.dot(e.astype(BF16), v, preferred_element_type=F32) / den
        o_ref[:, sl] = o.astype(BF16)


def _na_attn_body(q_ref, k0_ref, k1_ref, k2_ref, v0_ref, v1_ref, v2_ref,
                  ck_ref, cv_ref, bias_ref, o_ref):
    scale = np.float32(HEAD_DIM ** -0.5)
    k_refs = (k0_ref, k1_ref, k2_ref)
    v_refs = (v0_ref, v1_ref, v2_ref)
    for h in range(H_ATT):
        sl = slice(h * HEAD_DIM, (h + 1) * HEAD_DIM)
        q = (q_ref[:, sl] * scale).astype(BF16)
        s_loc = [_dot_nt(q, k_refs[j][:, sl].astype(BF16)) + bias_ref[h, :, j * TM:(j + 1) * TM]
                 for j in range(BAND_TILES)]
        s_ctx = _dot_nt(q, ck_ref[:, sl].astype(BF16))
        m = jnp.max(s_ctx, axis=-1, keepdims=True)
        for s in s_loc:
            m = jnp.maximum(m, jnp.max(s, axis=-1, keepdims=True))
        e_ctx = jnp.exp(s_ctx - m)
        den = jnp.sum(e_ctx, axis=-1, keepdims=True)
        acc = jnp.dot(e_ctx.astype(BF16), cv_ref[:, sl].astype(BF16), preferred_element_type=F32)
        for j in range(BAND_TILES):
            e = jnp.exp(s_loc[j] - m)
            den = den + jnp.sum(e, axis=-1, keepdims=True)
            acc = acc + jnp.dot(e.astype(BF16), v_refs[j][:, sl].astype(BF16),
                                preferred_element_type=F32)
        o_ref[:, sl] = (acc / den).astype(BF16)


def _na_variant_tables():
    kh = min(NA_KH, ROWS)
    per_tile = []
    for rb in range(ROWS // QROWS):
        r0 = rb * QROWS
        bs = int(np.clip(rb - 1, 0, ROWS // QROWS - BAND_TILES)) * QROWS
        tab = -np.ones((QROWS, BAND_TILES * QROWS), np.int32)
        for qr in range(QROWS):
            r = r0 + qr
            rs = int(np.clip(r - kh // 2, 0, ROWS - kh))
            for kr in range(BAND_TILES * QROWS):
                ka = bs + kr
                if rs <= ka < rs + kh:
                    tab[qr, kr] = ka - r + (NA_KH - 1)
            assert (tab[qr] >= 0).sum() == kh
        per_tile.append(tab)
    variants, variant_of_tile = [], []
    for tab in per_tile:
        for vi, v in enumerate(variants):
            if np.array_equal(v, tab):
                variant_of_tile.append(vi)
                break
        else:
            variants.append(tab)
            variant_of_tile.append(len(variants) - 1)
    return np.asarray(variant_of_tile, np.int32), np.stack(variants)


_NA_VARIANT_OF_TILE, _NA_DR_IDX = _na_variant_tables()
_NA_NVAR = _NA_DR_IDX.shape[0]
_N_DR = 2 * NA_KH - 1
_N_DC = 2 * NA_KW - 1


def _bias_kernel(w_ref, o_ref):
    qc = lax.broadcasted_iota(jnp.int32, (GRID_W, LANES), 0)
    lane = lax.broadcasted_iota(jnp.int32, (GRID_W, LANES), 1)
    kc = lane % GRID_W
    cs = jnp.clip(qc - NA_KW // 2, 0, GRID_W - NA_KW)
    col_ok = jnp.logical_and(kc >= cs, kc < cs + NA_KW)
    left = lane < GRID_W
    neg = jnp.full((GRID_W, LANES), NEG_INF, F32)
    cache = {}
    for vi in range(_NA_NVAR):
        for qr in range(QROWS):
            for pp in range(BAND_TILES * QROWS // 2):
                d0 = int(_NA_DR_IDX[vi, qr, 2 * pp])
                d1 = int(_NA_DR_IDX[vi, qr, 2 * pp + 1])
                if (d0, d1) not in cache:
                    if d0 < 0 and d1 < 0:
                        tile = neg
                    else:
                        u = (w_ref[pl.ds(d0 if d0 >= 0 else _N_DR, 1), :]
                             + w_ref[pl.ds(_N_DR + 1 + (d1 if d1 >= 0 else _N_DR), 1), :])
                        t = pltpu.roll(jnp.broadcast_to(u, (GRID_W, LANES)), 0, 1,
                                       stride=1, stride_axis=0)
                        ok = col_ok
                        if d0 < 0:
                            ok = jnp.logical_and(ok, jnp.logical_not(left))
                        if d1 < 0:
                            ok = jnp.logical_and(ok, left)
                        tile = jnp.where(ok, t, neg)
                    cache[(d0, d1)] = tile
                o_ref[vi, qr * GRID_W:(qr + 1) * GRID_W, pp * LANES:(pp + 1) * LANES] = cache[(d0, d1)]


def _na_bias_tables(rpb):
    half = NA_KW - 1
    zeros = jnp.zeros((DEPTH, H_ATT, _N_DR, LANES - _N_DC), F32)
    w_lo = jnp.concatenate([rpb[..., half:], zeros, rpb[..., :half]], axis=-1)
    w_lo = jnp.pad(w_lo, ((0, 0), (0, 0), (0, 1), (0, 0)))
    w_hi = jnp.roll(w_lo, GRID_W, axis=-1)
    w = jnp.concatenate([w_lo, w_hi], axis=2)
    return pl.pallas_call(
        _bias_kernel,
        grid=(DEPTH, H_ATT),
        in_specs=[pl.BlockSpec((None, None, 2 * (_N_DR + 1), LANES), lambda l, h: (l, h, 0, 0))],
        out_specs=pl.BlockSpec((None, _NA_NVAR, None, TM, BAND), lambda l, h: (l, 0, h, 0, 0)),
        out_shape=jax.ShapeDtypeStruct((DEPTH, _NA_NVAR, H_ATT, TM, BAND), F32),
        compiler_params=pltpu.CompilerParams(
            dimension_semantics=("parallel", "parallel"), vmem_limit_bytes=VMEM_LIMIT),
        name="na_bias",
    )(w)


def _attention(p, ck_all, cv_all, bias_all, l):
    nrb = TILES_PER_DEC
    var_of_tile = [int(v) for v in _NA_VARIANT_OF_TILE]

    def dec_batch(i):
        return jnp.maximum(i - NT_P, 0) // nrb

    def band_tile(i, j):
        rb = (i - NT_P) % nrb
        first = NT_P + dec_batch(i) * nrb + jnp.clip(rb - 1, 0, nrb - BAND_TILES)
        return jnp.where(i < NT_P, i, first + j)

    def variant(i):
        rb = jnp.maximum(i - NT_P, 0) % nrb
        v = jnp.int32(var_of_tile[0])
        for t in range(1, nrb):
            v = jnp.where(rb >= t, jnp.int32(var_of_tile[t]), v)
        return v

    kv_specs = [pl.BlockSpec((TM, D_ATT), functools.partial(
        lambda i, j, col: (band_tile(i, j), col), j=j, col=col))
        for col in (1, 2) for j in range(BAND_TILES)]
    return pl.pallas_call(
        _attn_kernel,
        grid=(NT,),
        in_specs=[pl.BlockSpec((TM, D_ATT), lambda i: (i, 0))] + kv_specs + [
            pl.BlockSpec((None, None, PAST_LEN, D_ATT), lambda i: (dec_batch(i), l, 0, 0)),
            pl.BlockSpec((None, None, PAST_LEN, D_ATT), lambda i: (dec_batch(i), l, 0, 0)),
            pl.BlockSpec((None, None, H_ATT, TM, BAND), lambda i: (l, variant(i), 0, 0, 0)),
        ],
        out_specs=pl.BlockSpec((TM, D_ATT), lambda i: (i, 0)),
        out_shape=jax.ShapeDtypeStruct((T, D_ATT), BF16),
        compiler_params=pltpu.CompilerParams(
            dimension_semantics=("arbitrary",), vmem_limit_bytes=VMEM_LIMIT),
        name="attn",
    )(p, p, p, p, p, p, p, ck_all, cv_all, bias_all)


def _mixer_out_kernel(with_router, *refs):
    if with_router:
        (u_ref, vm_ref, bg_ref, cg_ref, hx_ref, cgp_ref, hxp_ref, cgn_ref, hxn_ref,
         oatt_ref, x_ref, mod_ref, g_ref, gv_ref, ws_ref, bs_ref, wc_ref, wo_ref, wr_ref,
         x1_ref, h2_ref, route_ref) = refs
    else:
        (u_ref, vm_ref, bg_ref, cg_ref, hx_ref, cgp_ref, hxp_ref, cgn_ref, hxn_ref,
         oatt_ref, x_ref, mod_ref, g_ref, gv_ref, ws_ref, bs_ref, wc_ref, wo_ref,
         x1_ref, h2_ref) = refs
    i = pl.program_id(0)

    u = _gelu_tanh(u_ref[...])
    vm = _rms(_gelu_tanh(vm_ref[...]), gv_ref[...]).astype(BF16)
    chunks = []
    for c in range(TM // CHUNK):
        rows = slice(c * CHUNK, (c + 1) * CHUNK)
        heads = [jnp.dot(ws_ref[h].astype(BF16), vm[rows, h * HEAD_DIM:(h + 1) * HEAD_DIM],
                         preferred_element_type=F32) for h in range(H_CMLP)]
        chunks.append(jnp.concatenate(heads, axis=1) + bs_ref[...])
    o_mlp = u * jnp.concatenate(chunks, axis=0)

    j = (i - NT_P) % TILES_PER_DEC
    has_prev = jnp.logical_and(i >= NT_P, j > 0)
    has_next = jnp.logical_and(i >= NT_P, j < TILES_PER_DEC - 1)
    z = cg_ref[...] * hx_ref[...]
    z_halo_prev = jnp.where(has_prev, cgp_ref[7:8, :] * hxp_ref[7:8, :], 0.0)
    z_halo_next = jnp.where(has_next, cgn_ref[0:1, :] * hxn_ref[0:1, :], 0.0)
    row = lax.broadcasted_iota(jnp.int32, (TM, D_CONV), 0)
    z_prev = jnp.where(row == 0, z_halo_prev, pltpu.roll(z, 1, 0))
    z_next = jnp.where(row == TM - 1, z_halo_next, pltpu.roll(z, TM - 1, 0))
    y_conv = z_prev * wc_ref[0:1, :] + z * wc_ref[1:2, :] + z_next * wc_ref[2:3, :]
    o_conv = bg_ref[...] * y_conv

    y = jnp.dot(oatt_ref[...], wo_ref[0:D_ATT, :], preferred_element_type=F32)
    y = y + jnp.dot(o_mlp.astype(BF16), wo_ref[D_ATT:D_ATT + D_CMLP, :],
                    preferred_element_type=F32)
    y = y + jnp.dot(o_conv.astype(BF16), wo_ref[D_ATT + D_CMLP:, :],
                    preferred_element_type=F32)

    x1 = x_ref[...] + mod_ref[2:3, :] * _rms(y, g_ref[1:2, :])
    h2 = _rms(x1, g_ref[2:3, :]) * (1.0 + mod_ref[4:5, :]) + mod_ref[3:4, :]
    x1_ref[...] = x1
    if with_router:
        _store_row_tiled(h2_ref, h2, TM)
    else:
        h2_ref[...] = h2

    if with_router:
        lg = [jnp.sum(h2 * wr_ref[e:e + 1, :], axis=-1, keepdims=True) for e in range(N_EXPERTS)]

        def top1(cols):
            m = cols[0]
            for col in cols[1:]:
                m = jnp.maximum(m, col)
            idx = jnp.full_like(m, N_EXPERTS - 1)
            for e in range(N_EXPERTS - 2, -1, -1):
                idx = jnp.where(cols[e] == m, np.float32(e), idx)
            return m, idx

        m1, i1 = top1(lg)
        m2, i2 = top1([jnp.where(i1 == np.float32(e), -jnp.inf, lg[e]) for e in range(N_EXPERTS)])
        lane = lax.broadcasted_iota(jnp.int32, (TM, 128), 1)
        e2 = jnp.exp(m2 - m1)
        den = 1.0 + e2
        gate1 = 1.0 / den
        gate2 = e2 / den
        route = jnp.where(lane == 0, i1,
                          jnp.where(lane == 1, i2,
                                    jnp.where(lane == 2, gate1,
                                              jnp.where(lane == 3, gate2, 0.0))))
        route_ref[...] = route


def _mixer_out(p, o_att, x, mod, norm_g, g_v, w_s, bs_b, wc_t, w_out_bf, w_router_t, l, moe_idx):
    with_router = moe_idx is not None
    hb = TM // 8

    def col(cb):
        return pl.BlockSpec((TM, D_CMLP), lambda i: (i, cb))

    def halo_prev(cb):
        return pl.BlockSpec((8, D_CONV), lambda i: (jnp.maximum(i * hb - 1, 0), cb))

    def halo_next(cb):
        return pl.BlockSpec((8, D_CONV), lambda i: (jnp.minimum((i + 1) * hb, T // 8 - 1), cb))

    in_specs = [col(6), col(7), col(8), col(9), col(10),
                halo_prev(9), halo_prev(10), halo_next(9), halo_next(10),
                pl.BlockSpec((TM, D_ATT), lambda i: (i, 0)),
                pl.BlockSpec((TM, D_MODEL), lambda i: (i, 0)),
                pl.BlockSpec((None, None, N_MOD, D_MODEL), lambda i: (l, _group_of_tile(i), 0, 0)),
                pl.BlockSpec((None, 4, D_MODEL), lambda i: (l, 0, 0)),
                pl.BlockSpec((None, 1, D_CMLP), lambda i: (l, 0, 0)),
                pl.BlockSpec((None, H_CMLP, CHUNK, CHUNK), lambda i: (l, 0, 0, 0)),
                pl.BlockSpec((None, CHUNK, D_CMLP), lambda i: (l, 0, 0)),
                pl.BlockSpec((None, 3, D_CONV), lambda i: (l, 0, 0)),
                pl.BlockSpec((None, D_MODEL, D_MODEL), lambda i: (l, 0, 0))]
    args = [p, p, p, p, p, p, p, p, p, o_att, x, mod, norm_g, g_v, w_s, bs_b, wc_t, w_out_bf]
    out_specs = [pl.BlockSpec((TM, D_MODEL), lambda i: (i, 0)),
                 pl.BlockSpec((TM, D_MODEL), lambda i: (i, 0))]
    out_shape = [jax.ShapeDtypeStruct((T, D_MODEL), F32), jax.ShapeDtypeStruct((T, D_MODEL), F32)]
    if with_router:
        in_specs.append(pl.BlockSpec((None, N_EXPERTS, D_MODEL), lambda i: (moe_idx, 0, 0)))
        args.append(w_router_t)
        out_specs[1] = pl.BlockSpec((TM * ROW_SUB, LANES), lambda i: (i, 0))
        out_shape[1] = jax.ShapeDtypeStruct((T * ROW_SUB, LANES), F32)
        out_specs.append(pl.BlockSpec((TM, 128), lambda i: (i, 0)))
        out_shape.append(jax.ShapeDtypeStruct((T, 128), F32))
    return pl.pallas_call(
        functools.partial(_mixer_out_kernel, with_router),
        grid=(NT,),
        in_specs=in_specs,
        out_specs=out_specs,
        out_shape=out_shape,
        compiler_params=pltpu.CompilerParams(
            dimension_semantics=("parallel",), vmem_limit_bytes=VMEM_LIMIT),
        name="mixer_out_router" if with_router else "mixer_out",
    )(*args)


LANES = 128
ROW_SUB = D_MODEL // LANES


def _load_row_tiled(ref, n):
    return jnp.concatenate([ref[pl.ds(c, n, stride=ROW_SUB), :] for c in range(ROW_SUB)], axis=1)


def _store_row_tiled(ref, val, n):
    for c in range(ROW_SUB):
        ref[pl.ds(c, n, stride=ROW_SUB), :] = val[:, c * LANES:(c + 1) * LANES]


def _ffn_kernel(row_tiled, te_ref, nv_ref, et_ref, x_ref, wg_ref, wu_ref, wd_ref, *rest):
    if row_tiled:
        o_ref, xb_ref, acc_ref = rest
    else:
        x1_ref, mod_ref, g_ref, o_ref, xb_ref, acc_ref = rest
    del te_ref, et_ref
    t = pl.program_id(0)
    j = pl.program_id(1)
    nvalid = nv_ref[t]
    nsub = (nvalid + (FF_SUB - 1)) // FF_SUB

    @pl.when(j == 0)
    def _():
        acc_ref[...] = jnp.zeros((FF_TILE, D_MODEL), F32)

    @pl.when(jnp.logical_and(j == 0, nvalid >= 0))
    def _():
        if row_tiled:
            xb_ref[...] = _load_row_tiled(x_ref, FF_TILE).astype(BF16)
        else:
            xb_ref[...] = x_ref[...].astype(BF16)

    def swiglu_rows(n):
        wg = wg_ref[...].astype(BF16)
        wu = wu_ref[...].astype(BF16)
        wd = wd_ref[...].astype(BF16)
        xs = xb_ref[0:n, :]
        g = jnp.dot(xs, wg, preferred_element_type=F32)
        u = jnp.dot(xs, wu, preferred_element_type=F32)
        a = (_silu(g) * u).astype(BF16)
        acc_ref[0:n, :] += jnp.dot(a, wd, preferred_element_type=F32)

    if row_tiled:
        for k in range(1, FF_TILE // FF_SUB + 1):
            pl.when(nsub == k)(functools.partial(swiglu_rows, k * FF_SUB))
    else:
        swiglu_rows(FF_TILE)

    @pl.when(j == N_FF_CH - 1)
    def _():
        if row_tiled:
            _store_row_tiled(o_ref, acc_ref[...], FF_TILE)
        else:
            o_ref[...] = x1_ref[...] + mod_ref[5:6, :] * _rms(acc_ref[...], g_ref[3:4, :])


def _ffn(x_rows, w_gu, w_d, tile_expert, tile_nvalid, tile_eff, n_tiles, residual, name):
    def chunk(t, j, nv):
        return jnp.where(nv[t] >= 0, j, N_FF_CH - 1)

    row_tiled = residual is None
    if row_tiled:
        x_block = (FF_TILE * ROW_SUB, LANES)
        out_shape = jax.ShapeDtypeStruct((n_tiles * FF_TILE * ROW_SUB, LANES), F32)
        extra_specs, extra_args = [], ()
    else:
        x1, mod, norm_g, l = residual
        x_block = (FF_TILE, D_MODEL)
        out_shape = jax.ShapeDtypeStruct((n_tiles * FF_TILE, D_MODEL), F32)
        tiles_per_tm = FF_TILE // TM
        extra_specs = [
            pl.BlockSpec(x_block, lambda t, j, te, nv, et: (t, 0)),
            pl.BlockSpec((None, None, N_MOD, D_MODEL),
                         lambda t, j, te, nv, et: (l, _group_of_tile(t * tiles_per_tm), 0, 0)),
            pl.BlockSpec((None, 4, D_MODEL), lambda t, j, te, nv, et: (l, 0, 0)),
        ]
        extra_args = (x1, mod, norm_g)
    grid_spec = pltpu.PrefetchScalarGridSpec(
        num_scalar_prefetch=3,
        grid=(n_tiles, N_FF_CH),
        in_specs=[
            pl.BlockSpec(x_block, lambda t, j, te, nv, et: (et[t], 0)),
            pl.BlockSpec((None, D_MODEL, FF_CH),
                         lambda t, j, te, nv, et: (te[t], 0, chunk(t, j, nv))),
            pl.BlockSpec((None, D_MODEL, FF_CH),
                         lambda t, j, te, nv, et: (te[t], 0, N_FF_CH + chunk(t, j, nv))),
            pl.BlockSpec((None, FF_CH, D_MODEL),
                         lambda t, j, te, nv, et: (te[t], chunk(t, j, nv), 0)),
        ] + extra_specs,
        out_specs=pl.BlockSpec(x_block, lambda t, j, te, nv, et: (t, 0)),
        scratch_shapes=[pltpu.VMEM((FF_TILE, D_MODEL), BF16),
                        pltpu.VMEM((FF_TILE, D_MODEL), F32)],
    )
    return pl.pallas_call(
        functools.partial(_ffn_kernel, row_tiled),
        grid_spec=grid_spec,
        out_shape=out_shape,
        compiler_params=pltpu.CompilerParams(
            dimension_semantics=("arbitrary", "arbitrary"), vmem_limit_bytes=VMEM_LIMIT),
        name=name,
    )(tile_expert, tile_nvalid, tile_eff, x_rows, w_gu, w_gu, w_d, *extra_args)


def _gather_kernel(idx_ref, src_ref, out_ref, sem):
    def row_copy(r):
        src_row = pl.multiple_of(idx_ref[0, r] * ROW_SUB, ROW_SUB)
        dst_row = pl.multiple_of(r * ROW_SUB, ROW_SUB)
        return pltpu.make_async_copy(src_ref.at[pl.ds(src_row, ROW_SUB)],
                                     out_ref.at[pl.ds(dst_row, ROW_SUB)], sem)

    def start_group(g, carry):
        for u in range(GATHER_UNROLL):
            row_copy(g * GATHER_UNROLL + u).start(priority=u % 2)
        return carry

    lax.fori_loop(0, GATHER_CH // GATHER_UNROLL, start_group, 0)
    pltpu.make_async_copy(src_ref.at[pl.ds(0, GATHER_CH * ROW_SUB)], out_ref, sem).wait()


def _gather_rows(src, idx, name):
    n = idx.shape[0]
    steps = n // GATHER_CH
    return pl.pallas_call(
        _gather_kernel,
        grid=(steps,),
        in_specs=[pl.BlockSpec((None, 1, GATHER_CH), lambda i: (i, 0, 0),
                               memory_space=pltpu.SMEM),
                  pl.BlockSpec(memory_space=pl.ANY)],
        out_specs=pl.BlockSpec((GATHER_CH * ROW_SUB, LANES), lambda i: (i, 0)),
        out_shape=jax.ShapeDtypeStruct((n * ROW_SUB, LANES), src.dtype),
        scratch_shapes=[pltpu.SemaphoreType.DMA(())],
        compiler_params=pltpu.CompilerParams(
            dimension_semantics=("arbitrary",), vmem_limit_bytes=VMEM_LIMIT),
        name=name,
    )(idx.reshape(steps, 1, GATHER_CH), src)


def _ffn_out_moe_kernel(x1_ref, y1_ref, y2_ref, route_ref, mod_ref, g_ref, o_ref):
    f = (route_ref[:, 2:3] * _load_row_tiled(y1_ref, TM)
         + route_ref[:, 3:4] * _load_row_tiled(y2_ref, TM))
    o_ref[...] = x1_ref[...] + mod_ref[5:6, :] * _rms(f, g_ref[3:4, :])


def _ffn_out_moe(x1, ys, route, mod, norm_g, l):
    row = pl.BlockSpec((TM, D_MODEL), lambda i: (i, 0))
    return pl.pallas_call(
        _ffn_out_moe_kernel,
        grid=(NT,),
        in_specs=[row, pl.BlockSpec((TM * ROW_SUB, LANES), lambda i: (i, 0)),
                  pl.BlockSpec((TM * ROW_SUB, LANES), lambda i: (i + NT, 0)),
                  pl.BlockSpec((TM, 128), lambda i: (i, 0)),
                  pl.BlockSpec((None, None, N_MOD, D_MODEL),
                               lambda i: (l, _group_of_tile(i), 0, 0)),
                  pl.BlockSpec((None, 4, D_MODEL), lambda i: (l, 0, 0))],
        out_specs=row,
        out_shape=jax.ShapeDtypeStruct((T, D_MODEL), F32),
        compiler_params=pltpu.CompilerParams(
            dimension_semantics=("parallel",), vmem_limit_bytes=VMEM_LIMIT),
        name="ffn_out_moe",
    )(x1, ys, ys, route, mod, norm_g)


def _route_plan(route):
    e1 = route[:, 0].astype(jnp.int32)
    e2 = route[:, 1].astype(jnp.int32)
    ar = jnp.arange(N_EXPERTS, dtype=jnp.int32)
    oh1 = (e1[:, None] == ar[None, :]).astype(jnp.int32)
    oh2 = (e2[:, None] == ar[None, :]).astype(jnp.int32)
    oh = oh1 + oh2
    csum_incl = jnp.cumsum(oh, axis=0)
    csum = csum_incl - oh
    counts = csum_incl[-1]
    ntile_e = (counts + FF_TILE - 1) // FF_TILE
    tile_end_e = jnp.cumsum(ntile_e)
    tile_start_e = tile_end_e - ntile_e
    gstart = tile_start_e * FF_TILE
    pos1 = jnp.sum((gstart[None, :] + csum) * oh1, axis=1)
    pos2 = jnp.sum((gstart[None, :] + csum) * oh2, axis=1)
    tok = jnp.arange(T, dtype=jnp.int32)
    src = (jnp.arange(R_MOE, dtype=jnp.int32) % T).at[pos1].set(tok).at[pos2].set(tok)
    n_used = tile_end_e[-1]
    tiles = jnp.arange(NT_MOE, dtype=jnp.int32)
    eff = jnp.minimum(tiles, n_used - 1)
    te = jnp.sum((eff[:, None] >= tile_end_e[None, :]).astype(jnp.int32), axis=1)
    te = jnp.minimum(te, N_EXPERTS - 1)
    nvalid = jnp.clip(counts[te] - (eff - tile_start_e[te]) * FF_TILE, 0, FF_TILE)
    nvalid = jnp.where(tiles < n_used, nvalid, -1)
    return src, pos1, pos2, te.astype(jnp.int32), nvalid.astype(jnp.int32), eff.astype(jnp.int32)


def kernel(x_prompt, x_sample, cache_k, cache_v, c, c_ctx, w_ada, b_ada, norm_g, w_in, w_out,
           rpb, g_v, w_s, b_s, w_conv, w_ffn_gu, w_ffn_d, w_router, w_moe_gu, w_moe_d):
    x = jnp.concatenate([x_prompt.reshape(T_P, D_MODEL), x_sample.reshape(T_S, D_MODEL)], axis=0)
    cvec = jnp.concatenate([c_ctx[None], c, jnp.zeros((GROUPS - 1 - DEC_BATCH, D_MODEL), F32)],
                           axis=0)
    mod = _modulation(cvec, w_ada, b_ada)

    w_in_bf = w_in.astype(BF16)
    w_out_bf = w_out.astype(BF16)
    w_ffn_gu_bf = w_ffn_gu.astype(BF16)
    w_ffn_d_bf = w_ffn_d.astype(BF16)
    ck_all = cache_k.reshape(DEC_BATCH, DEPTH, PAST_LEN, D_ATT)
    cv_all = cache_v.reshape(DEC_BATCH, DEPTH, PAST_LEN, D_ATT)
    bias_all = _na_bias_tables(rpb)
    g_v3 = g_v.reshape(DEPTH, 1, D_CMLP)
    bs_b = jnp.repeat(jnp.swapaxes(b_s, 1, 2), HEAD_DIM, axis=2)
    wc_t = jnp.swapaxes(w_conv, 1, 2)
    w_router_t = jnp.swapaxes(w_router, 1, 2)
    w_moe_gu_all = w_moe_gu.reshape(-1, D_MODEL, 2 * D_FF)
    w_moe_d_all = w_moe_d.reshape(-1, D_FF, D_MODEL)

    dense_te = jnp.zeros((NT_DENSE,), jnp.int32)
    dense_nv = jnp.full((NT_DENSE,), FF_TILE, jnp.int32)
    dense_eff = jnp.arange(NT_DENSE, dtype=jnp.int32)

    new_k = jnp.zeros((BATCH, DEPTH, SEQ, D_ATT), F32)
    new_v = jnp.zeros((BATCH, DEPTH, SEQ, D_ATT), F32)
    for l in range(DEPTH):
        p, new_k, new_v = _inproj(x, mod, norm_g, w_in_bf, new_k, new_v, l)
        o_att = _attention(p, ck_all, cv_all, bias_all, l)
        if l % 2 == 0:
            x1, h2 = _mixer_out(p, o_att, x, mod, norm_g, g_v3, w_s, bs_b, wc_t, w_out_bf,
                                None, l, None)
            x = _ffn(h2, w_ffn_gu_bf, w_ffn_d_bf, dense_te + l // 2, dense_nv, dense_eff, NT_DENSE,
                     (x1, mod, norm_g, l), "ffn_dense")
        else:
            x1, h2, route = _mixer_out(p, o_att, x, mod, norm_g, g_v3, w_s, bs_b, wc_t, w_out_bf,
                                       w_router_t, l, l // 2)
            src, pos1, pos2, te, nv, eff = _route_plan(route)
            xs = _gather_rows(h2, src, "moe_dispatch")
            ys = _ffn(xs, w_moe_gu_all, w_moe_d_all, te + (l // 2) * N_EXPERTS, nv, eff, NT_MOE,
                      None, "ffn_moe")
            y12 = _gather_rows(ys, jnp.concatenate([pos1, pos2]), "moe_combine")
            x = _ffn_out_moe(x1, y12, route, mod, norm_g, l)

    y_prompt = x[:T_P].reshape(BATCH, SEQ, D_MODEL)
    y_sample = x[T_P:].reshape(DEC_BATCH, DEC_SEQ, D_MODEL)
    cache_shape = (BATCH, DEPTH, SEQ, H_ATT, HEAD_DIM)
    return y_prompt, y_sample, new_k.reshape(cache_shape), new_v.reshape(cache_shape)
```

```python
import functools

import numpy as np
import jax
import jax.numpy as jnp
from jax import lax
from jax.experimental import pallas as pl
from jax.experimental.pallas import tpu as pltpu

F32 = jnp.float32
BF16 = jnp.bfloat16

D_MODEL = 1024
BATCH = 16
SEQ = 256
DEPTH = 4
DEC_BATCH = 4
DEC_SEQ = 2048
PAST_LEN = 256
GRID_W = 64
HEAD_DIM = 64
D_ATT = 512
D_CMLP = 256
D_CONV = 256
H_ATT = 8
H_CMLP = 4
CHUNK = 128
NA_KH = 8
NA_KW = 16
D_IN = 2816
D_FF = 2816
N_EXPERTS = 8
TOP_K = 2
N_MOD = 6
EPS = 1e-6
NEG_INF = -1e30

T_P = BATCH * SEQ
T_S = DEC_BATCH * DEC_SEQ
T = T_P + T_S
TM = 256
NT = T // TM
NT_P = T_P // TM
TM_IN = 512
TILES_PER_DEC = DEC_SEQ // TM
GROUPS = 8

ROWS = DEC_SEQ // GRID_W
QROWS = TM // GRID_W
BAND_TILES = 3
BAND = BAND_TILES * TM

FF_TILE = 1024
FF_SUB = 256
FF_CH = 256
N_FF_CH = D_FF // FF_CH
R_MOE = 2 * T + N_EXPERTS * FF_TILE
NT_MOE = R_MOE // FF_TILE
NT_DENSE = T // FF_TILE
GATHER_CH = 512
GATHER_UNROLL = 16

VMEM_LIMIT = 56 * 1024 * 1024


def _group_of_tile(i):
    return jnp.where(i < NT_P, 0, 1 + (i - NT_P) // TILES_PER_DEC)


def _rms(x, g):
    return x * lax.rsqrt(jnp.mean(x * x, axis=-1, keepdims=True) + EPS) * g


def _silu(x):
    return x / (1.0 + jnp.exp(-x))


def _gelu_tanh(x):
    c = np.float32(np.sqrt(2.0 / np.pi))
    return 0.5 * x * (1.0 + jnp.tanh(c * (x + np.float32(0.044715) * (x * x * x))))


def _mod_kernel(cv_ref, w_ref, b_ref, o_ref):
    a = _silu(cv_ref[...])
    o_ref[...] = jnp.dot(a, w_ref[...], preferred_element_type=F32,
                         precision=lax.Precision.HIGHEST) + b_ref[...]


def _modulation(cvec, w_ada, b_ada):
    tn = 1536
    nn = (N_MOD * D_MODEL) // tn
    out = pl.pallas_call(
        _mod_kernel,
        grid=(DEPTH, nn),
        in_specs=[
            pl.BlockSpec((GROUPS, D_MODEL), lambda l, n: (0, 0)),
            pl.BlockSpec((None, D_MODEL, tn), lambda l, n: (l, 0, n)),
            pl.BlockSpec((None, 1, tn), lambda l, n: (l, 0, n)),
        ],
        out_specs=pl.BlockSpec((None, GROUPS, tn), lambda l, n: (l, 0, n)),
        out_shape=jax.ShapeDtypeStruct((DEPTH, GROUPS, N_MOD * D_MODEL), F32),
        compiler_params=pltpu.CompilerParams(
            dimension_semantics=("parallel", "parallel"), vmem_limit_bytes=VMEM_LIMIT),
        name="adaln_mod",
    )(cvec, w_ada, b_ada.reshape(DEPTH, 1, N_MOD * D_MODEL))
    return out.reshape(DEPTH, GROUPS, N_MOD, D_MODEL)


def _inproj_kernel(x_ref, mod_ref, g_ref, w_ref, kc_in_ref, vc_in_ref, p_ref, kc_ref, vc_ref):
    del kc_in_ref, vc_in_ref
    h = _rms(x_ref[...], g_ref[0:1, :]) * (1.0 + mod_ref[1:2, :]) + mod_ref[0:1, :]
    p_ref[...] = jnp.dot(h.astype(BF16), w_ref[...], preferred_element_type=F32)

    @pl.when(pl.program_id(0) < T_P // TM_IN)
    def _():
        for r in range(TM_IN // SEQ):
            rows = slice(r * SEQ, (r + 1) * SEQ)
            kc_ref[r] = p_ref[rows, D_ATT:2 * D_ATT]
            vc_ref[r] = p_ref[rows, 2 * D_ATT:3 * D_ATT]


def _inproj(x, mod, norm_g, w_in_bf, new_k, new_v, l):
    cache_spec = pl.BlockSpec((TM_IN // SEQ, None, SEQ, D_ATT),
                              lambda i: (jnp.minimum(i, T_P // TM_IN - 1), l, 0, 0))
    return pl.pallas_call(
        _inproj_kernel,
        grid=(T // TM_IN,),
        in_specs=[
            pl.BlockSpec((TM_IN, D_MODEL), lambda i: (i, 0)),
            pl.BlockSpec((None, None, N_MOD, D_MODEL),
                         lambda i: (l, _group_of_tile(i * (TM_IN // TM)), 0, 0)),
            pl.BlockSpec((None, 4, D_MODEL), lambda i: (l, 0, 0)),
            pl.BlockSpec((None, D_MODEL, D_IN), lambda i: (l, 0, 0)),
            pl.BlockSpec(memory_space=pl.ANY),
            pl.BlockSpec(memory_space=pl.ANY),
        ],
        out_specs=[pl.BlockSpec((TM_IN, D_IN), lambda i: (i, 0)), cache_spec, cache_spec],
        out_shape=[jax.ShapeDtypeStruct((T, D_IN), F32),
                   jax.ShapeDtypeStruct(new_k.shape, F32),
                   jax.ShapeDtypeStruct(new_v.shape, F32)],
        input_output_aliases={4: 1, 5: 2},
        compiler_params=pltpu.CompilerParams(
            dimension_semantics=("arbitrary",), vmem_limit_bytes=VMEM_LIMIT),
        name="inproj",
    )(x, mod, norm_g, w_in_bf, new_k, new_v)


def _dot_nt(a, b):
    return lax.dot_general(a, b, (((1,), (1,)), ((), ())), preferred_element_type=F32)


def _attn_kernel(q_ref, k0_ref, k1_ref, k2_ref, v0_ref, v1_ref, v2_ref,
                 ck_ref, cv_ref, bias_ref, o_ref):
    is_prompt = pl.program_id(0) < NT_P

    @pl.when(is_prompt)
    def _():
        _ctx_attn_body(q_ref, k0_ref, v0_ref, o_ref)

    @pl.when(jnp.logical_not(is_prompt))
    def _():
        _na_attn_body(q_ref, k0_ref, k1_ref, k2_ref, v0_ref, v1_ref, v2_ref,
                      ck_ref, cv_ref, bias_ref, o_ref)


def _ctx_attn_body(q_ref, k_ref, v_ref, o_ref):
    scale = np.float32(HEAD_DIM ** -0.5)
    for h in range(H_ATT):
        sl = slice(h * HEAD_DIM, (h + 1) * HEAD_DIM)
        q = (q_ref[:, sl] * scale).astype(BF16)
        k = k_ref[:, sl].astype(BF16)
        v = v_ref[:, sl].astype(BF16)
        s = _dot_nt(q, k)
        m = jnp.max(s, axis=-1, keepdims=True)
        e = jnp.exp(s - m)
        den = jnp.sum(e, axis=-1, keepdims=True)
        o = jnp.dot(e.astype(BF16), v, preferred_element_type=F32) / den
        o_ref[:, sl] = o.astype(BF16)


def _na_attn_body(q_ref, k0_ref, k1_ref, k2_ref, v0_ref, v1_ref, v2_ref,
                  ck_ref, cv_ref, bias_ref, o_ref):
    scale = np.float32(HEAD_DIM ** -0.5)
    k_refs = (k0_ref, k1_ref, k2_ref)
    v_refs = (v0_ref, v1_ref, v2_ref)
    for h in range(H_ATT):
        sl = slice(h * HEAD_DIM, (h + 1) * HEAD_DIM)
        q = (q_ref[:, sl] * scale).astype(BF16)
        s_loc = [_dot_nt(q, k_refs[j][:, sl].astype(BF16)) + bias_ref[h, :, j * TM:(j + 1) * TM]
                 for j in range(BAND_TILES)]
        s_ctx = _dot_nt(q, ck_ref[:, sl].astype(BF16))
        m = jnp.max(s_ctx, axis=-1, keepdims=True)
        for s in s_loc:
            m = jnp.maximum(m, jnp.max(s, axis=-1, keepdims=True))
        e_ctx = jnp.exp(s_ctx - m)
        den = jnp.sum(e_ctx, axis=-1, keepdims=True)
        acc = jnp.dot(e_ctx.astype(BF16), cv_ref[:, sl].astype(BF16), preferred_element_type=F32)
        for j in range(BAND_TILES):
            e = jnp.exp(s_loc[j] - m)
            den = den + jnp.sum(e, axis=-1, keepdims=True)
            acc = acc + jnp.dot(e.astype(BF16), v_refs[j][:, sl].astype(BF16),
                                preferred_element_type=F32)
        o_ref[:, sl] = (acc / den).astype(BF16)


def _na_variant_tables():
    kh = min(NA_KH, ROWS)
    per_tile = []
    for rb in range(ROWS // QROWS):
        r0 = rb * QROWS
        bs = int(np.clip(rb - 1, 0, ROWS // QROWS - BAND_TILES)) * QROWS
        tab = -np.ones((QROWS, BAND_TILES * QROWS), np.int32)
        for qr in range(QROWS):
            r = r0 + qr
            rs = int(np.clip(r - kh // 2, 0, ROWS - kh))
            for kr in range(BAND_TILES * QROWS):
                ka = bs + kr
                if rs <= ka < rs + kh:
                    tab[qr, kr] = ka - r + (NA_KH - 1)
            assert (tab[qr] >= 0).sum() == kh
        per_tile.append(tab)
    variants, variant_of_tile = [], []
    for tab in per_tile:
        for vi, v in enumerate(variants):
            if np.array_equal(v, tab):
                variant_of_tile.append(vi)
                break
        else:
            variants.append(tab)
            variant_of_tile.append(len(variants) - 1)
    return np.asarray(variant_of_tile, np.int32), np.stack(variants)


_NA_VARIANT_OF_TILE, _NA_DR_IDX = _na_variant_tables()
_NA_NVAR = _NA_DR_IDX.shape[0]
_N_DR = 2 * NA_KH - 1
_N_DC = 2 * NA_KW - 1


def _bias_kernel(w_ref, o_ref):
    qc = lax.broadcasted_iota(jnp.int32, (GRID_W, LANES), 0)
    lane = lax.broadcasted_iota(jnp.int32, (GRID_W, LANES), 1)
    kc = lane % GRID_W
    cs = jnp.clip(qc - NA_KW // 2, 0, GRID_W - NA_KW)
    col_ok = jnp.logical_and(kc >= cs, kc < cs + NA_KW)
    left = lane < GRID_W
    neg = jnp.full((GRID_W, LANES), NEG_INF, F32)
    cache = {}
    for vi in range(_NA_NVAR):
        for qr in range(QROWS):
            for pp in range(BAND_TILES * QROWS // 2):
                d0 = int(_NA_DR_IDX[vi, qr, 2 * pp])
                d1 = int(_NA_DR_IDX[vi, qr, 2 * pp + 1])
                if (d0, d1) not in cache:
                    if d0 < 0 and d1 < 0:
                        tile = neg
                    else:
                        u = (w_ref[pl.ds(d0 if d0 >= 0 else _N_DR, 1), :]
                             + w_ref[pl.ds(_N_DR + 1 + (d1 if d1 >= 0 else _N_DR), 1), :])
                        t = pltpu.roll(jnp.broadcast_to(u, (GRID_W, LANES)), 0, 1,
                                       stride=1, stride_axis=0)
                        ok = col_ok
                        if d0 < 0:
                            ok = jnp.logical_and(ok, jnp.logical_not(left))
                        if d1 < 0:
                            ok = jnp.logical_and(ok, left)
                        tile = jnp.where(ok, t, neg)
                    cache[(d0, d1)] = tile
                o_ref[vi, qr * GRID_W:(qr + 1) * GRID_W, pp * LANES:(pp + 1) * LANES] = cache[(d0, d1)]


def _na_bias_tables(rpb):
    half = NA_KW - 1
    zeros = jnp.zeros((DEPTH, H_ATT, _N_DR, LANES - _N_DC), F32)
    w_lo = jnp.concatenate([rpb[..., half:], zeros, rpb[..., :half]], axis=-1)
    w_lo = jnp.pad(w_lo, ((0, 0), (0, 0), (0, 1), (0, 0)))
    w_hi = jnp.roll(w_lo, GRID_W, axis=-1)
    w = jnp.concatenate([w_lo, w_hi], axis=2)
    return pl.pallas_call(
        _bias_kernel,
        grid=(DEPTH, H_ATT),
        in_specs=[pl.BlockSpec((None, None, 2 * (_N_DR + 1), LANES), lambda l, h: (l, h, 0, 0))],
        out_specs=pl.BlockSpec((None, _NA_NVAR, None, TM, BAND), lambda l, h: (l, 0, h, 0, 0)),
        out_shape=jax.ShapeDtypeStruct((DEPTH, _NA_NVAR, H_ATT, TM, BAND), F32),
        compiler_params=pltpu.CompilerParams(
            dimension_semantics=("parallel", "parallel"), vmem_limit_bytes=VMEM_LIMIT),
        name="na_bias",
    )(w)


def _attention(p, ck_all, cv_all, bias_all, l):
    nrb = TILES_PER_DEC
    var_of_tile = [int(v) for v in _NA_VARIANT_OF_TILE]

    def dec_batch(i):
        return jnp.maximum(i - NT_P, 0) // nrb

    def band_tile(i, j):
        rb = (i - NT_P) % nrb
        first = NT_P + dec_batch(i) * nrb + jnp.clip(rb - 1, 0, nrb - BAND_TILES)
        return jnp.where(i < NT_P, i, first + j)

    def variant(i):
        rb = jnp.maximum(i - NT_P, 0) % nrb
        v = jnp.int32(var_of_tile[0])
        for t in range(1, nrb):
            v = jnp.where(rb >= t, jnp.int32(var_of_tile[t]), v)
        return v

    kv_specs = [pl.BlockSpec((TM, D_ATT), functools.partial(
        lambda i, j, col: (band_tile(i, j), col), j=j, col=col))
        for col in (1, 2) for j in range(BAND_TILES)]
    return pl.pallas_call(
        _attn_kernel,
        grid=(NT,),
        in_specs=[pl.BlockSpec((TM, D_ATT), lambda i: (i, 0))] + kv_specs + [
            pl.BlockSpec((None, None, PAST_LEN, D_ATT), lambda i: (dec_batch(i), l, 0, 0)),
            pl.BlockSpec((None, None, PAST_LEN, D_ATT), lambda i: (dec_batch(i), l, 0, 0)),
            pl.BlockSpec((None, None, H_ATT, TM, BAND), lambda i: (l, variant(i), 0, 0, 0)),
        ],
        out_specs=pl.BlockSpec((TM, D_ATT), lambda i: (i, 0)),
        out_shape=jax.ShapeDtypeStruct((T, D_ATT), BF16),
        compiler_params=pltpu.CompilerParams(
            dimension_semantics=("arbitrary",), vmem_limit_bytes=VMEM_LIMIT),
        name="attn",
    )(p, p, p, p, p, p, p, ck_all, cv_all, bias_all)


def _mixer_out_kernel(with_router, *refs):
    if with_router:
        (u_ref, vm_ref, bg_ref, cg_ref, hx_ref, cgp_ref, hxp_ref, cgn_ref, hxn_ref,
         oatt_ref, x_ref, mod_ref, g_ref, gv_ref, ws_ref, bs_ref, wc_ref, wo_ref, wr_ref,
         x1_ref, h2_ref, route_ref) = refs
    else:
        (u_ref, vm_ref, bg_ref, cg_ref, hx_ref, cgp_ref, hxp_ref, cgn_ref, hxn_ref,
         oatt_ref, x_ref, mod_ref, g_ref, gv_ref, ws_ref, bs_ref, wc_ref, wo_ref,
         x1_ref, h2_ref) = refs
    i = pl.program_id(0)

    u = _gelu_tanh(u_ref[...])
    vm = _rms(_gelu_tanh(vm_ref[...]), gv_ref[...]).astype(BF16)
    chunks = []
    for c in range(TM // CHUNK):
        rows = slice(c * CHUNK, (c + 1) * CHUNK)
        heads = [jnp.dot(ws_ref[h].astype(BF16), vm[rows, h * HEAD_DIM:(h + 1) * HEAD_DIM],
                         preferred_element_type=F32) for h in range(H_CMLP)]
        chunks.append(jnp.concatenate(heads, axis=1) + bs_ref[...])
    o_mlp = u * jnp.concatenate(chunks, axis=0)

    j = (i - NT_P) % TILES_PER_DEC
    has_prev = jnp.logical_and(i >= NT_P, j > 0)
    has_next = jnp.logical_and(i >= NT_P, j < TILES_PER_DEC - 1)
    z = cg_ref[...] * hx_ref[...]
    z_halo_prev = jnp.where(has_prev, cgp_ref[7:8, :] * hxp_ref[7:8, :], 0.0)
    z_halo_next = jnp.where(has_next, cgn_ref[0:1, :] * hxn_ref[0:1, :], 0.0)
    row = lax.broadcasted_iota(jnp.int32, (TM, D_CONV), 0)
    z_prev = jnp.where(row == 0, z_halo_prev, pltpu.roll(z, 1, 0))
    z_next = jnp.where(row == TM - 1, z_halo_next, pltpu.roll(z, TM - 1, 0))
    y_conv = z_prev * wc_ref[0:1, :] + z * wc_ref[1:2, :] + z_next * wc_ref[2:3, :]
    o_conv = bg_ref[...] * y_conv

    y = jnp.dot(oatt_ref[...], wo_ref[0:D_ATT, :], preferred_element_type=F32)
    y = y + jnp.dot(o_mlp.astype(BF16), wo_ref[D_ATT:D_ATT + D_CMLP, :],
                    preferred_element_type=F32)
    y = y + jnp.dot(o_conv.astype(BF16), wo_ref[D_ATT + D_CMLP:, :],
                    preferred_element_type=F32)

    x1 = x_ref[...] + mod_ref[2:3, :] * _rms(y, g_ref[1:2, :])
    h2 = _rms(x1, g_ref[2:3, :]) * (1.0 + mod_ref[4:5, :]) + mod_ref[3:4, :]
    x1_ref[...] = x1
    if with_router:
        _store_row_tiled(h2_ref, h2, TM)
    else:
        h2_ref[...] = h2

    if with_router:
        lg = [jnp.sum(h2 * wr_ref[e:e + 1, :], axis=-1, keepdims=True) for e in range(N_EXPERTS)]

        def top1(cols):
            m = cols[0]
            for col in cols[1:]:
                m = jnp.maximum(m, col)
            idx = jnp.full_like(m, N_EXPERTS - 1)
            for e in range(N_EXPERTS - 2, -1, -1):
                idx = jnp.where(cols[e] == m, np.float32(e), idx)
            return m, idx

        m1, i1 = top1(lg)
        m2, i2 = top1([jnp.where(i1 == np.float32(e), -jnp.inf, lg[e]) for e in range(N_EXPERTS)])
        lane = lax.broadcasted_iota(jnp.int32, (TM, 128), 1)
        e2 = jnp.exp(m2 - m1)
        den = 1.0 + e2
        gate1 = 1.0 / den
        gate2 = e2 / den
        route = jnp.where(lane == 0, i1,
                          jnp.where(lane == 1, i2,
                                    jnp.where(lane == 2, gate1,
                                              jnp.where(lane == 3, gate2, 0.0))))
        route_ref[...] = route


def _mixer_out(p, o_att, x, mod, norm_g, g_v, w_s, bs_b, wc_t, w_out_bf, w_router_t, l, moe_idx):
    with_router = moe_idx is not None
    hb = TM // 8

    def col(cb):
        return pl.BlockSpec((TM, D_CMLP), lambda i: (i, cb))

    def halo_prev(cb):
        return pl.BlockSpec((8, D_CONV), lambda i: (jnp.maximum(i * hb - 1, 0), cb))

    def halo_next(cb):
        return pl.BlockSpec((8, D_CONV), lambda i: (jnp.minimum((i + 1) * hb, T // 8 - 1), cb))

    in_specs = [col(6), col(7), col(8), col(9), col(10),
                halo_prev(9), halo_prev(10), halo_next(9), halo_next(10),
                pl.BlockSpec((TM, D_ATT), lambda i: (i, 0)),
                pl.BlockSpec((TM, D_MODEL), lambda i: (i, 0)),
                pl.BlockSpec((None, None, N_MOD, D_MODEL), lambda i: (l, _group_of_tile(i), 0, 0)),
                pl.BlockSpec((None, 4, D_MODEL), lambda i: (l, 0, 0)),
                pl.BlockSpec((None, 1, D_CMLP), lambda i: (l, 0, 0)),
                pl.BlockSpec((None, H_CMLP, CHUNK, CHUNK), lambda i: (l, 0, 0, 0)),
                pl.BlockSpec((None, CHUNK, D_CMLP), lambda i: (l, 0, 0)),
                pl.BlockSpec((None, 3, D_CONV), lambda i: (l, 0, 0)),
                pl.BlockSpec((None, D_MODEL, D_MODEL), lambda i: (l, 0, 0))]
    args = [p, p, p, p, p, p, p, p, p, o_att, x, mod, norm_g, g_v, w_s, bs_b, wc_t, w_out_bf]
    out_specs = [pl.BlockSpec((TM, D_MODEL), lambda i: (i, 0)),
                 pl.BlockSpec((TM, D_MODEL), lambda i: (i, 0))]
    out_shape = [jax.ShapeDtypeStruct((T, D_MODEL), F32), jax.ShapeDtypeStruct((T, D_MODEL), F32)]
    if with_router:
        in_specs.append(pl.BlockSpec((None, N_EXPERTS, D_MODEL), lambda i: (moe_idx, 0, 0)))
        args.append(w_router_t)
        out_specs[1] = pl.BlockSpec((TM * ROW_SUB, LANES), lambda i: (i, 0))
        out_shape[1] = jax.ShapeDtypeStruct((T * ROW_SUB, LANES), F32)
        out_specs.append(pl.BlockSpec((TM, 128), lambda i: (i, 0)))
        out_shape.append(jax.ShapeDtypeStruct((T, 128), F32))
    return pl.pallas_call(
        functools.partial(_mixer_out_kernel, with_router),
        grid=(NT,),
        in_specs=in_specs,
        out_specs=out_specs,
        out_shape=out_shape,
        compiler_params=pltpu.CompilerParams(
            dimension_semantics=("parallel",), vmem_limit_bytes=VMEM_LIMIT),
        name="mixer_out_router" if with_router else "mixer_out",
    )(*args)


LANES = 128
ROW_SUB = D_MODEL // LANES


def _load_row_tiled(ref, n):
    return jnp.concatenate([ref[pl.ds(c, n, stride=ROW_SUB), :] for c in range(ROW_SUB)], axis=1)


def _store_row_tiled(ref, val, n):
    for c in range(ROW_SUB):
        ref[pl.ds(c, n, stride=ROW_SUB), :] = val[:, c * LANES:(c + 1) * LANES]


def _ffn_kernel(row_tiled, te_ref, nv_ref, et_ref, x_ref, wg_ref, wu_ref, wd_ref, *rest):
    if row_tiled:
        o_ref, xb_ref, acc_ref = rest
    else:
        x1_ref, mod_ref, g_ref, o_ref, xb_ref, acc_ref = rest
    del te_ref, et_ref
    t = pl.program_id(0)
    j = pl.program_id(1)
    nvalid = nv_ref[t]
    nsub = (nvalid + (FF_SUB - 1)) // FF_SUB

    @pl.when(j == 0)
    def _():
        acc_ref[...] = jnp.zeros((FF_TILE, D_MODEL), F32)

    @pl.when(jnp.logical_and(j == 0, nvalid >= 0))
    def _():
        if row_tiled:
            xb_ref[...] = _load_row_tiled(x_ref, FF_TILE).astype(BF16)
        else:
            xb_ref[...] = x_ref[...].astype(BF16)

    def swiglu_rows(n):
        wg = wg_ref[...].astype(BF16)
        wu = wu_ref[...].astype(BF16)
        wd = wd_ref[...].astype(BF16)
        xs = xb_ref[0:n, :]
        g = jnp.dot(xs, wg, preferred_element_type=F32)
        u = jnp.dot(xs, wu, preferred_element_type=F32)
        a = (_silu(g) * u).astype(BF16)
        acc_ref[0:n, :] += jnp.dot(a, wd, preferred_element_type=F32)

    if row_tiled:
        for k in range(1, FF_TILE // FF_SUB + 1):
            pl.when(nsub == k)(functools.partial(swiglu_rows, k * FF_SUB))
    else:
        swiglu_rows(FF_TILE)

    @pl.when(j == N_FF_CH - 1)
    def _():
        if row_tiled:
            _store_row_tiled(o_ref, acc_ref[...], FF_TILE)
        else:
            o_ref[...] = x1_ref[...] + mod_ref[5:6, :] * _rms(acc_ref[...], g_ref[3:4, :])


def _ffn(x_rows, w_gu, w_d, tile_expert, tile_nvalid, tile_eff, n_tiles, residual, name):
    def chunk(t, j, nv):
        return jnp.where(nv[t] >= 0, j, N_FF_CH - 1)

    row_tiled = residual is None
    if row_tiled:
        x_block = (FF_TILE * ROW_SUB, LANES)
        out_shape = jax.ShapeDtypeStruct((n_tiles * FF_TILE * ROW_SUB, LANES), F32)
        extra_specs, extra_args = [], ()
    else:
        x1, mod, norm_g, l = residual
        x_block = (FF_TILE, D_MODEL)
        out_shape = jax.ShapeDtypeStruct((n_tiles * FF_TILE, D_MODEL), F32)
        tiles_per_tm = FF_TILE // TM
        extra_specs = [
            pl.BlockSpec(x_block, lambda t, j, te, nv, et: (t, 0)),
            pl.BlockSpec((None, None, N_MOD, D_MODEL),
                         lambda t, j, te, nv, et: (l, _group_of_tile(t * tiles_per_tm), 0, 0)),
            pl.BlockSpec((None, 4, D_MODEL), lambda t, j, te, nv, et: (l, 0, 0)),
        ]
        extra_args = (x1, mod, norm_g)
    grid_spec = pltpu.PrefetchScalarGridSpec(
        num_scalar_prefetch=3,
        grid=(n_tiles, N_FF_CH),
        in_specs=[
            pl.BlockSpec(x_block, lambda t, j, te, nv, et: (et[t], 0)),
            pl.BlockSpec((None, D_MODEL, FF_CH),
                         lambda t, j, te, nv, et: (te[t], 0, chunk(t, j, nv))),
            pl.BlockSpec((None, D_MODEL, FF_CH),
                         lambda t, j, te, nv, et: (te[t], 0, N_FF_CH + chunk(t, j, nv))),
            pl.BlockSpec((None, FF_CH, D_MODEL),
                         lambda t, j, te, nv, et: (te[t], chunk(t, j, nv), 0)),
        ] + extra_specs,
        out_specs=pl.BlockSpec(x_block, lambda t, j, te, nv, et: (t, 0)),
        scratch_shapes=[pltpu.VMEM((FF_TILE, D_MODEL), BF16),
                        pltpu.VMEM((FF_TILE, D_MODEL), F32)],
    )
    return pl.pallas_call(
        functools.partial(_ffn_kernel, row_tiled),
        grid_spec=grid_spec,
        out_shape=out_shape,
        compiler_params=pltpu.CompilerParams(
            dimension_semantics=("arbitrary", "arbitrary"), vmem_limit_bytes=VMEM_LIMIT),
        name=name,
    )(tile_expert, tile_nvalid, tile_eff, x_rows, w_gu, w_gu, w_d, *extra_args)


def _dispatch_kernel(pos_ref, h_ref, xs_in_ref, xs_ref, sem):
    del xs_in_ref

    def row_copy(r, k):
        dst_row = pl.multiple_of(pos_ref[0, k * GATHER_CH + r] * ROW_SUB, ROW_SUB)
        src_row = pl.multiple_of(r * ROW_SUB, ROW_SUB)
        return pltpu.make_async_copy(h_ref.at[pl.ds(src_row, ROW_SUB)],
                                     xs_ref.at[pl.ds(dst_row, ROW_SUB)], sem)

    def start_group(g, carry):
        for u in range(GATHER_UNROLL):
            r = g * GATHER_UNROLL + u
            row_copy(r, 0).start(priority=0)
            row_copy(r, 1).start(priority=1)
        return carry

    lax.fori_loop(0, GATHER_CH // GATHER_UNROLL, start_group, 0)
    for _ in range(TOP_K):
        pltpu.make_async_copy(h_ref, xs_ref.at[pl.ds(0, GATHER_CH * ROW_SUB)], sem).wait()


def _dispatch(h2r, pos1, pos2):
    steps = T // GATHER_CH
    pos = jnp.concatenate([pos1.reshape(steps, 1, GATHER_CH), pos2.reshape(steps, 1, GATHER_CH)],
                          axis=2)
    return pl.pallas_call(
        _dispatch_kernel,
        grid=(steps,),
        in_specs=[pl.BlockSpec((None, 1, TOP_K * GATHER_CH), lambda i: (i, 0, 0),
                               memory_space=pltpu.SMEM),
                  pl.BlockSpec((GATHER_CH * ROW_SUB, LANES), lambda i: (i, 0)),
                  pl.BlockSpec(memory_space=pl.ANY)],
        out_specs=pl.BlockSpec(memory_space=pl.ANY),
        out_shape=jax.ShapeDtypeStruct((R_MOE * ROW_SUB, LANES), F32),
        input_output_aliases={2: 0},
        scratch_shapes=[pltpu.SemaphoreType.DMA(())],
        compiler_params=pltpu.CompilerParams(
            dimension_semantics=("arbitrary",), vmem_limit_bytes=VMEM_LIMIT),
        name="moe_dispatch",
    )(pos, h2r, jnp.zeros((R_MOE * ROW_SUB, LANES), F32))


def _combine_kernel(pos_ref, pos_next_ref, x1_ref, route_ref, mod_ref, g_ref, ys_ref, o_ref,
                    ybuf_ref, sems):
    i = pl.program_id(0)
    slot = i % 2
    n_rows = TOP_K * TM

    def start_rows(p_ref, s):
        def row_copy(r):
            src_row = pl.multiple_of(p_ref[0, r] * ROW_SUB, ROW_SUB)
            dst_row = pl.multiple_of(r * ROW_SUB, ROW_SUB)
            return pltpu.make_async_copy(ys_ref.at[pl.ds(src_row, ROW_SUB)],
                                         ybuf_ref.at[s, pl.ds(dst_row, ROW_SUB)], sems.at[s])

        def start_group(g, carry):
            for u in range(GATHER_UNROLL):
                row_copy(g * GATHER_UNROLL + u).start(priority=u % 2)
            return carry

        lax.fori_loop(0, n_rows // GATHER_UNROLL, start_group, 0)

    @pl.when(i == 0)
    def _():
        start_rows(pos_ref, 0)

    @pl.when(i + 1 < pl.num_programs(0))
    def _():
        start_rows(pos_next_ref, 1 - slot)

    pltpu.make_async_copy(ys_ref.at[pl.ds(0, n_rows * ROW_SUB)], ybuf_ref.at[slot],
                          sems.at[slot]).wait()
    ybuf = ybuf_ref.at[slot]
    y1 = jnp.concatenate([ybuf[pl.ds(c, TM, stride=ROW_SUB), :] for c in range(ROW_SUB)], axis=1)
    y2 = jnp.concatenate([ybuf[pl.ds(TM * ROW_SUB + c, TM, stride=ROW_SUB), :]
                          for c in range(ROW_SUB)], axis=1)
    f = route_ref[:, 2:3] * y1 + route_ref[:, 3:4] * y2
    o_ref[...] = x1_ref[...] + mod_ref[5:6, :] * _rms(f, g_ref[3:4, :])


def _combine(x1, ys, pos1, pos2, route, mod, norm_g, l):
    pos = jnp.concatenate([pos1.reshape(NT, 1, TM), pos2.reshape(NT, 1, TM)], axis=2)
    row = pl.BlockSpec((TM, D_MODEL), lambda i: (i, 0))
    pos_block = (None, 1, TOP_K * TM)
    return pl.pallas_call(
        _combine_kernel,
        grid=(NT,),
        in_specs=[pl.BlockSpec(pos_block, lambda i: (i, 0, 0), memory_space=pltpu.SMEM),
                  pl.BlockSpec(pos_block, lambda i: (jnp.minimum(i + 1, NT - 1), 0, 0),
                               memory_space=pltpu.SMEM),
                  row,
                  pl.BlockSpec((TM, 128), lambda i: (i, 0)),
                  pl.BlockSpec((None, None, N_MOD, D_MODEL),
                               lambda i: (l, _group_of_tile(i), 0, 0)),
                  pl.BlockSpec((None, 4, D_MODEL), lambda i: (l, 0, 0)),
                  pl.BlockSpec(memory_space=pl.ANY)],
        out_specs=row,
        out_shape=jax.ShapeDtypeStruct((T, D_MODEL), F32),
        scratch_shapes=[pltpu.VMEM((2, TOP_K * TM * ROW_SUB, LANES), F32),
                        pltpu.SemaphoreType.DMA((2,))],
        compiler_params=pltpu.CompilerParams(
            dimension_semantics=("arbitrary",), vmem_limit_bytes=VMEM_LIMIT),
        name="moe_combine",
    )(pos, pos, x1, route, mod, norm_g, ys)


def _route_plan(route):
    e1 = route[:, 0].astype(jnp.int32)
    e2 = route[:, 1].astype(jnp.int32)
    ar = jnp.arange(N_EXPERTS, dtype=jnp.int32)
    oh1 = (e1[:, None] == ar[None, :]).astype(jnp.int32)
    oh2 = (e2[:, None] == ar[None, :]).astype(jnp.int32)
    oh = oh1 + oh2
    csum_incl = jnp.cumsum(oh, axis=0)
    csum = csum_incl - oh
    counts = csum_incl[-1]
    ntile_e = (counts + FF_TILE - 1) // FF_TILE
    tile_end_e = jnp.cumsum(ntile_e)
    tile_start_e = tile_end_e - ntile_e
    gstart = tile_start_e * FF_TILE
    pos1 = jnp.sum((gstart[None, :] + csum) * oh1, axis=1)
    pos2 = jnp.sum((gstart[None, :] + csum) * oh2, axis=1)
    n_used = tile_end_e[-1]
    tiles = jnp.arange(NT_MOE, dtype=jnp.int32)
    eff = jnp.minimum(tiles, n_used - 1)
    te = jnp.sum((eff[:, None] >= tile_end_e[None, :]).astype(jnp.int32), axis=1)
    te = jnp.minimum(te, N_EXPERTS - 1)
    nvalid = jnp.clip(counts[te] - (eff - tile_start_e[te]) * FF_TILE, 0, FF_TILE)
    nvalid = jnp.where(tiles < n_used, nvalid, -1)
    return pos1, pos2, te.astype(jnp.int32), nvalid.astype(jnp.int32), eff.astype(jnp.int32)


def kernel(x_prompt, x_sample, cache_k, cache_v, c, c_ctx, w_ada, b_ada, norm_g, w_in, w_out,
           rpb, g_v, w_s, b_s, w_conv, w_ffn_gu, w_ffn_d, w_router, w_moe_gu, w_moe_d):
    x = jnp.concatenate([x_prompt.reshape(T_P, D_MODEL), x_sample.reshape(T_S, D_MODEL)], axis=0)
    cvec = jnp.concatenate([c_ctx[None], c, jnp.zeros((GROUPS - 1 - DEC_BATCH, D_MODEL), F32)],
                           axis=0)
    mod = _modulation(cvec, w_ada, b_ada)

    w_in_bf = w_in.astype(BF16)
    w_out_bf = w_out.astype(BF16)
    ck_all = cache_k.reshape(DEC_BATCH, DEPTH, PAST_LEN, D_ATT)
    cv_all = cache_v.reshape(DEC_BATCH, DEPTH, PAST_LEN, D_ATT)
    bias_all = _na_bias_tables(rpb)
    g_v3 = g_v.reshape(DEPTH, 1, D_CMLP)
    bs_b = jnp.repeat(jnp.swapaxes(b_s, 1, 2), HEAD_DIM, axis=2)
    wc_t = jnp.swapaxes(w_conv, 1, 2)
    w_router_t = jnp.swapaxes(w_router, 1, 2)
    w_moe_gu_all = w_moe_gu.reshape(-1, D_MODEL, 2 * D_FF)
    w_moe_d_all = w_moe_d.reshape(-1, D_FF, D_MODEL)

    dense_te = jnp.zeros((NT_DENSE,), jnp.int32)
    dense_nv = jnp.full((NT_DENSE,), FF_TILE, jnp.int32)
    dense_eff = jnp.arange(NT_DENSE, dtype=jnp.int32)

    new_k = jnp.zeros((BATCH, DEPTH, SEQ, D_ATT), F32)
    new_v = jnp.zeros((BATCH, DEPTH, SEQ, D_ATT), F32)
    for l in range(DEPTH):
        p, new_k, new_v = _inproj(x, mod, norm_g, w_in_bf, new_k, new_v, l)
        o_att = _attention(p, ck_all, cv_all, bias_all, l)
        if l % 2 == 0:
            x1, h2 = _mixer_out(p, o_att, x, mod, norm_g, g_v3, w_s, bs_b, wc_t, w_out_bf,
                                None, l, None)
            x = _ffn(h2, w_ffn_gu, w_ffn_d, dense_te + l // 2, dense_nv, dense_eff, NT_DENSE,
                     (x1, mod, norm_g, l), "ffn_dense")
        else:
            x1, h2, route = _mixer_out(p, o_att, x, mod, norm_g, g_v3, w_s, bs_b, wc_t, w_out_bf,
                                       w_router_t, l, l // 2)
            pos1, pos2, te, nv, eff = _route_plan(route)
            xs = _dispatch(h2, pos1, pos2)
            ys = _ffn(xs, w_moe_gu_all, w_moe_d_all, te + (l // 2) * N_EXPERTS, nv, eff, NT_MOE,
                      None, "ffn_moe")
            x = _combine(x1, ys, pos1, pos2, route, mod, norm_g, l)

    y_prompt = x[:T_P].reshape(BATCH, SEQ, D_MODEL)
    y_sample = x[T_P:].reshape(DEC_BATCH, DEC_SEQ, D_MODEL)
    cache_shape = (BATCH, DEPTH, SEQ, H_ATT, HEAD_DIM)
    return y_prompt, y_sample, new_k.reshape(cache_shape), new_v.reshape(cache_shape)
```

```python
import functools

import numpy as np
import jax
import jax.numpy as jnp
from jax import lax
from jax.experimental import pallas as pl
from jax.experimental.pallas import tpu as pltpu

F32 = jnp.float32
BF16 = jnp.bfloat16

D_MODEL = 1024
BATCH = 16
SEQ = 256
DEPTH = 4
DEC_BATCH = 4
DEC_SEQ = 2048
PAST_LEN = 256
GRID_W = 64
HEAD_DIM = 64
D_ATT = 512
D_CMLP = 256
D_CONV = 256
H_ATT = 8
H_CMLP = 4
CHUNK = 128
NA_KH = 8
NA_KW = 16
D_IN = 2816
D_FF = 2816
N_EXPERTS = 8
TOP_K = 2
N_MOD = 6
EPS = 1e-6
NEG_INF = -1e30

T_P = BATCH * SEQ
T_S = DEC_BATCH * DEC_SEQ
T = T_P + T_S
TM = 256
NT = T // TM
NT_P = T_P // TM
TM_IN = 512
TILES_PER_DEC = DEC_SEQ // TM
GROUPS = 8

ROWS = DEC_SEQ // GRID_W
QROWS = TM // GRID_W
BAND_TILES = 3
BAND = BAND_TILES * TM

FF_TILE = 1024
FF_SUB = 256
FF_CH = 256
N_FF_CH = D_FF // FF_CH
N_FF_STEPS = (N_FF_CH + 1) // 2
R_MOE = 2 * T + N_EXPERTS * FF_TILE
NT_MOE = R_MOE // FF_TILE
NT_DENSE = T // FF_TILE
GATHER_CH = 512
GATHER_UNROLL = 16

VMEM_LIMIT = 56 * 1024 * 1024


def _group_of_tile(i):
    return jnp.where(i < NT_P, 0, 1 + (i - NT_P) // TILES_PER_DEC)


def _rms(x, g):
    return x * lax.rsqrt(jnp.mean(x * x, axis=-1, keepdims=True) + EPS) * g


def _silu(x):
    return x / (1.0 + jnp.exp(-x))


def _gelu_tanh(x):
    c = np.float32(np.sqrt(2.0 / np.pi))
    return 0.5 * x * (1.0 + jnp.tanh(c * (x + np.float32(0.044715) * (x * x * x))))


def _mod_kernel(cv_ref, w_ref, b_ref, o_ref):
    a = _silu(cv_ref[...])
    o_ref[...] = jnp.dot(a, w_ref[...], preferred_element_type=F32,
                         precision=lax.Precision.HIGHEST) + b_ref[...]


def _modulation(cvec, w_ada, b_ada):
    tn = 1536
    nn = (N_MOD * D_MODEL) // tn
    out = pl.pallas_call(
        _mod_kernel,
        grid=(DEPTH, nn),
        in_specs=[
            pl.BlockSpec((GROUPS, D_MODEL), lambda l, n: (0, 0)),
            pl.BlockSpec((None, D_MODEL, tn), lambda l, n: (l, 0, n)),
            pl.BlockSpec((None, 1, tn), lambda l, n: (l, 0, n)),
        ],
        out_specs=pl.BlockSpec((None, GROUPS, tn), lambda l, n: (l, 0, n)),
        out_shape=jax.ShapeDtypeStruct((DEPTH, GROUPS, N_MOD * D_MODEL), F32),
        compiler_params=pltpu.CompilerParams(
            dimension_semantics=("parallel", "parallel"), vmem_limit_bytes=VMEM_LIMIT),
        name="adaln_mod",
    )(cvec, w_ada, b_ada.reshape(DEPTH, 1, N_MOD * D_MODEL))
    return out.reshape(DEPTH, GROUPS, N_MOD, D_MODEL)


def _inproj_kernel(x_ref, mod_ref, g_ref, w_ref, kc_in_ref, vc_in_ref, p_ref, kc_ref, vc_ref,
                   wb_ref):
    del kc_in_ref, vc_in_ref

    @pl.when(pl.program_id(0) == 0)
    def _():
        wb_ref[...] = w_ref[...].astype(BF16)

    h = _rms(x_ref[...], g_ref[0:1, :]) * (1.0 + mod_ref[1:2, :]) + mod_ref[0:1, :]
    p_ref[...] = jnp.dot(h.astype(BF16), wb_ref[...], preferred_element_type=F32)

    @pl.when(pl.program_id(0) < T_P // TM_IN)
    def _():
        for r in range(TM_IN // SEQ):
            rows = slice(r * SEQ, (r + 1) * SEQ)
            kc_ref[r] = p_ref[rows, D_ATT:2 * D_ATT]
            vc_ref[r] = p_ref[rows, 2 * D_ATT:3 * D_ATT]


def _inproj(x, mod, norm_g, w_in, new_k, new_v, l):
    cache_spec = pl.BlockSpec((TM_IN // SEQ, None, SEQ, D_ATT),
                              lambda i: (jnp.minimum(i, T_P // TM_IN - 1), l, 0, 0))
    return pl.pallas_call(
        _inproj_kernel,
        grid=(T // TM_IN,),
        in_specs=[
            pl.BlockSpec((TM_IN, D_MODEL), lambda i: (i, 0)),
            pl.BlockSpec((None, None, N_MOD, D_MODEL),
                         lambda i: (l, _group_of_tile(i * (TM_IN // TM)), 0, 0)),
            pl.BlockSpec((None, 4, D_MODEL), lambda i: (l, 0, 0)),
            pl.BlockSpec((None, D_MODEL, D_IN), lambda i: (l, 0, 0),
                         pipeline_mode=pl.Buffered(1)),
            pl.BlockSpec(memory_space=pl.ANY),
            pl.BlockSpec(memory_space=pl.ANY),
        ],
        out_specs=[pl.BlockSpec((TM_IN, D_IN), lambda i: (i, 0)), cache_spec, cache_spec],
        out_shape=[jax.ShapeDtypeStruct((T, D_IN), F32),
                   jax.ShapeDtypeStruct(new_k.shape, F32),
                   jax.ShapeDtypeStruct(new_v.shape, F32)],
        input_output_aliases={4: 1, 5: 2},
        scratch_shapes=[pltpu.VMEM((D_MODEL, D_IN), BF16)],
        compiler_params=pltpu.CompilerParams(
            dimension_semantics=("arbitrary",), vmem_limit_bytes=VMEM_LIMIT),
        name="inproj",
    )(x, mod, norm_g, w_in, new_k, new_v)


def _dot_nt(a, b):
    return lax.dot_general(a, b, (((1,), (1,)), ((), ())), preferred_element_type=F32)


def _attn_kernel(q_ref, k0_ref, k1_ref, k2_ref, v0_ref, v1_ref, v2_ref,
                 ck_ref, cv_ref, bias_ref, o_ref):
    is_prompt = pl.program_id(0) < NT_P

    @pl.when(is_prompt)
    def _():
        _ctx_attn_body(q_ref, k0_ref, v0_ref, o_ref)

    @pl.when(jnp.logical_not(is_prompt))
    def _():
        _na_attn_body(q_ref, k0_ref, k1_ref, k2_ref, v0_ref, v1_ref, v2_ref,
                      ck_ref, cv_ref, bias_ref, o_ref)


def _ctx_attn_body(q_ref, k_ref, v_ref, o_ref):
    scale = np.float32(HEAD_DIM ** -0.5)
    for h in range(H_ATT):
        sl = slice(h * HEAD_DIM, (h + 1) * HEAD_DIM)
        q = (q_ref[:, sl] * scale).astype(BF16)
        k = k_ref[:, sl].astype(BF16)
        v = v_ref[:, sl].astype(BF16)
        s = _dot_nt(q, k)
        m = jnp.max(s, axis=-1, keepdims=True)
        e = jnp.exp(s - m)
        den = jnp.sum(e, axis=-1, keepdims=True)
        o = jnp.dot(e.astype(BF16), v, preferred_element_type=F32) / den
        o_ref[:, sl] = o.astype(BF16)


def _na_attn_body(q_ref, k0_ref, k1_ref, k2_ref, v0_ref, v1_ref, v2_ref,
                  ck_ref, cv_ref, bias_ref, o_ref):
    scale = np.float32(HEAD_DIM ** -0.5)
    k_refs = (k0_ref, k1_ref, k2_ref)
    v_refs = (v0_ref, v1_ref, v2_ref)
    for h in range(H_ATT):
        sl = slice(h * HEAD_DIM, (h + 1) * HEAD_DIM)
        q = (q_ref[:, sl] * scale).astype(BF16)
        s_loc = [_dot_nt(q, k_refs[j][:, sl].astype(BF16)) + bias_ref[h, :, j * TM:(j + 1) * TM]
                 for j in range(BAND_TILES)]
        s_ctx = _dot_nt(q, ck_ref[:, sl].astype(BF16))
        m = jnp.max(s_ctx, axis=-1, keepdims=True)
        for s in s_loc:
            m = jnp.maximum(m, jnp.max(s, axis=-1, keepdims=True))
        e_ctx = jnp.exp(s_ctx - m)
        den = jnp.sum(e_ctx, axis=-1, keepdims=True)
        acc = jnp.dot(e_ctx.astype(BF16), cv_ref[:, sl].astype(BF16), preferred_element_type=F32)
        for j in range(BAND_TILES):
            e = jnp.exp(s_loc[j] - m)
            den = den + jnp.sum(e, axis=-1, keepdims=True)
            acc = acc + jnp.dot(e.astype(BF16), v_refs[j][:, sl].astype(BF16),
                                preferred_element_type=F32)
        o_ref[:, sl] = (acc / den).astype(BF16)


def _na_variant_tables():
    kh = min(NA_KH, ROWS)
    per_tile = []
    for rb in range(ROWS // QROWS):
        r0 = rb * QROWS
        bs = int(np.clip(rb - 1, 0, ROWS // QROWS - BAND_TILES)) * QROWS
        tab = -np.ones((QROWS, BAND_TILES * QROWS), np.int32)
        for qr in range(QROWS):
            r = r0 + qr
            rs = int(np.clip(r - kh // 2, 0, ROWS - kh))
            for kr in range(BAND_TILES * QROWS):
                ka = bs + kr
                if rs <= ka < rs + kh:
                    tab[qr, kr] = ka - r + (NA_KH - 1)
            assert (tab[qr] >= 0).sum() == kh
        per_tile.append(tab)
    variants, variant_of_tile = [], []
    for tab in per_tile:
        for vi, v in enumerate(variants):
            if np.array_equal(v, tab):
                variant_of_tile.append(vi)
                break
        else:
            variants.append(tab)
            variant_of_tile.append(len(variants) - 1)
    return np.asarray(variant_of_tile, np.int32), np.stack(variants)


_NA_VARIANT_OF_TILE, _NA_DR_IDX = _na_variant_tables()
_NA_NVAR = _NA_DR_IDX.shape[0]
_N_DR = 2 * NA_KH - 1
_N_DC = 2 * NA_KW - 1


def _bias_kernel(w_ref, o_ref):
    qc = lax.broadcasted_iota(jnp.int32, (GRID_W, LANES), 0)
    lane = lax.broadcasted_iota(jnp.int32, (GRID_W, LANES), 1)
    kc = lane % GRID_W
    cs = jnp.clip(qc - NA_KW // 2, 0, GRID_W - NA_KW)
    col_ok = jnp.logical_and(kc >= cs, kc < cs + NA_KW)
    left = lane < GRID_W
    neg = jnp.full((GRID_W, LANES), NEG_INF, F32)
    cache = {}
    for vi in range(_NA_NVAR):
        for qr in range(QROWS):
            for pp in range(BAND_TILES * QROWS // 2):
                d0 = int(_NA_DR_IDX[vi, qr, 2 * pp])
                d1 = int(_NA_DR_IDX[vi, qr, 2 * pp + 1])
                if (d0, d1) not in cache:
                    if d0 < 0 and d1 < 0:
                        tile = neg
                    else:
                        u = (w_ref[pl.ds(d0 if d0 >= 0 else _N_DR, 1), :]
                             + w_ref[pl.ds(_N_DR + 1 + (d1 if d1 >= 0 else _N_DR), 1), :])
                        t = pltpu.roll(jnp.broadcast_to(u, (GRID_W, LANES)), 0, 1,
                                       stride=1, stride_axis=0)
                        ok = col_ok
                        if d0 < 0:
                            ok = jnp.logical_and(ok, jnp.logical_not(left))
                        if d1 < 0:
                            ok = jnp.logical_and(ok, left)
                        tile = jnp.where(ok, t, neg)
                    cache[(d0, d1)] = tile
                o_ref[vi, qr * GRID_W:(qr + 1) * GRID_W, pp * LANES:(pp + 1) * LANES] = cache[(d0, d1)]


def _na_bias_tables(rpb):
    half = NA_KW - 1
    zeros = jnp.zeros((DEPTH, H_ATT, _N_DR, LANES - _N_DC), F32)
    w_lo = jnp.concatenate([rpb[..., half:], zeros, rpb[..., :half]], axis=-1)
    w_lo = jnp.pad(w_lo, ((0, 0), (0, 0), (0, 1), (0, 0)))
    w_hi = jnp.roll(w_lo, GRID_W, axis=-1)
    w = jnp.concatenate([w_lo, w_hi], axis=2)
    return pl.pallas_call(
        _bias_kernel,
        grid=(DEPTH, H_ATT),
        in_specs=[pl.BlockSpec((None, None, 2 * (_N_DR + 1), LANES), lambda l, h: (l, h, 0, 0))],
        out_specs=pl.BlockSpec((None, _NA_NVAR, None, TM, BAND), lambda l, h: (l, 0, h, 0, 0)),
        out_shape=jax.ShapeDtypeStruct((DEPTH, _NA_NVAR, H_ATT, TM, BAND), F32),
        compiler_params=pltpu.CompilerParams(
            dimension_semantics=("parallel", "parallel"), vmem_limit_bytes=VMEM_LIMIT),
        name="na_bias",
    )(w)


def _attention(p, ck_all, cv_all, bias_all, l):
    nrb = TILES_PER_DEC
    var_of_tile = [int(v) for v in _NA_VARIANT_OF_TILE]

    def dec_batch(i):
        return jnp.maximum(i - NT_P, 0) // nrb

    def band_tile(i, j):
        rb = (i - NT_P) % nrb
        first = NT_P + dec_batch(i) * nrb + jnp.clip(rb - 1, 0, nrb - BAND_TILES)
        return jnp.where(i < NT_P, i, first + j)

    def variant(i):
        rb = jnp.maximum(i - NT_P, 0) % nrb
        v = jnp.int32(var_of_tile[0])
        for t in range(1, nrb):
            v = jnp.where(rb >= t, jnp.int32(var_of_tile[t]), v)
        return v

    kv_specs = [pl.BlockSpec((TM, D_ATT), functools.partial(
        lambda i, j, col: (band_tile(i, j), col), j=j, col=col))
        for col in (1, 2) for j in range(BAND_TILES)]
    return pl.pallas_call(
        _attn_kernel,
        grid=(NT,),
        in_specs=[pl.BlockSpec((TM, D_ATT), lambda i: (i, 0))] + kv_specs + [
            pl.BlockSpec((None, None, PAST_LEN, D_ATT), lambda i: (dec_batch(i), l, 0, 0)),
            pl.BlockSpec((None, None, PAST_LEN, D_ATT), lambda i: (dec_batch(i), l, 0, 0)),
            pl.BlockSpec((None, None, H_ATT, TM, BAND), lambda i: (l, variant(i), 0, 0, 0)),
        ],
        out_specs=pl.BlockSpec((TM, D_ATT), lambda i: (i, 0)),
        out_shape=jax.ShapeDtypeStruct((T, D_ATT), BF16),
        compiler_params=pltpu.CompilerParams(
            dimension_semantics=("arbitrary",), vmem_limit_bytes=VMEM_LIMIT),
        name="attn",
    )(p, p, p, p, p, p, p, ck_all, cv_all, bias_all)


def _mixer_out_kernel(with_router, *refs):
    if with_router:
        (u_ref, vm_ref, bg_ref, cg_ref, hx_ref, cgp_ref, hxp_ref, cgn_ref, hxn_ref,
         oatt_ref, x_ref, mod_ref, g_ref, gv_ref, ws_ref, bs_ref, wc_ref, wo_ref, wr_ref,
         x1_ref, h2_ref, route_ref, wob_ref) = refs
    else:
        (u_ref, vm_ref, bg_ref, cg_ref, hx_ref, cgp_ref, hxp_ref, cgn_ref, hxn_ref,
         oatt_ref, x_ref, mod_ref, g_ref, gv_ref, ws_ref, bs_ref, wc_ref, wo_ref,
         x1_ref, h2_ref, wob_ref) = refs
    i = pl.program_id(0)

    @pl.when(i == 0)
    def _():
        wob_ref[...] = wo_ref[...].astype(BF16)

    u = _gelu_tanh(u_ref[...])
    vm = _rms(_gelu_tanh(vm_ref[...]), gv_ref[...]).astype(BF16)
    chunks = []
    for c in range(TM // CHUNK):
        rows = slice(c * CHUNK, (c + 1) * CHUNK)
        heads = [jnp.dot(ws_ref[h].astype(BF16), vm[rows, h * HEAD_DIM:(h + 1) * HEAD_DIM],
                         preferred_element_type=F32) for h in range(H_CMLP)]
        chunks.append(jnp.concatenate(heads, axis=1) + bs_ref[...])
    o_mlp = u * jnp.concatenate(chunks, axis=0)

    j = (i - NT_P) % TILES_PER_DEC
    has_prev = jnp.logical_and(i >= NT_P, j > 0)
    has_next = jnp.logical_and(i >= NT_P, j < TILES_PER_DEC - 1)
    z = cg_ref[...] * hx_ref[...]
    z_halo_prev = jnp.where(has_prev, cgp_ref[7:8, :] * hxp_ref[7:8, :], 0.0)
    z_halo_next = jnp.where(has_next, cgn_ref[0:1, :] * hxn_ref[0:1, :], 0.0)
    row = lax.broadcasted_iota(jnp.int32, (TM, D_CONV), 0)
    z_prev = jnp.where(row == 0, z_halo_prev, pltpu.roll(z, 1, 0))
    z_next = jnp.where(row == TM - 1, z_halo_next, pltpu.roll(z, TM - 1, 0))
    y_conv = z_prev * wc_ref[0:1, :] + z * wc_ref[1:2, :] + z_next * wc_ref[2:3, :]
    o_conv = bg_ref[...] * y_conv

    y = jnp.dot(oatt_ref[...], wob_ref[0:D_ATT, :], preferred_element_type=F32)
    y = y + jnp.dot(o_mlp.astype(BF16), wob_ref[D_ATT:D_ATT + D_CMLP, :],
                    preferred_element_type=F32)
    y = y + jnp.dot(o_conv.astype(BF16), wob_ref[D_ATT + D_CMLP:, :],
                    preferred_element_type=F32)

    x1 = x_ref[...] + mod_ref[2:3, :] * _rms(y, g_ref[1:2, :])
    h2 = _rms(x1, g_ref[2:3, :]) * (1.0 + mod_ref[4:5, :]) + mod_ref[3:4, :]
    x1_ref[...] = x1
    if with_router:
        _store_row_tiled(h2_ref, h2, TM)
    else:
        h2_ref[...] = h2

    if with_router:
        lg = [jnp.sum(h2 * wr_ref[e:e + 1, :], axis=-1, keepdims=True) for e in range(N_EXPERTS)]

        def top1(cols):
            m = cols[0]
            for col in cols[1:]:
                m = jnp.maximum(m, col)
            idx = jnp.full_like(m, N_EXPERTS - 1)
            for e in range(N_EXPERTS - 2, -1, -1):
                idx = jnp.where(cols[e] == m, np.float32(e), idx)
            return m, idx

        m1, i1 = top1(lg)
        m2, i2 = top1([jnp.where(i1 == np.float32(e), -jnp.inf, lg[e]) for e in range(N_EXPERTS)])
        lane = lax.broadcasted_iota(jnp.int32, (TM, 128), 1)
        e2 = jnp.exp(m2 - m1)
        den = 1.0 + e2
        gate1 = 1.0 / den
        gate2 = e2 / den
        route = jnp.where(lane == 0, i1,
                          jnp.where(lane == 1, i2,
                                    jnp.where(lane == 2, gate1,
                                              jnp.where(lane == 3, gate2, 0.0))))
        route_ref[...] = route


def _mixer_out(p, o_att, x, mod, norm_g, g_v, w_s, bs_b, wc_t, w_out, w_router_t, l, moe_idx):
    with_router = moe_idx is not None
    hb = TM // 8

    def col(cb):
        return pl.BlockSpec((TM, D_CMLP), lambda i: (i, cb))

    def halo_prev(cb):
        return pl.BlockSpec((8, D_CONV), lambda i: (jnp.maximum(i * hb - 1, 0), cb))

    def halo_next(cb):
        return pl.BlockSpec((8, D_CONV), lambda i: (jnp.minimum((i + 1) * hb, T // 8 - 1), cb))

    in_specs = [col(6), col(7), col(8), col(9), col(10),
                halo_prev(9), halo_prev(10), halo_next(9), halo_next(10),
                pl.BlockSpec((TM, D_ATT), lambda i: (i, 0)),
                pl.BlockSpec((TM, D_MODEL), lambda i: (i, 0)),
                pl.BlockSpec((None, None, N_MOD, D_MODEL), lambda i: (l, _group_of_tile(i), 0, 0)),
                pl.BlockSpec((None, 4, D_MODEL), lambda i: (l, 0, 0)),
                pl.BlockSpec((None, 1, D_CMLP), lambda i: (l, 0, 0)),
                pl.BlockSpec((None, H_CMLP, CHUNK, CHUNK), lambda i: (l, 0, 0, 0)),
                pl.BlockSpec((None, CHUNK, D_CMLP), lambda i: (l, 0, 0)),
                pl.BlockSpec((None, 3, D_CONV), lambda i: (l, 0, 0)),
                pl.BlockSpec((None, D_MODEL, D_MODEL), lambda i: (l, 0, 0),
                             pipeline_mode=pl.Buffered(1))]
    args = [p, p, p, p, p, p, p, p, p, o_att, x, mod, norm_g, g_v, w_s, bs_b, wc_t, w_out]
    out_specs = [pl.BlockSpec((TM, D_MODEL), lambda i: (i, 0)),
                 pl.BlockSpec((TM, D_MODEL), lambda i: (i, 0))]
    out_shape = [jax.ShapeDtypeStruct((T, D_MODEL), F32), jax.ShapeDtypeStruct((T, D_MODEL), F32)]
    if with_router:
        in_specs.append(pl.BlockSpec((None, N_EXPERTS, D_MODEL), lambda i: (moe_idx, 0, 0)))
        args.append(w_router_t)
        out_specs[1] = pl.BlockSpec((TM * ROW_SUB, LANES), lambda i: (i, 0))
        out_shape[1] = jax.ShapeDtypeStruct((T * ROW_SUB, LANES), F32)
        out_specs.append(pl.BlockSpec((TM, 128), lambda i: (i, 0)))
        out_shape.append(jax.ShapeDtypeStruct((T, 128), F32))
    return pl.pallas_call(
        functools.partial(_mixer_out_kernel, with_router),
        grid=(NT,),
        in_specs=in_specs,
        out_specs=out_specs,
        out_shape=out_shape,
        scratch_shapes=[pltpu.VMEM((D_MODEL, D_MODEL), BF16)],
        compiler_params=pltpu.CompilerParams(
            dimension_semantics=("arbitrary",), vmem_limit_bytes=VMEM_LIMIT),
        name="mixer_out_router" if with_router else "mixer_out",
    )(*args)


LANES = 128
ROW_SUB = D_MODEL // LANES


def _load_row_tiled(ref, n):
    return jnp.concatenate([ref[pl.ds(c, n, stride=ROW_SUB), :] for c in range(ROW_SUB)], axis=1)


def _store_row_tiled(ref, val, n):
    for c in range(ROW_SUB):
        ref[pl.ds(c, n, stride=ROW_SUB), :] = val[:, c * LANES:(c + 1) * LANES]


def _ffn_kernel(row_tiled, te_ref, nv_ref, et_ref, x_ref, wga_ref, wua_ref, wda_ref,
                wgb_ref, wub_ref, wdb_ref, *rest):
    if row_tiled:
        o_ref, xb_ref, acc_ref = rest
    else:
        x1_ref, mod_ref, g_ref, o_ref, xb_ref, acc_ref = rest
    del te_ref, et_ref
    t = pl.program_id(0)
    j = pl.program_id(1)
    nvalid = nv_ref[t]
    nsub = (nvalid + (FF_SUB - 1)) // FF_SUB
    last = j == N_FF_STEPS - 1

    @pl.when(j == 0)
    def _():
        acc_ref[...] = jnp.zeros((FF_TILE, D_MODEL), F32)

    @pl.when(jnp.logical_and(j == 0, nvalid >= 0))
    def _():
        if row_tiled:
            xb_ref[...] = _load_row_tiled(x_ref, FF_TILE).astype(BF16)
        else:
            xb_ref[...] = x_ref[...].astype(BF16)

    def swiglu_rows(n, chunks):
        xs = xb_ref[0:n, :]
        acts, wds = [], []
        for wg_ref, wu_ref, wd_ref in chunks:
            g = jnp.dot(xs, wg_ref[...].astype(BF16), preferred_element_type=F32)
            u = jnp.dot(xs, wu_ref[...].astype(BF16), preferred_element_type=F32)
            acts.append((_silu(g) * u).astype(BF16))
            wds.append(wd_ref[...].astype(BF16))
        a = acts[0] if len(acts) == 1 else jnp.concatenate(acts, axis=1)
        wd = wds[0] if len(wds) == 1 else jnp.concatenate(wds, axis=0)
        acc_ref[0:n, :] += jnp.dot(a, wd, preferred_element_type=F32)

    chunk_a = (wga_ref, wua_ref, wda_ref)
    chunk_b = (wgb_ref, wub_ref, wdb_ref)
    n_options = range(1, FF_TILE // FF_SUB + 1) if row_tiled else (FF_TILE // FF_SUB,)
    for k in n_options:
        rows_ok = (nsub == k) if row_tiled else True
        pl.when(jnp.logical_and(rows_ok, jnp.logical_not(last)))(
            functools.partial(swiglu_rows, k * FF_SUB, (chunk_a, chunk_b)))
        pl.when(jnp.logical_and(rows_ok, last))(
            functools.partial(swiglu_rows, k * FF_SUB, (chunk_a,)))

    @pl.when(last)
    def _():
        if row_tiled:
            _store_row_tiled(o_ref, acc_ref[...], FF_TILE)
        else:
            o_ref[...] = x1_ref[...] + mod_ref[5:6, :] * _rms(acc_ref[...], g_ref[3:4, :])


def _ffn(x_rows, w_gu, w_d, tile_expert, tile_nvalid, tile_eff, n_tiles, residual, name):
    def chunk(t, j, nv, second):
        c = jnp.minimum(2 * j + second, N_FF_CH - 1)
        return jnp.where(nv[t] >= 0, c, N_FF_CH - 1)

    def weight_specs(second):
        return [
            pl.BlockSpec((None, D_MODEL, FF_CH),
                         lambda t, j, te, nv, et: (te[t], 0, chunk(t, j, nv, second))),
            pl.BlockSpec((None, D_MODEL, FF_CH),
                         lambda t, j, te, nv, et: (te[t], 0, N_FF_CH + chunk(t, j, nv, second))),
            pl.BlockSpec((None, FF_CH, D_MODEL),
                         lambda t, j, te, nv, et: (te[t], chunk(t, j, nv, second), 0)),
        ]

    row_tiled = residual is None
    if row_tiled:
        x_block = (FF_TILE * ROW_SUB, LANES)
        out_shape = jax.ShapeDtypeStruct((n_tiles * FF_TILE * ROW_SUB, LANES), F32)
        extra_specs, extra_args = [], ()
    else:
        x1, mod, norm_g, l = residual
        x_block = (FF_TILE, D_MODEL)
        out_shape = jax.ShapeDtypeStruct((n_tiles * FF_TILE, D_MODEL), F32)
        tiles_per_tm = FF_TILE // TM
        extra_specs = [
            pl.BlockSpec(x_block, lambda t, j, te, nv, et: (t, 0)),
            pl.BlockSpec((None, None, N_MOD, D_MODEL),
                         lambda t, j, te, nv, et: (l, _group_of_tile(t * tiles_per_tm), 0, 0)),
            pl.BlockSpec((None, 4, D_MODEL), lambda t, j, te, nv, et: (l, 0, 0)),
        ]
        extra_args = (x1, mod, norm_g)
    grid_spec = pltpu.PrefetchScalarGridSpec(
        num_scalar_prefetch=3,
        grid=(n_tiles, N_FF_STEPS),
        in_specs=[pl.BlockSpec(x_block, lambda t, j, te, nv, et: (et[t], 0))]
        + weight_specs(0) + weight_specs(1) + extra_specs,
        out_specs=pl.BlockSpec(x_block, lambda t, j, te, nv, et: (t, 0)),
        scratch_shapes=[pltpu.VMEM((FF_TILE, D_MODEL), BF16),
                        pltpu.VMEM((FF_TILE, D_MODEL), F32)],
    )
    return pl.pallas_call(
        functools.partial(_ffn_kernel, row_tiled),
        grid_spec=grid_spec,
        out_shape=out_shape,
        compiler_params=pltpu.CompilerParams(
            dimension_semantics=("arbitrary", "arbitrary"), vmem_limit_bytes=VMEM_LIMIT),
        name=name,
    )(tile_expert, tile_nvalid, tile_eff, x_rows, w_gu, w_gu, w_d, w_gu, w_gu, w_d, *extra_args)


def _dispatch_kernel(pos_ref, h_ref, xs_in_ref, xs_ref, sem):
    del xs_in_ref

    def row_copy(r, k):
        dst_row = pl.multiple_of(pos_ref[0, k * GATHER_CH + r] * ROW_SUB, ROW_SUB)
        src_row = pl.multiple_of(r * ROW_SUB, ROW_SUB)
        return pltpu.make_async_copy(h_ref.at[pl.ds(src_row, ROW_SUB)],
                                     xs_ref.at[pl.ds(dst_row, ROW_SUB)], sem)

    def start_group(g, carry):
        for u in range(GATHER_UNROLL):
            r = g * GATHER_UNROLL + u
            row_copy(r, 0).start(priority=0)
            row_copy(r, 1).start(priority=1)
        return carry

    lax.fori_loop(0, GATHER_CH // GATHER_UNROLL, start_group, 0)
    for _ in range(TOP_K):
        pltpu.make_async_copy(h_ref, xs_ref.at[pl.ds(0, GATHER_CH * ROW_SUB)], sem).wait()


def _dispatch(h2r, pos1, pos2):
    steps = T // GATHER_CH
    pos = jnp.concatenate([pos1.reshape(steps, 1, GATHER_CH), pos2.reshape(steps, 1, GATHER_CH)],
                          axis=2)
    return pl.pallas_call(
        _dispatch_kernel,
        grid=(steps,),
        in_specs=[pl.BlockSpec((None, 1, TOP_K * GATHER_CH), lambda i: (i, 0, 0),
                               memory_space=pltpu.SMEM),
                  pl.BlockSpec((GATHER_CH * ROW_SUB, LANES), lambda i: (i, 0)),
                  pl.BlockSpec(memory_space=pl.ANY)],
        out_specs=pl.BlockSpec(memory_space=pl.ANY),
        out_shape=jax.ShapeDtypeStruct((R_MOE * ROW_SUB, LANES), F32),
        input_output_aliases={2: 0},
        scratch_shapes=[pltpu.SemaphoreType.DMA(())],
        compiler_params=pltpu.CompilerParams(
            dimension_semantics=("arbitrary",), vmem_limit_bytes=VMEM_LIMIT),
        name="moe_dispatch",
    )(pos, h2r, jnp.zeros((R_MOE * ROW_SUB, LANES), F32))


def _combine_kernel(pos_ref, pos_next_ref, x1_ref, route_ref, mod_ref, g_ref, ys_ref, o_ref,
                    ybuf_ref, sems):
    i = pl.program_id(0)
    slot = i % 2
    n_rows = TOP_K * TM

    def start_rows(p_ref, s):
        def row_copy(r):
            src_row = pl.multiple_of(p_ref[0, r] * ROW_SUB, ROW_SUB)
            dst_row = pl.multiple_of(r * ROW_SUB, ROW_SUB)
            return pltpu.make_async_copy(ys_ref.at[pl.ds(src_row, ROW_SUB)],
                                         ybuf_ref.at[s, pl.ds(dst_row, ROW_SUB)], sems.at[s])

        def start_group(g, carry):
            for u in range(GATHER_UNROLL):
                row_copy(g * GATHER_UNROLL + u).start(priority=u % 2)
            return carry

        lax.fori_loop(0, n_rows // GATHER_UNROLL, start_group, 0)

    @pl.when(i == 0)
    def _():
        start_rows(pos_ref, 0)

    @pl.when(i + 1 < pl.num_programs(0))
    def _():
        start_rows(pos_next_ref, 1 - slot)

    pltpu.make_async_copy(ys_ref.at[pl.ds(0, n_rows * ROW_SUB)], ybuf_ref.at[slot],
                          sems.at[slot]).wait()
    ybuf = ybuf_ref.at[slot]
    y1 = jnp.concatenate([ybuf[pl.ds(c, TM, stride=ROW_SUB), :] for c in range(ROW_SUB)], axis=1)
    y2 = jnp.concatenate([ybuf[pl.ds(TM * ROW_SUB + c, TM, stride=ROW_SUB), :]
                          for c in range(ROW_SUB)], axis=1)
    f = route_ref[:, 2:3] * y1 + route_ref[:, 3:4] * y2
    o_ref[...] = x1_ref[...] + mod_ref[5:6, :] * _rms(f, g_ref[3:4, :])


def _combine(x1, ys, pos1, pos2, route, mod, norm_g, l):
    pos = jnp.concatenate([pos1.reshape(NT, 1, TM), pos2.reshape(NT, 1, TM)], axis=2)
    row = pl.BlockSpec((TM, D_MODEL), lambda i: (i, 0))
    pos_block = (None, 1, TOP_K * TM)
    return pl.pallas_call(
        _combine_kernel,
        grid=(NT,),
        in_specs=[pl.BlockSpec(pos_block, lambda i: (i, 0, 0), memory_space=pltpu.SMEM),
                  pl.BlockSpec(pos_block, lambda i: (jnp.minimum(i + 1, NT - 1), 0, 0),
                               memory_space=pltpu.SMEM),
                  row,
                  pl.BlockSpec((TM, 128), lambda i: (i, 0)),
                  pl.BlockSpec((None, None, N_MOD, D_MODEL),
                               lambda i: (l, _group_of_tile(i), 0, 0)),
                  pl.BlockSpec((None, 4, D_MODEL), lambda i: (l, 0, 0)),
                  pl.BlockSpec(memory_space=pl.ANY)],
        out_specs=row,
        out_shape=jax.ShapeDtypeStruct((T, D_MODEL), F32),
        scratch_shapes=[pltpu.VMEM((2, TOP_K * TM * ROW_SUB, LANES), F32),
                        pltpu.SemaphoreType.DMA((2,))],
        compiler_params=pltpu.CompilerParams(
            dimension_semantics=("arbitrary",), vmem_limit_bytes=VMEM_LIMIT),
        name="moe_combine",
    )(pos, pos, x1, route, mod, norm_g, ys)


def _route_plan(route):
    e1 = route[:, 0].astype(jnp.int32)
    e2 = route[:, 1].astype(jnp.int32)
    ar = jnp.arange(N_EXPERTS, dtype=jnp.int32)
    oh1 = (e1[:, None] == ar[None, :]).astype(jnp.int32)
    oh2 = (e2[:, None] == ar[None, :]).astype(jnp.int32)
    oh = oh1 + oh2
    csum_incl = jnp.cumsum(oh, axis=0)
    csum = csum_incl - oh
    counts = csum_incl[-1]
    ntile_e = (counts + FF_TILE - 1) // FF_TILE
    tile_end_e = jnp.cumsum(ntile_e)
    tile_start_e = tile_end_e - ntile_e
    gstart = tile_start_e * FF_TILE
    pos1 = jnp.sum((gstart[None, :] + csum) * oh1, axis=1)
    pos2 = jnp.sum((gstart[None, :] + csum) * oh2, axis=1)
    n_used = tile_end_e[-1]
    tiles = jnp.arange(NT_MOE, dtype=jnp.int32)
    eff = jnp.minimum(tiles, n_used - 1)
    te = jnp.sum((eff[:, None] >= tile_end_e[None, :]).astype(jnp.int32), axis=1)
    te = jnp.minimum(te, N_EXPERTS - 1)
    nvalid = jnp.clip(counts[te] - (eff - tile_start_e[te]) * FF_TILE, 0, FF_TILE)
    nvalid = jnp.where(tiles < n_used, nvalid, -1)
    return pos1, pos2, te.astype(jnp.int32), nvalid.astype(jnp.int32), eff.astype(jnp.int32)


def kernel(x_prompt, x_sample, cache_k, cache_v, c, c_ctx, w_ada, b_ada, norm_g, w_in, w_out,
           rpb, g_v, w_s, b_s, w_conv, w_ffn_gu, w_ffn_d, w_router, w_moe_gu, w_moe_d):
    x = jnp.concatenate([x_prompt.reshape(T_P, D_MODEL), x_sample.reshape(T_S, D_MODEL)], axis=0)
    cvec = jnp.concatenate([c_ctx[None], c, jnp.zeros((GROUPS - 1 - DEC_BATCH, D_MODEL), F32)],
                           axis=0)
    mod = _modulation(cvec, w_ada, b_ada)

    ck_all = cache_k.reshape(DEC_BATCH, DEPTH, PAST_LEN, D_ATT)
    cv_all = cache_v.reshape(DEC_BATCH, DEPTH, PAST_LEN, D_ATT)
    bias_all = _na_bias_tables(rpb)
    g_v3 = g_v.reshape(DEPTH, 1, D_CMLP)
    bs_b = jnp.repeat(jnp.swapaxes(b_s, 1, 2), HEAD_DIM, axis=2)
    wc_t = jnp.swapaxes(w_conv, 1, 2)
    w_router_t = jnp.swapaxes(w_router, 1, 2)
    w_moe_gu_all = w_moe_gu.reshape(-1, D_MODEL, 2 * D_FF)
    w_moe_d_all = w_moe_d.reshape(-1, D_FF, D_MODEL)

    dense_te = jnp.zeros((NT_DENSE,), jnp.int32)
    dense_nv = jnp.full((NT_DENSE,), FF_TILE, jnp.int32)
    dense_eff = jnp.arange(NT_DENSE, dtype=jnp.int32)

    new_k = jnp.zeros((BATCH, DEPTH, SEQ, D_ATT), F32)
    new_v = jnp.zeros((BATCH, DEPTH, SEQ, D_ATT), F32)
    for l in range(DEPTH):
        p, new_k, new_v = _inproj(x, mod, norm_g, w_in, new_k, new_v, l)
        o_att = _attention(p, ck_all, cv_all, bias_all, l)
        if l % 2 == 0:
            x1, h2 = _mixer_out(p, o_att, x, mod, norm_g, g_v3, w_s, bs_b, wc_t, w_out,
                                None, l, None)
            x = _ffn(h2, w_ffn_gu, w_ffn_d, dense_te + l // 2, dense_nv, dense_eff, NT_DENSE,
                     (x1, mod, norm_g, l), "ffn_dense")
        else:
            x1, h2, route = _mixer_out(p, o_att, x, mod, norm_g, g_v3, w_s, bs_b, wc_t, w_out,
                                       w_router_t, l, l // 2)
            pos1, pos2, te, nv, eff = _route_plan(route)
            xs = _dispatch(h2, pos1, pos2)
            ys = _ffn(xs, w_moe_gu_all, w_moe_d_all, te + (l // 2) * N_EXPERTS, nv, eff, NT_MOE,
                      None, "ffn_moe")
            x = _combine(x1, ys, pos1, pos2, route, mod, norm_g, l)

    y_prompt = x[:T_P].reshape(BATCH, SEQ, D_MODEL)
    y_sample = x[T_P:].reshape(DEC_BATCH, DEC_SEQ, D_MODEL)
    cache_shape = (BATCH, DEPTH, SEQ, H_ATT, HEAD_DIM)
    return y_prompt, y_sample, new_k.reshape(cache_shape), new_v.reshape(cache_shape)
```

```python
import functools

import numpy as np
import jax
import jax.numpy as jnp
from jax import lax
from jax.experimental import pallas as pl
from jax.experimental.pallas import tpu as pltpu

F32 = jnp.float32
BF16 = jnp.bfloat16

D_MODEL = 1024
BATCH = 16
SEQ = 256
DEPTH = 4
DEC_BATCH = 4
DEC_SEQ = 2048
PAST_LEN = 256
GRID_W = 64
HEAD_DIM = 64
D_ATT = 512
D_CMLP = 256
D_CONV = 256
H_ATT = 8
H_CMLP = 4
CHUNK = 128
NA_KH = 8
NA_KW = 16
D_IN = 2816
D_MIX_IN = D_IN - 3 * D_ATT
D_FF = 2816
N_EXPERTS = 8
TOP_K = 2
N_MOD = 6
EPS = 1e-6
NEG_INF = -1e30

T_P = BATCH * SEQ
T_S = DEC_BATCH * DEC_SEQ
T = T_P + T_S
TM = 256
NT = T // TM
NT_P = T_P // TM
TM_IN = 512
TILES_PER_DEC = DEC_SEQ // TM
GROUPS = 8

ROWS = DEC_SEQ // GRID_W
QROWS = TM // GRID_W
BAND_TILES = 3
BAND = BAND_TILES * TM

FF_TILE = 1024
FF_SUB = 256
FF_CH = 256
N_FF_CH = D_FF // FF_CH
N_FF_STEPS = (N_FF_CH + 1) // 2
R_MOE = 2 * T + N_EXPERTS * FF_TILE
NT_MOE = R_MOE // FF_TILE
NT_DENSE = T // FF_TILE
GATHER_CH = 512
GATHER_UNROLL = 16

VMEM_LIMIT = 56 * 1024 * 1024


def _group_of_tile(i):
    return jnp.where(i < NT_P, 0, 1 + (i - NT_P) // TILES_PER_DEC)


def _rms(x, g):
    return x * lax.rsqrt(jnp.mean(x * x, axis=-1, keepdims=True) + EPS) * g


def _silu(x):
    return x / (1.0 + jnp.exp(-x))


def _gelu_tanh(x):
    c = np.float32(np.sqrt(2.0 / np.pi))
    return 0.5 * x * (1.0 + jnp.tanh(c * (x + np.float32(0.044715) * (x * x * x))))


def _mod_kernel(cv_ref, w_ref, b_ref, o_ref):
    a = _silu(cv_ref[...])
    o_ref[...] = jnp.dot(a, w_ref[...], preferred_element_type=F32,
                         precision=lax.Precision.HIGHEST) + b_ref[...]


def _modulation(cvec, w_ada, b_ada):
    tn = 1536
    nn = (N_MOD * D_MODEL) // tn
    out = pl.pallas_call(
        _mod_kernel,
        grid=(DEPTH, nn),
        in_specs=[
            pl.BlockSpec((GROUPS, D_MODEL), lambda l, n: (0, 0)),
            pl.BlockSpec((None, D_MODEL, tn), lambda l, n: (l, 0, n)),
            pl.BlockSpec((None, 1, tn), lambda l, n: (l, 0, n)),
        ],
        out_specs=pl.BlockSpec((None, GROUPS, tn), lambda l, n: (l, 0, n)),
        out_shape=jax.ShapeDtypeStruct((DEPTH, GROUPS, N_MOD * D_MODEL), F32),
        compiler_params=pltpu.CompilerParams(
            dimension_semantics=("parallel", "parallel"), vmem_limit_bytes=VMEM_LIMIT),
        name="adaln_mod",
    )(cvec, w_ada, b_ada.reshape(DEPTH, 1, N_MOD * D_MODEL))
    return out.reshape(DEPTH, GROUPS, N_MOD, D_MODEL)


def _inproj_kernel(x_ref, mod_ref, g_ref, w_ref, kc_in_ref, vc_in_ref, qkv_ref, pm_ref, kc_ref,
                   vc_ref, wb_ref):
    del kc_in_ref, vc_in_ref

    @pl.when(pl.program_id(0) == 0)
    def _():
        wb_ref[...] = w_ref[...].astype(BF16)

    h = _rms(x_ref[...], g_ref[0:1, :]) * (1.0 + mod_ref[1:2, :]) + mod_ref[0:1, :]
    hb = h.astype(BF16)
    qkv = jnp.dot(hb, wb_ref[:, 0:3 * D_ATT], preferred_element_type=F32)
    pm_ref[...] = jnp.dot(hb, wb_ref[:, 3 * D_ATT:], preferred_element_type=F32)
    qkv_ref[:, 0:D_ATT] = (qkv[:, 0:D_ATT] * np.float32(HEAD_DIM ** -0.5)).astype(BF16)
    qkv_ref[:, D_ATT:] = qkv[:, D_ATT:].astype(BF16)

    @pl.when(pl.program_id(0) < T_P // TM_IN)
    def _():
        for r in range(TM_IN // SEQ):
            rows = slice(r * SEQ, (r + 1) * SEQ)
            kc_ref[r] = qkv[rows, D_ATT:2 * D_ATT]
            vc_ref[r] = qkv[rows, 2 * D_ATT:3 * D_ATT]


def _inproj(x, mod, norm_g, w_in, new_k, new_v, l):
    cache_spec = pl.BlockSpec((TM_IN // SEQ, None, SEQ, D_ATT),
                              lambda i: (jnp.minimum(i, T_P // TM_IN - 1), l, 0, 0))
    return pl.pallas_call(
        _inproj_kernel,
        grid=(T // TM_IN,),
        in_specs=[
            pl.BlockSpec((TM_IN, D_MODEL), lambda i: (i, 0)),
            pl.BlockSpec((None, None, N_MOD, D_MODEL),
                         lambda i: (l, _group_of_tile(i * (TM_IN // TM)), 0, 0)),
            pl.BlockSpec((None, 4, D_MODEL), lambda i: (l, 0, 0)),
            pl.BlockSpec((None, D_MODEL, D_IN), lambda i: (l, 0, 0),
                         pipeline_mode=pl.Buffered(1)),
            pl.BlockSpec(memory_space=pl.ANY),
            pl.BlockSpec(memory_space=pl.ANY),
        ],
        out_specs=[pl.BlockSpec((TM_IN, 3 * D_ATT), lambda i: (i, 0)),
                   pl.BlockSpec((TM_IN, D_MIX_IN), lambda i: (i, 0)), cache_spec, cache_spec],
        out_shape=[jax.ShapeDtypeStruct((T, 3 * D_ATT), BF16),
                   jax.ShapeDtypeStruct((T, D_MIX_IN), F32),
                   jax.ShapeDtypeStruct(new_k.shape, F32),
                   jax.ShapeDtypeStruct(new_v.shape, F32)],
        input_output_aliases={4: 2, 5: 3},
        scratch_shapes=[pltpu.VMEM((D_MODEL, D_IN), BF16)],
        compiler_params=pltpu.CompilerParams(
            dimension_semantics=("arbitrary",), vmem_limit_bytes=VMEM_LIMIT),
        name="inproj",
    )(x, mod, norm_g, w_in, new_k, new_v)


def _dot_nt(a, b):
    return lax.dot_general(a, b, (((1,), (1,)), ((), ())), preferred_element_type=F32)


def _attn_kernel(q_ref, k0_ref, k1_ref, k2_ref, v0_ref, v1_ref, v2_ref,
                 ck_ref, cv_ref, bias_ref, o_ref):
    is_prompt = pl.program_id(0) < NT_P

    @pl.when(is_prompt)
    def _():
        _ctx_attn_body(q_ref, k0_ref, v0_ref, o_ref)

    @pl.when(jnp.logical_not(is_prompt))
    def _():
        _na_attn_body(q_ref, k0_ref, k1_ref, k2_ref, v0_ref, v1_ref, v2_ref,
                      ck_ref, cv_ref, bias_ref, o_ref)


def _ctx_attn_body(q_ref, k_ref, v_ref, o_ref):
    for h in range(H_ATT):
        sl = slice(h * HEAD_DIM, (h + 1) * HEAD_DIM)
        s = _dot_nt(q_ref[:, sl], k_ref[:, sl])
        m = jnp.max(s, axis=-1, keepdims=True)
        e = jnp.exp(s - m)
        den = jnp.sum(e, axis=-1, keepdims=True)
        o = jnp.dot(e.astype(BF16), v_ref[:, sl], preferred_element_type=F32) / den
        o_ref[:, sl] = o.astype(BF16)


def _na_attn_body(q_ref, k0_ref, k1_ref, k2_ref, v0_ref, v1_ref, v2_ref,
                  ck_ref, cv_ref, bias_ref, o_ref):
    k_refs = (k0_ref, k1_ref, k2_ref)
    v_refs = (v0_ref, v1_ref, v2_ref)
    for h in range(H_ATT):
        sl = slice(h * HEAD_DIM, (h + 1) * HEAD_DIM)
        q = q_ref[:, sl]
        s_loc = [_dot_nt(q, k_refs[j][:, sl]) + bias_ref[h, :, j * TM:(j + 1) * TM]
                 for j in range(BAND_TILES)]
        s_ctx = _dot_nt(q, ck_ref[:, sl])
        m = jnp.max(jnp.maximum(jnp.maximum(s_loc[0], s_loc[1]), jnp.maximum(s_loc[2], s_ctx)),
                    axis=-1, keepdims=True)
        es = [jnp.exp(s - m) for s in s_loc + [s_ctx]]
        den = jnp.sum((es[0] + es[1]) + (es[2] + es[3]), axis=-1, keepdims=True)
        e_all = jnp.concatenate([e.astype(BF16) for e in es], axis=1)
        v_all = jnp.concatenate([v_ref[:, sl] for v_ref in v_refs] + [cv_ref[:, sl]], axis=0)
        acc = jnp.dot(e_all, v_all, preferred_element_type=F32)
        o_ref[:, sl] = (acc / den).astype(BF16)


def _na_variant_tables():
    kh = min(NA_KH, ROWS)
    per_tile = []
    for rb in range(ROWS // QROWS):
        r0 = rb * QROWS
        bs = int(np.clip(rb - 1, 0, ROWS // QROWS - BAND_TILES)) * QROWS
        tab = -np.ones((QROWS, BAND_TILES * QROWS), np.int32)
        for qr in range(QROWS):
            r = r0 + qr
            rs = int(np.clip(r - kh // 2, 0, ROWS - kh))
            for kr in range(BAND_TILES * QROWS):
                ka = bs + kr
                if rs <= ka < rs + kh:
                    tab[qr, kr] = ka - r + (NA_KH - 1)
            assert (tab[qr] >= 0).sum() == kh
        per_tile.append(tab)
    variants, variant_of_tile = [], []
    for tab in per_tile:
        for vi, v in enumerate(variants):
            if np.array_equal(v, tab):
                variant_of_tile.append(vi)
                break
        else:
            variants.append(tab)
            variant_of_tile.append(len(variants) - 1)
    return np.asarray(variant_of_tile, np.int32), np.stack(variants)


_NA_VARIANT_OF_TILE, _NA_DR_IDX = _na_variant_tables()
_NA_NVAR = _NA_DR_IDX.shape[0]
_N_DR = 2 * NA_KH - 1
_N_DC = 2 * NA_KW - 1


def _bias_kernel(w_ref, o_ref):
    qc = lax.broadcasted_iota(jnp.int32, (GRID_W, LANES), 0)
    lane = lax.broadcasted_iota(jnp.int32, (GRID_W, LANES), 1)
    kc = lane % GRID_W
    cs = jnp.clip(qc - NA_KW // 2, 0, GRID_W - NA_KW)
    col_ok = jnp.logical_and(kc >= cs, kc < cs + NA_KW)
    left = lane < GRID_W
    neg = jnp.full((GRID_W, LANES), NEG_INF, F32)
    cache = {}
    for vi in range(_NA_NVAR):
        for qr in range(QROWS):
            for pp in range(BAND_TILES * QROWS // 2):
                d0 = int(_NA_DR_IDX[vi, qr, 2 * pp])
                d1 = int(_NA_DR_IDX[vi, qr, 2 * pp + 1])
                if (d0, d1) not in cache:
                    if d0 < 0 and d1 < 0:
                        tile = neg
                    else:
                        u = (w_ref[pl.ds(d0 if d0 >= 0 else _N_DR, 1), :]
                             + w_ref[pl.ds(_N_DR + 1 + (d1 if d1 >= 0 else _N_DR), 1), :])
                        t = pltpu.roll(jnp.broadcast_to(u, (GRID_W, LANES)), 0, 1,
                                       stride=1, stride_axis=0)
                        ok = col_ok
                        if d0 < 0:
                            ok = jnp.logical_and(ok, jnp.logical_not(left))
                        if d1 < 0:
                            ok = jnp.logical_and(ok, left)
                        tile = jnp.where(ok, t, neg)
                    cache[(d0, d1)] = tile
                o_ref[vi, qr * GRID_W:(qr + 1) * GRID_W, pp * LANES:(pp + 1) * LANES] = cache[(d0, d1)]


def _na_bias_tables(rpb):
    half = NA_KW - 1
    zeros = jnp.zeros((DEPTH, H_ATT, _N_DR, LANES - _N_DC), F32)
    w_lo = jnp.concatenate([rpb[..., half:], zeros, rpb[..., :half]], axis=-1)
    w_lo = jnp.pad(w_lo, ((0, 0), (0, 0), (0, 1), (0, 0)))
    w_hi = jnp.roll(w_lo, GRID_W, axis=-1)
    w = jnp.concatenate([w_lo, w_hi], axis=2)
    return pl.pallas_call(
        _bias_kernel,
        grid=(DEPTH, H_ATT),
        in_specs=[pl.BlockSpec((None, None, 2 * (_N_DR + 1), LANES), lambda l, h: (l, h, 0, 0))],
        out_specs=pl.BlockSpec((None, _NA_NVAR, None, TM, BAND), lambda l, h: (l, 0, h, 0, 0)),
        out_shape=jax.ShapeDtypeStruct((DEPTH, _NA_NVAR, H_ATT, TM, BAND), F32),
        compiler_params=pltpu.CompilerParams(
            dimension_semantics=("parallel", "parallel"), vmem_limit_bytes=VMEM_LIMIT),
        name="na_bias",
    )(w)


def _attention(p, ck_all, cv_all, bias_all, l):
    nrb = TILES_PER_DEC
    var_of_tile = [int(v) for v in _NA_VARIANT_OF_TILE]

    def dec_batch(i):
        return jnp.maximum(i - NT_P, 0) // nrb

    def band_tile(i, j):
        rb = (i - NT_P) % nrb
        first = NT_P + dec_batch(i) * nrb + jnp.clip(rb - 1, 0, nrb - BAND_TILES)
        return jnp.where(i < NT_P, i, first + j)

    def variant(i):
        rb = jnp.maximum(i - NT_P, 0) % nrb
        v = jnp.int32(var_of_tile[0])
        for t in range(1, nrb):
            v = jnp.where(rb >= t, jnp.int32(var_of_tile[t]), v)
        return v

    kv_specs = [pl.BlockSpec((TM, D_ATT), functools.partial(
        lambda i, j, col: (band_tile(i, j), col), j=j, col=col))
        for col in (1, 2) for j in range(BAND_TILES)]
    return pl.pallas_call(
        _attn_kernel,
        grid=(NT,),
        in_specs=[pl.BlockSpec((TM, D_ATT), lambda i: (i, 0))] + kv_specs + [
            pl.BlockSpec((None, None, PAST_LEN, D_ATT), lambda i: (dec_batch(i), l, 0, 0)),
            pl.BlockSpec((None, None, PAST_LEN, D_ATT), lambda i: (dec_batch(i), l, 0, 0)),
            pl.BlockSpec((None, None, H_ATT, TM, BAND), lambda i: (l, variant(i), 0, 0, 0)),
        ],
        out_specs=pl.BlockSpec((TM, D_ATT), lambda i: (i, 0)),
        out_shape=jax.ShapeDtypeStruct((T, D_ATT), BF16),
        compiler_params=pltpu.CompilerParams(
            dimension_semantics=("arbitrary",), vmem_limit_bytes=VMEM_LIMIT),
        name="attn",
    )(p, p, p, p, p, p, p, ck_all, cv_all, bias_all)


def _mixer_out_kernel(with_router, *refs):
    if with_router:
        (u_ref, vm_ref, bg_ref, cg_ref, hx_ref, cgp_ref, hxp_ref, cgn_ref, hxn_ref,
         oatt_ref, x_ref, mod_ref, g_ref, gv_ref, ws_ref, bs_ref, wc_ref, wo_ref, wr_ref,
         x1_ref, h2_ref, route_ref, wob_ref) = refs
    else:
        (u_ref, vm_ref, bg_ref, cg_ref, hx_ref, cgp_ref, hxp_ref, cgn_ref, hxn_ref,
         oatt_ref, x_ref, mod_ref, g_ref, gv_ref, ws_ref, bs_ref, wc_ref, wo_ref,
         x1_ref, h2_ref, wob_ref) = refs
    i = pl.program_id(0)

    @pl.when(i == 0)
    def _():
        wob_ref[...] = wo_ref[...].astype(BF16)

    u = _gelu_tanh(u_ref[...])
    vm = _rms(_gelu_tanh(vm_ref[...]), gv_ref[...]).astype(BF16)
    chunks = []
    for c in range(TM // CHUNK):
        rows = slice(c * CHUNK, (c + 1) * CHUNK)
        heads = [jnp.dot(ws_ref[h].astype(BF16), vm[rows, h * HEAD_DIM:(h + 1) * HEAD_DIM],
                         preferred_element_type=F32) for h in range(H_CMLP)]
        chunks.append(jnp.concatenate(heads, axis=1) + bs_ref[...])
    o_mlp = u * jnp.concatenate(chunks, axis=0)

    j = (i - NT_P) % TILES_PER_DEC
    has_prev = jnp.logical_and(i >= NT_P, j > 0)
    has_next = jnp.logical_and(i >= NT_P, j < TILES_PER_DEC - 1)
    z = cg_ref[...] * hx_ref[...]
    z_halo_prev = jnp.where(has_prev, cgp_ref[7:8, :] * hxp_ref[7:8, :], 0.0)
    z_halo_next = jnp.where(has_next, cgn_ref[0:1, :] * hxn_ref[0:1, :], 0.0)
    row = lax.broadcasted_iota(jnp.int32, (TM, D_CONV), 0)
    z_prev = jnp.where(row == 0, z_halo_prev, pltpu.roll(z, 1, 0))
    z_next = jnp.where(row == TM - 1, z_halo_next, pltpu.roll(z, TM - 1, 0))
    y_conv = z_prev * wc_ref[0:1, :] + z * wc_ref[1:2, :] + z_next * wc_ref[2:3, :]
    o_conv = bg_ref[...] * y_conv

    y = jnp.dot(oatt_ref[...], wob_ref[0:D_ATT, :], preferred_element_type=F32)
    y = y + jnp.dot(o_mlp.astype(BF16), wob_ref[D_ATT:D_ATT + D_CMLP, :],
                    preferred_element_type=F32)
    y = y + jnp.dot(o_conv.astype(BF16), wob_ref[D_ATT + D_CMLP:, :],
                    preferred_element_type=F32)

    x1 = x_ref[...] + mod_ref[2:3, :] * _rms(y, g_ref[1:2, :])
    h2 = _rms(x1, g_ref[2:3, :]) * (1.0 + mod_ref[4:5, :]) + mod_ref[3:4, :]
    x1_ref[...] = x1
    if with_router:
        _store_row_tiled(h2_ref, h2, TM)
    else:
        h2_ref[...] = h2

    if with_router:
        lg = [jnp.sum(h2 * wr_ref[e:e + 1, :], axis=-1, keepdims=True) for e in range(N_EXPERTS)]

        def top1(cols):
            m = cols[0]
            for col in cols[1:]:
                m = jnp.maximum(m, col)
            idx = jnp.full_like(m, N_EXPERTS - 1)
            for e in range(N_EXPERTS - 2, -1, -1):
                idx = jnp.where(cols[e] == m, np.float32(e), idx)
            return m, idx

        m1, i1 = top1(lg)
        m2, i2 = top1([jnp.where(i1 == np.float32(e), -jnp.inf, lg[e]) for e in range(N_EXPERTS)])
        lane = lax.broadcasted_iota(jnp.int32, (TM, 128), 1)
        e2 = jnp.exp(m2 - m1)
        den = 1.0 + e2
        gate1 = 1.0 / den
        gate2 = e2 / den
        route = jnp.where(lane == 0, i1,
                          jnp.where(lane == 1, i2,
                                    jnp.where(lane == 2, gate1,
                                              jnp.where(lane == 3, gate2, 0.0))))
        route_ref[...] = route


def _mixer_out(p, o_att, x, mod, norm_g, g_v, w_s, bs_b, wc_t, w_out, w_router_t, l, moe_idx):
    with_router = moe_idx is not None
    hb = TM // 8

    def col(cb):
        return pl.BlockSpec((TM, D_CMLP), lambda i: (i, cb))

    def halo_prev(cb):
        return pl.BlockSpec((8, D_CONV), lambda i: (jnp.maximum(i * hb - 1, 0), cb))

    def halo_next(cb):
        return pl.BlockSpec((8, D_CONV), lambda i: (jnp.minimum((i + 1) * hb, T // 8 - 1), cb))

    in_specs = [col(0), col(1), col(2), col(3), col(4),
                halo_prev(3), halo_prev(4), halo_next(3), halo_next(4),
                pl.BlockSpec((TM, D_ATT), lambda i: (i, 0)),
                pl.BlockSpec((TM, D_MODEL), lambda i: (i, 0)),
                pl.BlockSpec((None, None, N_MOD, D_MODEL), lambda i: (l, _group_of_tile(i), 0, 0)),
                pl.BlockSpec((None, 4, D_MODEL), lambda i: (l, 0, 0)),
                pl.BlockSpec((None, 1, D_CMLP), lambda i: (l, 0, 0)),
                pl.BlockSpec((None, H_CMLP, CHUNK, CHUNK), lambda i: (l, 0, 0, 0)),
                pl.BlockSpec((None, CHUNK, D_CMLP), lambda i: (l, 0, 0)),
                pl.BlockSpec((None, 3, D_CONV), lambda i: (l, 0, 0)),
                pl.BlockSpec((None, D_MODEL, D_MODEL), lambda i: (l, 0, 0),
                             pipeline_mode=pl.Buffered(1))]
    args = [p, p, p, p, p, p, p, p, p, o_att, x, mod, norm_g, g_v, w_s, bs_b, wc_t, w_out]
    out_specs = [pl.BlockSpec((TM, D_MODEL), lambda i: (i, 0)),
                 pl.BlockSpec((TM, D_MODEL), lambda i: (i, 0))]
    out_shape = [jax.ShapeDtypeStruct((T, D_MODEL), F32), jax.ShapeDtypeStruct((T, D_MODEL), F32)]
    if with_router:
        in_specs.append(pl.BlockSpec((None, N_EXPERTS, D_MODEL), lambda i: (moe_idx, 0, 0)))
        args.append(w_router_t)
        out_specs[1] = pl.BlockSpec((TM * ROW_SUB, LANES), lambda i: (i, 0))
        out_shape[1] = jax.ShapeDtypeStruct((T * ROW_SUB, LANES), F32)
        out_specs.append(pl.BlockSpec((TM, 128), lambda i: (i, 0)))
        out_shape.append(jax.ShapeDtypeStruct((T, 128), F32))
    return pl.pallas_call(
        functools.partial(_mixer_out_kernel, with_router),
        grid=(NT,),
        in_specs=in_specs,
        out_specs=out_specs,
        out_shape=out_shape,
        scratch_shapes=[pltpu.VMEM((D_MODEL, D_MODEL), BF16)],
        compiler_params=pltpu.CompilerParams(
            dimension_semantics=("arbitrary",), vmem_limit_bytes=VMEM_LIMIT),
        name="mixer_out_router" if with_router else "mixer_out",
    )(*args)


LANES = 128
ROW_SUB = D_MODEL // LANES


def _load_row_tiled(ref, n):
    return jnp.concatenate([ref[pl.ds(c, n, stride=ROW_SUB), :] for c in range(ROW_SUB)], axis=1)


def _store_row_tiled(ref, val, n):
    for c in range(ROW_SUB):
        ref[pl.ds(c, n, stride=ROW_SUB), :] = val[:, c * LANES:(c + 1) * LANES]


def _ffn_kernel(row_tiled, te_ref, nv_ref, et_ref, x_ref, wga_ref, wua_ref, wda_ref,
                wgb_ref, wub_ref, wdb_ref, *rest):
    if row_tiled:
        o_ref, xb_ref, acc_ref = rest
    else:
        x1_ref, mod_ref, g_ref, o_ref, xb_ref, acc_ref = rest
    del te_ref, et_ref
    t = pl.program_id(0)
    j = pl.program_id(1)
    nvalid = nv_ref[t]
    nsub = (nvalid + (FF_SUB - 1)) // FF_SUB
    last = j == N_FF_STEPS - 1

    @pl.when(j == 0)
    def _():
        acc_ref[...] = jnp.zeros((FF_TILE, D_MODEL), F32)

    @pl.when(jnp.logical_and(j == 0, nvalid >= 0))
    def _():
        if row_tiled:
            xb_ref[...] = _load_row_tiled(x_ref, FF_TILE).astype(BF16)
        else:
            xb_ref[...] = x_ref[...].astype(BF16)

    def swiglu_rows(n, chunks):
        xs = xb_ref[0:n, :]
        acts, wds = [], []
        for wg_ref, wu_ref, wd_ref in chunks:
            g = jnp.dot(xs, wg_ref[...].astype(BF16), preferred_element_type=F32)
            u = jnp.dot(xs, wu_ref[...].astype(BF16), preferred_element_type=F32)
            acts.append((_silu(g) * u).astype(BF16))
            wds.append(wd_ref[...].astype(BF16))
        a = acts[0] if len(acts) == 1 else jnp.concatenate(acts, axis=1)
        wd = wds[0] if len(wds) == 1 else jnp.concatenate(wds, axis=0)
        acc_ref[0:n, :] += jnp.dot(a, wd, preferred_element_type=F32)

    chunk_a = (wga_ref, wua_ref, wda_ref)
    chunk_b = (wgb_ref, wub_ref, wdb_ref)
    n_options = range(1, FF_TILE // FF_SUB + 1) if row_tiled else (FF_TILE // FF_SUB,)
    for k in n_options:
        rows_ok = (nsub == k) if row_tiled else True
        pl.when(jnp.logical_and(rows_ok, jnp.logical_not(last)))(
            functools.partial(swiglu_rows, k * FF_SUB, (chunk_a, chunk_b)))
        pl.when(jnp.logical_and(rows_ok, last))(
            functools.partial(swiglu_rows, k * FF_SUB, (chunk_a,)))

    @pl.when(last)
    def _():
        if row_tiled:
            _store_row_tiled(o_ref, acc_ref[...], FF_TILE)
        else:
            o_ref[...] = x1_ref[...] + mod_ref[5:6, :] * _rms(acc_ref[...], g_ref[3:4, :])


def _ffn(x_rows, w_gu, w_d, tile_expert, tile_nvalid, tile_eff, n_tiles, residual, name):
    def chunk(t, j, nv, second):
        c = jnp.minimum(2 * j + second, N_FF_CH - 1)
        return jnp.where(nv[t] >= 0, c, N_FF_CH - 1)

    def weight_specs(second):
        return [
            pl.BlockSpec((None, D_MODEL, FF_CH),
                         lambda t, j, te, nv, et: (te[t], 0, chunk(t, j, nv, second))),
            pl.BlockSpec((None, D_MODEL, FF_CH),
                         lambda t, j, te, nv, et: (te[t], 0, N_FF_CH + chunk(t, j, nv, second))),
            pl.BlockSpec((None, FF_CH, D_MODEL),
                         lambda t, j, te, nv, et: (te[t], chunk(t, j, nv, second), 0)),
        ]

    row_tiled = residual is None
    if row_tiled:
        x_block = (FF_TILE * ROW_SUB, LANES)
        out_shape = jax.ShapeDtypeStruct((n_tiles * FF_TILE * ROW_SUB, LANES), F32)
        extra_specs, extra_args = [], ()
    else:
        x1, mod, norm_g, l = residual
        x_block = (FF_TILE, D_MODEL)
        out_shape = jax.ShapeDtypeStruct((n_tiles * FF_TILE, D_MODEL), F32)
        tiles_per_tm = FF_TILE // TM
        extra_specs = [
            pl.BlockSpec(x_block, lambda t, j, te, nv, et: (t, 0)),
            pl.BlockSpec((None, None, N_MOD, D_MODEL),
                         lambda t, j, te, nv, et: (l, _group_of_tile(t * tiles_per_tm), 0, 0)),
            pl.BlockSpec((None, 4, D_MODEL), lambda t, j, te, nv, et: (l, 0, 0)),
        ]
        extra_args = (x1, mod, norm_g)
    grid_spec = pltpu.PrefetchScalarGridSpec(
        num_scalar_prefetch=3,
        grid=(n_tiles, N_FF_STEPS),
        in_specs=[pl.BlockSpec(x_block, lambda t, j, te, nv, et: (et[t], 0))]
        + weight_specs(0) + weight_specs(1) + extra_specs,
        out_specs=pl.BlockSpec(x_block, lambda t, j, te, nv, et: (t, 0)),
        scratch_shapes=[pltpu.VMEM((FF_TILE, D_MODEL), BF16),
                        pltpu.VMEM((FF_TILE, D_MODEL), F32)],
    )
    return pl.pallas_call(
        functools.partial(_ffn_kernel, row_tiled),
        grid_spec=grid_spec,
        out_shape=out_shape,
        compiler_params=pltpu.CompilerParams(
            dimension_semantics=("arbitrary", "arbitrary"), vmem_limit_bytes=VMEM_LIMIT),
        name=name,
    )(tile_expert, tile_nvalid, tile_eff, x_rows, w_gu, w_gu, w_d, w_gu, w_gu, w_d, *extra_args)


def _dispatch_kernel(pos_ref, h_ref, xs_in_ref, xs_ref, sem):
    del xs_in_ref

    def row_copy(r, k):
        dst_row = pl.multiple_of(pos_ref[0, k * GATHER_CH + r] * ROW_SUB, ROW_SUB)
        src_row = pl.multiple_of(r * ROW_SUB, ROW_SUB)
        return pltpu.make_async_copy(h_ref.at[pl.ds(src_row, ROW_SUB)],
                                     xs_ref.at[pl.ds(dst_row, ROW_SUB)], sem)

    def start_group(g, carry):
        for u in range(GATHER_UNROLL):
            r = g * GATHER_UNROLL + u
            row_copy(r, 0).start(priority=0)
            row_copy(r, 1).start(priority=1)
        return carry

    lax.fori_loop(0, GATHER_CH // GATHER_UNROLL, start_group, 0)
    for _ in range(TOP_K):
        pltpu.make_async_copy(h_ref, xs_ref.at[pl.ds(0, GATHER_CH * ROW_SUB)], sem).wait()


def _dispatch(h2r, pos1, pos2):
    steps = T // GATHER_CH
    pos = jnp.concatenate([pos1.reshape(steps, 1, GATHER_CH), pos2.reshape(steps, 1, GATHER_CH)],
                          axis=2)
    return pl.pallas_call(
        _dispatch_kernel,
        grid=(steps,),
        in_specs=[pl.BlockSpec((None, 1, TOP_K * GATHER_CH), lambda i: (i, 0, 0),
                               memory_space=pltpu.SMEM),
                  pl.BlockSpec((GATHER_CH * ROW_SUB, LANES), lambda i: (i, 0)),
                  pl.BlockSpec(memory_space=pl.ANY)],
        out_specs=pl.BlockSpec(memory_space=pl.ANY),
        out_shape=jax.ShapeDtypeStruct((R_MOE * ROW_SUB, LANES), F32),
        input_output_aliases={2: 0},
        scratch_shapes=[pltpu.SemaphoreType.DMA(())],
        compiler_params=pltpu.CompilerParams(
            dimension_semantics=("arbitrary",), vmem_limit_bytes=VMEM_LIMIT),
        name="moe_dispatch",
    )(pos, h2r, jnp.zeros((R_MOE * ROW_SUB, LANES), F32))


def _combine_kernel(pos_ref, pos_next_ref, x1_ref, route_ref, mod_ref, g_ref, ys_ref, o_ref,
                    ybuf_ref, sems):
    i = pl.program_id(0)
    slot = i % 2
    n_rows = TOP_K * TM

    def start_rows(p_ref, s):
        def row_copy(r):
            src_row = pl.multiple_of(p_ref[0, r] * ROW_SUB, ROW_SUB)
            dst_row = pl.multiple_of(r * ROW_SUB, ROW_SUB)
            return pltpu.make_async_copy(ys_ref.at[pl.ds(src_row, ROW_SUB)],
                                         ybuf_ref.at[s, pl.ds(dst_row, ROW_SUB)], sems.at[s])

        def start_group(g, carry):
            for u in range(GATHER_UNROLL):
                row_copy(g * GATHER_UNROLL + u).start(priority=u % 2)
            return carry

        lax.fori_loop(0, n_rows // GATHER_UNROLL, start_group, 0)

    @pl.when(i == 0)
    def _():
        start_rows(pos_ref, 0)

    @pl.when(i + 1 < pl.num_programs(0))
    def _():
        start_rows(pos_next_ref, 1 - slot)

    pltpu.make_async_copy(ys_ref.at[pl.ds(0, n_rows * ROW_SUB)], ybuf_ref.at[slot],
                          sems.at[slot]).wait()
    ybuf = ybuf_ref.at[slot]
    y1 = jnp.concatenate([ybuf[pl.ds(c, TM, stride=ROW_SUB), :] for c in range(ROW_SUB)], axis=1)
    y2 = jnp.concatenate([ybuf[pl.ds(TM * ROW_SUB + c, TM, stride=ROW_SUB), :]
                          for c in range(ROW_SUB)], axis=1)
    f = route_ref[:, 2:3] * y1 + route_ref[:, 3:4] * y2
    o_ref[...] = x1_ref[...] + mod_ref[5:6, :] * _rms(f, g_ref[3:4, :])


def _combine(x1, ys, pos1, pos2, route, mod, norm_g, l):
    pos = jnp.concatenate([pos1.reshape(NT, 1, TM), pos2.reshape(NT, 1, TM)], axis=2)
    row = pl.BlockSpec((TM, D_MODEL), lambda i: (i, 0))
    pos_block = (None, 1, TOP_K * TM)
    return pl.pallas_call(
        _combine_kernel,
        grid=(NT,),
        in_specs=[pl.BlockSpec(pos_block, lambda i: (i, 0, 0), memory_space=pltpu.SMEM),
                  pl.BlockSpec(pos_block, lambda i: (jnp.minimum(i + 1, NT - 1), 0, 0),
                               memory_space=pltpu.SMEM),
                  row,
                  pl.BlockSpec((TM, 128), lambda i: (i, 0)),
                  pl.BlockSpec((None, None, N_MOD, D_MODEL),
                               lambda i: (l, _group_of_tile(i), 0, 0)),
                  pl.BlockSpec((None, 4, D_MODEL), lambda i: (l, 0, 0)),
                  pl.BlockSpec(memory_space=pl.ANY)],
        out_specs=row,
        out_shape=jax.ShapeDtypeStruct((T, D_MODEL), F32),
        scratch_shapes=[pltpu.VMEM((2, TOP_K * TM * ROW_SUB, LANES), F32),
                        pltpu.SemaphoreType.DMA((2,))],
        compiler_params=pltpu.CompilerParams(
            dimension_semantics=("arbitrary",), vmem_limit_bytes=VMEM_LIMIT),
        name="moe_combine",
    )(pos, pos, x1, route, mod, norm_g, ys)


def _route_plan(route):
    e1 = route[:, 0].astype(jnp.int32)
    e2 = route[:, 1].astype(jnp.int32)
    ar = jnp.arange(N_EXPERTS, dtype=jnp.int32)
    oh1 = (e1[:, None] == ar[None, :]).astype(jnp.int32)
    oh2 = (e2[:, None] == ar[None, :]).astype(jnp.int32)
    oh = oh1 + oh2
    csum_incl = jnp.cumsum(oh, axis=0)
    csum = csum_incl - oh
    counts = csum_incl[-1]
    ntile_e = (counts + FF_TILE - 1) // FF_TILE
    tile_end_e = jnp.cumsum(ntile_e)
    tile_start_e = tile_end_e - ntile_e
    gstart = tile_start_e * FF_TILE
    pos1 = jnp.sum((gstart[None, :] + csum) * oh1, axis=1)
    pos2 = jnp.sum((gstart[None, :] + csum) * oh2, axis=1)
    n_used = tile_end_e[-1]
    tiles = jnp.arange(NT_MOE, dtype=jnp.int32)
    eff = jnp.minimum(tiles, n_used - 1)
    te = jnp.sum((eff[:, None] >= tile_end_e[None, :]).astype(jnp.int32), axis=1)
    te = jnp.minimum(te, N_EXPERTS - 1)
    nvalid = jnp.clip(counts[te] - (eff - tile_start_e[te]) * FF_TILE, 0, FF_TILE)
    nvalid = jnp.where(tiles < n_used, nvalid, -1)
    return pos1, pos2, te.astype(jnp.int32), nvalid.astype(jnp.int32), eff.astype(jnp.int32)


def kernel(x_prompt, x_sample, cache_k, cache_v, c, c_ctx, w_ada, b_ada, norm_g, w_in, w_out,
           rpb, g_v, w_s, b_s, w_conv, w_ffn_gu, w_ffn_d, w_router, w_moe_gu, w_moe_d):
    x = jnp.concatenate([x_prompt.reshape(T_P, D_MODEL), x_sample.reshape(T_S, D_MODEL)], axis=0)
    cvec = jnp.concatenate([c_ctx[None], c, jnp.zeros((GROUPS - 1 - DEC_BATCH, D_MODEL), F32)],
                           axis=0)
    mod = _modulation(cvec, w_ada, b_ada)

    ck_all = cache_k.reshape(DEC_BATCH, DEPTH, PAST_LEN, D_ATT).astype(BF16)
    cv_all = cache_v.reshape(DEC_BATCH, DEPTH, PAST_LEN, D_ATT).astype(BF16)
    bias_all = _na_bias_tables(rpb)
    g_v3 = g_v.reshape(DEPTH, 1, D_CMLP)
    bs_b = jnp.repeat(jnp.swapaxes(b_s, 1, 2), HEAD_DIM, axis=2)
    wc_t = jnp.swapaxes(w_conv, 1, 2)
    w_router_t = jnp.swapaxes(w_router, 1, 2)
    w_moe_gu_all = w_moe_gu.reshape(-1, D_MODEL, 2 * D_FF)
    w_moe_d_all = w_moe_d.reshape(-1, D_FF, D_MODEL)

    dense_te = jnp.zeros((NT_DENSE,), jnp.int32)
    dense_nv = jnp.full((NT_DENSE,), FF_TILE, jnp.int32)
    dense_eff = jnp.arange(NT_DENSE, dtype=jnp.int32)

    new_k = jnp.zeros((BATCH, DEPTH, SEQ, D_ATT), F32)
    new_v = jnp.zeros((BATCH, DEPTH, SEQ, D_ATT), F32)
    for l in range(DEPTH):
        qkv, p, new_k, new_v = _inproj(x, mod, norm_g, w_in, new_k, new_v, l)
        o_att = _attention(qkv, ck_all, cv_all, bias_all, l)
        if l % 2 == 0:
            x1, h2 = _mixer_out(p, o_att, x, mod, norm_g, g_v3, w_s, bs_b, wc_t, w_out,
                                None, l, None)
            x = _ffn(h2, w_ffn_gu, w_ffn_d, dense_te + l // 2, dense_nv, dense_eff, NT_DENSE,
                     (x1, mod, norm_g, l), "ffn_dense")
        else:
            x1, h2, route = _mixer_out(p, o_att, x, mod, norm_g, g_v3, w_s, bs_b, wc_t, w_out,
                                       w_router_t, l, l // 2)
            pos1, pos2, te, nv, eff = _route_plan(route)
            xs = _dispatch(h2, pos1, pos2)
            ys = _ffn(xs, w_moe_gu_all, w_moe_d_all, te + (l // 2) * N_EXPERTS, nv, eff, NT_MOE,
                      None, "ffn_moe")
            x = _combine(x1, ys, pos1, pos2, route, mod, norm_g, l)

    y_prompt = x[:T_P].reshape(BATCH, SEQ, D_MODEL)
    y_sample = x[T_P:].reshape(DEC_BATCH, DEC_SEQ, D_MODEL)
    cache_shape = (BATCH, DEPTH, SEQ, H_ATT, HEAD_DIM)
    return y_prompt, y_sample, new_k.reshape(cache_shape), new_v.reshape(cache_shape)
```

```python
import functools

import numpy as np
import jax
import jax.numpy as jnp
from jax import lax
from jax.experimental import pallas as pl
from jax.experimental.pallas import tpu as pltpu

F32 = jnp.float32
BF16 = jnp.bfloat16

D_MODEL = 1024
BATCH = 16
SEQ = 256
DEPTH = 4
DEC_BATCH = 4
DEC_SEQ = 2048
PAST_LEN = 256
GRID_W = 64
HEAD_DIM = 64
D_ATT = 512
D_CMLP = 256
D_CONV = 256
H_ATT = 8
H_CMLP = 4
CHUNK = 128
NA_KH = 8
NA_KW = 16
D_IN = 2816
D_MIX_IN = D_IN - 3 * D_ATT
D_FF = 2816
N_EXPERTS = 8
TOP_K = 2
N_MOD = 6
EPS = 1e-6
NEG_INF = -1e30

T_P = BATCH * SEQ
T_S = DEC_BATCH * DEC_SEQ
T = T_P + T_S
TM = 256
NT = T // TM
NT_P = T_P // TM
TM_IN = 512
TILES_PER_DEC = DEC_SEQ // TM
GROUPS = 8

ROWS = DEC_SEQ // GRID_W
QROWS = TM // GRID_W
BAND_TILES = 3
BAND = BAND_TILES * TM

FF_TILE = 1024
FF_SUB = 256
FF_CH = 256
N_FF_CH = D_FF // FF_CH
N_FF_STEPS = (N_FF_CH + 1) // 2
R_MOE = 2 * T + N_EXPERTS * FF_TILE
NT_MOE = R_MOE // FF_TILE
NT_DENSE = T // FF_TILE
GATHER_CH = 512
GATHER_UNROLL = 16

VMEM_LIMIT = 56 * 1024 * 1024


def _group_of_tile(i):
    return jnp.where(i < NT_P, 0, 1 + (i - NT_P) // TILES_PER_DEC)


def _rms(x, g):
    return x * lax.rsqrt(jnp.mean(x * x, axis=-1, keepdims=True) + EPS) * g


def _silu(x):
    return x / (1.0 + jnp.exp(-x))


def _gelu_tanh(x):
    c = np.float32(np.sqrt(2.0 / np.pi))
    return 0.5 * x * (1.0 + jnp.tanh(c * (x + np.float32(0.044715) * (x * x * x))))


def _mod_kernel(cv_ref, w_ref, b_ref, o_ref):
    a = _silu(cv_ref[...])
    o_ref[...] = jnp.dot(a, w_ref[...], preferred_element_type=F32,
                         precision=lax.Precision.HIGHEST) + b_ref[...]


def _modulation(cvec, w_ada, b_ada):
    tn = 1536
    nn = (N_MOD * D_MODEL) // tn
    out = pl.pallas_call(
        _mod_kernel,
        grid=(DEPTH, nn),
        in_specs=[
            pl.BlockSpec((GROUPS, D_MODEL), lambda l, n: (0, 0)),
            pl.BlockSpec((None, D_MODEL, tn), lambda l, n: (l, 0, n)),
            pl.BlockSpec((None, 1, tn), lambda l, n: (l, 0, n)),
        ],
        out_specs=pl.BlockSpec((None, GROUPS, tn), lambda l, n: (l, 0, n)),
        out_shape=jax.ShapeDtypeStruct((DEPTH, GROUPS, N_MOD * D_MODEL), F32),
        compiler_params=pltpu.CompilerParams(
            dimension_semantics=("parallel", "parallel"), vmem_limit_bytes=VMEM_LIMIT),
        name="adaln_mod",
    )(cvec, w_ada, b_ada.reshape(DEPTH, 1, N_MOD * D_MODEL))
    return out.reshape(DEPTH, GROUPS, N_MOD, D_MODEL)


def _inproj_kernel(two_inputs, *refs):
    if two_inputs:
        (xp_ref, xs_ref, mod_ref, g_ref, w_ref, kc_in_ref, vc_in_ref, qkv_ref, pm_ref, kc_ref,
         vc_ref, wb_ref) = refs
        x = jnp.where(pl.program_id(0) < T_P // TM_IN, xp_ref[...], xs_ref[...])
    else:
        (x_ref, mod_ref, g_ref, w_ref, kc_in_ref, vc_in_ref, qkv_ref, pm_ref, kc_ref,
         vc_ref, wb_ref) = refs
        x = x_ref[...]
    del kc_in_ref, vc_in_ref

    @pl.when(pl.program_id(0) == 0)
    def _():
        wb_ref[...] = w_ref[...].astype(BF16)

    h = _rms(x, g_ref[0:1, :]) * (1.0 + mod_ref[1:2, :]) + mod_ref[0:1, :]
    hb = h.astype(BF16)
    qkv = jnp.dot(hb, wb_ref[:, 0:3 * D_ATT], preferred_element_type=F32)
    pm_ref[...] = jnp.dot(hb, wb_ref[:, 3 * D_ATT:], preferred_element_type=F32)
    qkv_ref[:, 0:D_ATT] = (qkv[:, 0:D_ATT] * np.float32(HEAD_DIM ** -0.5)).astype(BF16)
    qkv_ref[:, D_ATT:] = qkv[:, D_ATT:].astype(BF16)

    @pl.when(pl.program_id(0) < T_P // TM_IN)
    def _():
        for r in range(TM_IN // SEQ):
            rows = slice(r * SEQ, (r + 1) * SEQ)
            kc_ref[r] = qkv[rows, D_ATT:2 * D_ATT]
            vc_ref[r] = qkv[rows, 2 * D_ATT:3 * D_ATT]


def _token_stream_specs(x, tile):
    if not isinstance(x, tuple):
        return [pl.BlockSpec((tile, D_MODEL), lambda i: (i, 0))], [x]
    n_p = T_P // tile
    return ([pl.BlockSpec((tile, D_MODEL), lambda i: (jnp.minimum(i, n_p - 1), 0)),
             pl.BlockSpec((tile, D_MODEL), lambda i: (jnp.maximum(i - n_p, 0), 0))], list(x))


def _inproj(x, mod, norm_g, w_in, new_k, new_v, l):
    cache_spec = pl.BlockSpec((TM_IN // SEQ, None, SEQ, D_ATT),
                              lambda i: (jnp.minimum(i, T_P // TM_IN - 1), l, 0, 0))
    x_specs, x_args = _token_stream_specs(x, TM_IN)
    n_x = len(x_args)
    return pl.pallas_call(
        functools.partial(_inproj_kernel, n_x == 2),
        grid=(T // TM_IN,),
        in_specs=x_specs + [
            pl.BlockSpec((None, None, N_MOD, D_MODEL),
                         lambda i: (l, _group_of_tile(i * (TM_IN // TM)), 0, 0)),
            pl.BlockSpec((None, 4, D_MODEL), lambda i: (l, 0, 0)),
            pl.BlockSpec((None, D_MODEL, D_IN), lambda i: (l, 0, 0),
                         pipeline_mode=pl.Buffered(1)),
            pl.BlockSpec(memory_space=pl.ANY),
            pl.BlockSpec(memory_space=pl.ANY),
        ],
        out_specs=[pl.BlockSpec((TM_IN, 3 * D_ATT), lambda i: (i, 0)),
                   pl.BlockSpec((TM_IN, D_MIX_IN), lambda i: (i, 0)), cache_spec, cache_spec],
        out_shape=[jax.ShapeDtypeStruct((T, 3 * D_ATT), BF16),
                   jax.ShapeDtypeStruct((T, D_MIX_IN), F32),
                   jax.ShapeDtypeStruct(new_k.shape, F32),
                   jax.ShapeDtypeStruct(new_v.shape, F32)],
        input_output_aliases={n_x + 3: 2, n_x + 4: 3},
        scratch_shapes=[pltpu.VMEM((D_MODEL, D_IN), BF16)],
        compiler_params=pltpu.CompilerParams(
            dimension_semantics=("arbitrary",), vmem_limit_bytes=VMEM_LIMIT),
        name="inproj",
    )(*x_args, mod, norm_g, w_in, new_k, new_v)


def _dot_nt(a, b):
    return lax.dot_general(a, b, (((1,), (1,)), ((), ())), preferred_element_type=F32)


def _attn_kernel(q_ref, k0_ref, k1_ref, k2_ref, v0_ref, v1_ref, v2_ref,
                 ck_ref, cv_ref, bias_ref, o_ref):
    is_prompt = pl.program_id(0) < NT_P

    @pl.when(is_prompt)
    def _():
        _ctx_attn_body(q_ref, k0_ref, v0_ref, o_ref)

    @pl.when(jnp.logical_not(is_prompt))
    def _():
        _na_attn_body(q_ref, k0_ref, k1_ref, k2_ref, v0_ref, v1_ref, v2_ref,
                      ck_ref, cv_ref, bias_ref, o_ref)


def _ctx_attn_body(q_ref, k_ref, v_ref, o_ref):
    for h in range(H_ATT):
        sl = slice(h * HEAD_DIM, (h + 1) * HEAD_DIM)
        s = _dot_nt(q_ref[:, sl], k_ref[:, sl])
        m = jnp.max(s, axis=-1, keepdims=True)
        e = jnp.exp(s - m)
        den = jnp.sum(e, axis=-1, keepdims=True)
        o = jnp.dot(e.astype(BF16), v_ref[:, sl], preferred_element_type=F32) / den
        o_ref[:, sl] = o.astype(BF16)


def _na_attn_body(q_ref, k0_ref, k1_ref, k2_ref, v0_ref, v1_ref, v2_ref,
                  ck_ref, cv_ref, bias_ref, o_ref):
    k_refs = (k0_ref, k1_ref, k2_ref)
    v_refs = (v0_ref, v1_ref, v2_ref)
    for h in range(H_ATT):
        sl = slice(h * HEAD_DIM, (h + 1) * HEAD_DIM)
        q = q_ref[:, sl]
        s_loc = [_dot_nt(q, k_refs[j][:, sl]) + bias_ref[h, :, j * TM:(j + 1) * TM]
                 for j in range(BAND_TILES)]
        s_ctx = _dot_nt(q, ck_ref[:, sl])
        m = jnp.max(jnp.maximum(jnp.maximum(s_loc[0], s_loc[1]), jnp.maximum(s_loc[2], s_ctx)),
                    axis=-1, keepdims=True)
        es = [jnp.exp(s - m) for s in s_loc + [s_ctx]]
        den = jnp.sum((es[0] + es[1]) + (es[2] + es[3]), axis=-1, keepdims=True)
        e_all = jnp.concatenate([e.astype(BF16) for e in es], axis=1)
        v_all = jnp.concatenate([v_ref[:, sl] for v_ref in v_refs] + [cv_ref[:, sl]], axis=0)
        acc = jnp.dot(e_all, v_all, preferred_element_type=F32)
        o_ref[:, sl] = (acc / den).astype(BF16)


def _na_variant_tables():
    kh = min(NA_KH, ROWS)
    per_tile = []
    for rb in range(ROWS // QROWS):
        r0 = rb * QROWS
        bs = int(np.clip(rb - 1, 0, ROWS // QROWS - BAND_TILES)) * QROWS
        tab = -np.ones((QROWS, BAND_TILES * QROWS), np.int32)
        for qr in range(QROWS):
            r = r0 + qr
            rs = int(np.clip(r - kh // 2, 0, ROWS - kh))
            for kr in range(BAND_TILES * QROWS):
                ka = bs + kr
                if rs <= ka < rs + kh:
                    tab[qr, kr] = ka - r + (NA_KH - 1)
            assert (tab[qr] >= 0).sum() == kh
        per_tile.append(tab)
    variants, variant_of_tile = [], []
    for tab in per_tile:
        for vi, v in enumerate(variants):
            if np.array_equal(v, tab):
                variant_of_tile.append(vi)
                break
        else:
            variants.append(tab)
            variant_of_tile.append(len(variants) - 1)
    return np.asarray(variant_of_tile, np.int32), np.stack(variants)


_NA_VARIANT_OF_TILE, _NA_DR_IDX = _na_variant_tables()
_NA_NVAR = _NA_DR_IDX.shape[0]
_N_DR = 2 * NA_KH - 1
_N_DC = 2 * NA_KW - 1


def _bias_kernel(w_ref, o_ref):
    qc = lax.broadcasted_iota(jnp.int32, (GRID_W, LANES), 0)
    lane = lax.broadcasted_iota(jnp.int32, (GRID_W, LANES), 1)
    kc = lane % GRID_W
    cs = jnp.clip(qc - NA_KW // 2, 0, GRID_W - NA_KW)
    col_ok = jnp.logical_and(kc >= cs, kc < cs + NA_KW)
    left = lane < GRID_W
    neg = jnp.full((GRID_W, LANES), NEG_INF, F32)
    cache = {}
    for vi in range(_NA_NVAR):
        for qr in range(QROWS):
            for pp in range(BAND_TILES * QROWS // 2):
                d0 = int(_NA_DR_IDX[vi, qr, 2 * pp])
                d1 = int(_NA_DR_IDX[vi, qr, 2 * pp + 1])
                if (d0, d1) not in cache:
                    if d0 < 0 and d1 < 0:
                        tile = neg
                    else:
                        u = (w_ref[pl.ds(d0 if d0 >= 0 else _N_DR, 1), :]
                             + w_ref[pl.ds(_N_DR + 1 + (d1 if d1 >= 0 else _N_DR), 1), :])
                        t = pltpu.roll(jnp.broadcast_to(u, (GRID_W, LANES)), 0, 1,
                                       stride=1, stride_axis=0)
                        ok = col_ok
                        if d0 < 0:
                            ok = jnp.logical_and(ok, jnp.logical_not(left))
                        if d1 < 0:
                            ok = jnp.logical_and(ok, left)
                        tile = jnp.where(ok, t, neg)
                    cache[(d0, d1)] = tile
                o_ref[vi, qr * GRID_W:(qr + 1) * GRID_W, pp * LANES:(pp + 1) * LANES] = cache[(d0, d1)]


def _na_bias_tables(rpb):
    half = NA_KW - 1
    zeros = jnp.zeros((DEPTH, H_ATT, _N_DR, LANES - _N_DC), F32)
    w_lo = jnp.concatenate([rpb[..., half:], zeros, rpb[..., :half]], axis=-1)
    w_lo = jnp.pad(w_lo, ((0, 0), (0, 0), (0, 1), (0, 0)))
    w_hi = jnp.roll(w_lo, GRID_W, axis=-1)
    w = jnp.concatenate([w_lo, w_hi], axis=2)
    return pl.pallas_call(
        _bias_kernel,
        grid=(DEPTH, H_ATT),
        in_specs=[pl.BlockSpec((None, None, 2 * (_N_DR + 1), LANES), lambda l, h: (l, h, 0, 0))],
        out_specs=pl.BlockSpec((None, _NA_NVAR, None, TM, BAND), lambda l, h: (l, 0, h, 0, 0)),
        out_shape=jax.ShapeDtypeStruct((DEPTH, _NA_NVAR, H_ATT, TM, BAND), F32),
        compiler_params=pltpu.CompilerParams(
            dimension_semantics=("parallel", "parallel"), vmem_limit_bytes=VMEM_LIMIT),
        name="na_bias",
    )(w)


def _attention(p, ck_all, cv_all, bias_all, l):
    nrb = TILES_PER_DEC
    var_of_tile = [int(v) for v in _NA_VARIANT_OF_TILE]

    def dec_batch(i):
        return jnp.maximum(i - NT_P, 0) // nrb

    def band_tile(i, j):
        rb = (i - NT_P) % nrb
        first = NT_P + dec_batch(i) * nrb + jnp.clip(rb - 1, 0, nrb - BAND_TILES)
        return jnp.where(i < NT_P, i, first + j)

    def variant(i):
        rb = jnp.maximum(i - NT_P, 0) % nrb
        v = jnp.int32(var_of_tile[0])
        for t in range(1, nrb):
            v = jnp.where(rb >= t, jnp.int32(var_of_tile[t]), v)
        return v

    kv_specs = [pl.BlockSpec((TM, D_ATT), functools.partial(
        lambda i, j, col: (band_tile(i, j), col), j=j, col=col))
        for col in (1, 2) for j in range(BAND_TILES)]
    return pl.pallas_call(
        _attn_kernel,
        grid=(NT,),
        in_specs=[pl.BlockSpec((TM, D_ATT), lambda i: (i, 0))] + kv_specs + [
            pl.BlockSpec((None, None, PAST_LEN, D_ATT), lambda i: (dec_batch(i), l, 0, 0)),
            pl.BlockSpec((None, None, PAST_LEN, D_ATT), lambda i: (dec_batch(i), l, 0, 0)),
            pl.BlockSpec((None, None, H_ATT, TM, BAND), lambda i: (l, variant(i), 0, 0, 0)),
        ],
        out_specs=pl.BlockSpec((TM, D_ATT), lambda i: (i, 0)),
        out_shape=jax.ShapeDtypeStruct((T, D_ATT), BF16),
        compiler_params=pltpu.CompilerParams(
            dimension_semantics=("arbitrary",), vmem_limit_bytes=VMEM_LIMIT),
        name="attn",
    )(p, p, p, p, p, p, p, ck_all, cv_all, bias_all)


def _mixer_out_kernel(with_router, n_x, *refs):
    x_refs, refs = refs[10:10 + n_x], refs[:10] + refs[10 + n_x:]
    if with_router:
        (u_ref, vm_ref, bg_ref, cg_ref, hx_ref, cgp_ref, hxp_ref, cgn_ref, hxn_ref,
         oatt_ref, mod_ref, g_ref, gv_ref, ws_ref, bs_ref, wc_ref, wo_ref, wr_ref,
         x1_ref, h2_ref, route_ref, wob_ref) = refs
    else:
        (u_ref, vm_ref, bg_ref, cg_ref, hx_ref, cgp_ref, hxp_ref, cgn_ref, hxn_ref,
         oatt_ref, mod_ref, g_ref, gv_ref, ws_ref, bs_ref, wc_ref, wo_ref,
         x1_ref, h2_ref, wob_ref) = refs
    i = pl.program_id(0)
    if n_x == 2:
        x = jnp.where(i < NT_P, x_refs[0][...], x_refs[1][...])
    else:
        x = x_refs[0][...]

    @pl.when(i == 0)
    def _():
        wob_ref[...] = wo_ref[...].astype(BF16)

    u = _gelu_tanh(u_ref[...])
    vm = _rms(_gelu_tanh(vm_ref[...]), gv_ref[...]).astype(BF16)
    chunks = []
    for c in range(TM // CHUNK):
        rows = slice(c * CHUNK, (c + 1) * CHUNK)
        heads = [jnp.dot(ws_ref[h].astype(BF16), vm[rows, h * HEAD_DIM:(h + 1) * HEAD_DIM],
                         preferred_element_type=F32) for h in range(H_CMLP)]
        chunks.append(jnp.concatenate(heads, axis=1) + bs_ref[...])
    o_mlp = u * jnp.concatenate(chunks, axis=0)

    j = (i - NT_P) % TILES_PER_DEC
    has_prev = jnp.logical_and(i >= NT_P, j > 0)
    has_next = jnp.logical_and(i >= NT_P, j < TILES_PER_DEC - 1)
    z = cg_ref[...] * hx_ref[...]
    z_halo_prev = jnp.where(has_prev, cgp_ref[7:8, :] * hxp_ref[7:8, :], 0.0)
    z_halo_next = jnp.where(has_next, cgn_ref[0:1, :] * hxn_ref[0:1, :], 0.0)
    row = lax.broadcasted_iota(jnp.int32, (TM, D_CONV), 0)
    z_prev = jnp.where(row == 0, z_halo_prev, pltpu.roll(z, 1, 0))
    z_next = jnp.where(row == TM - 1, z_halo_next, pltpu.roll(z, TM - 1, 0))
    y_conv = z_prev * wc_ref[0:1, :] + z * wc_ref[1:2, :] + z_next * wc_ref[2:3, :]
    o_conv = bg_ref[...] * y_conv

    y = jnp.dot(oatt_ref[...], wob_ref[0:D_ATT, :], preferred_element_type=F32)
    y = y + jnp.dot(o_mlp.astype(BF16), wob_ref[D_ATT:D_ATT + D_CMLP, :],
                    preferred_element_type=F32)
    y = y + jnp.dot(o_conv.astype(BF16), wob_ref[D_ATT + D_CMLP:, :],
                    preferred_element_type=F32)

    x1 = x + mod_ref[2:3, :] * _rms(y, g_ref[1:2, :])
    h2 = _rms(x1, g_ref[2:3, :]) * (1.0 + mod_ref[4:5, :]) + mod_ref[3:4, :]
    x1_ref[...] = x1
    if with_router:
        _store_row_tiled(h2_ref, h2, TM)
    else:
        h2_ref[...] = h2

    if with_router:
        lg = [jnp.sum(h2 * wr_ref[e:e + 1, :], axis=-1, keepdims=True) for e in range(N_EXPERTS)]

        def top1(cols):
            m = cols[0]
            for col in cols[1:]:
                m = jnp.maximum(m, col)
            idx = jnp.full_like(m, N_EXPERTS - 1)
            for e in range(N_EXPERTS - 2, -1, -1):
                idx = jnp.where(cols[e] == m, np.float32(e), idx)
            return m, idx

        m1, i1 = top1(lg)
        m2, i2 = top1([jnp.where(i1 == np.float32(e), -jnp.inf, lg[e]) for e in range(N_EXPERTS)])
        lane = lax.broadcasted_iota(jnp.int32, (TM, 128), 1)
        e2 = jnp.exp(m2 - m1)
        den = 1.0 + e2
        gate1 = 1.0 / den
        gate2 = e2 / den
        route = jnp.where(lane == 0, i1,
                          jnp.where(lane == 1, i2,
                                    jnp.where(lane == 2, gate1,
                                              jnp.where(lane == 3, gate2, 0.0))))
        route_ref[...] = route


def _mixer_out(p, o_att, x, mod, norm_g, g_v, w_s, bs_b, wc_t, w_out, w_router_t, l, moe_idx):
    with_router = moe_idx is not None
    hb = TM // 8

    def col(cb):
        return pl.BlockSpec((TM, D_CMLP), lambda i: (i, cb))

    def halo_prev(cb):
        return pl.BlockSpec((8, D_CONV), lambda i: (jnp.maximum(i * hb - 1, 0), cb))

    def halo_next(cb):
        return pl.BlockSpec((8, D_CONV), lambda i: (jnp.minimum((i + 1) * hb, T // 8 - 1), cb))

    x_specs, x_args = _token_stream_specs(x, TM)
    in_specs = [col(0), col(1), col(2), col(3), col(4),
                halo_prev(3), halo_prev(4), halo_next(3), halo_next(4),
                pl.BlockSpec((TM, D_ATT), lambda i: (i, 0))] + x_specs + [
                pl.BlockSpec((None, None, N_MOD, D_MODEL), lambda i: (l, _group_of_tile(i), 0, 0)),
                pl.BlockSpec((None, 4, D_MODEL), lambda i: (l, 0, 0)),
                pl.BlockSpec((None, 1, D_CMLP), lambda i: (l, 0, 0)),
                pl.BlockSpec((None, H_CMLP, CHUNK, CHUNK), lambda i: (l, 0, 0, 0)),
                pl.BlockSpec((None, CHUNK, D_CMLP), lambda i: (l, 0, 0)),
                pl.BlockSpec((None, 3, D_CONV), lambda i: (l, 0, 0)),
                pl.BlockSpec((None, D_MODEL, D_MODEL), lambda i: (l, 0, 0),
                             pipeline_mode=pl.Buffered(1))]
    args = [p, p, p, p, p, p, p, p, p, o_att] + x_args + [mod, norm_g, g_v, w_s, bs_b, wc_t, w_out]
    out_specs = [pl.BlockSpec((TM, D_MODEL), lambda i: (i, 0)),
                 pl.BlockSpec((TM, D_MODEL), lambda i: (i, 0))]
    out_shape = [jax.ShapeDtypeStruct((T, D_MODEL), F32), jax.ShapeDtypeStruct((T, D_MODEL), F32)]
    if with_router:
        in_specs.append(pl.BlockSpec((None, N_EXPERTS, D_MODEL), lambda i: (moe_idx, 0, 0)))
        args.append(w_router_t)
        out_specs[1] = pl.BlockSpec((TM * ROW_SUB, LANES), lambda i: (i, 0))
        out_shape[1] = jax.ShapeDtypeStruct((T * ROW_SUB, LANES), F32)
        out_specs.append(pl.BlockSpec((TM, 128), lambda i: (i, 0)))
        out_shape.append(jax.ShapeDtypeStruct((T, 128), F32))
    return pl.pallas_call(
        functools.partial(_mixer_out_kernel, with_router, len(x_args)),
        grid=(NT,),
        in_specs=in_specs,
        out_specs=out_specs,
        out_shape=out_shape,
        scratch_shapes=[pltpu.VMEM((D_MODEL, D_MODEL), BF16)],
        compiler_params=pltpu.CompilerParams(
            dimension_semantics=("arbitrary",), vmem_limit_bytes=VMEM_LIMIT),
        name="mixer_out_router" if with_router else "mixer_out",
    )(*args)


LANES = 128
ROW_SUB = D_MODEL // LANES


def _load_row_tiled(ref, n):
    return jnp.concatenate([ref[pl.ds(c, n, stride=ROW_SUB), :] for c in range(ROW_SUB)], axis=1)


def _store_row_tiled(ref, val, n):
    for c in range(ROW_SUB):
        ref[pl.ds(c, n, stride=ROW_SUB), :] = val[:, c * LANES:(c + 1) * LANES]


def _ffn_kernel(row_tiled, te_ref, nv_ref, et_ref, x_ref, wga_ref, wua_ref, wda_ref,
                wgb_ref, wub_ref, wdb_ref, *rest):
    if row_tiled:
        o_ref, xb_ref, acc_ref = rest
    else:
        x1_ref, mod_ref, g_ref, o_ref, xb_ref, acc_ref = rest
    del te_ref, et_ref
    t = pl.program_id(0)
    j = pl.program_id(1)
    nvalid = nv_ref[t]
    nsub = (nvalid + (FF_SUB - 1)) // FF_SUB
    last = j == N_FF_STEPS - 1

    @pl.when(j == 0)
    def _():
        acc_ref[...] = jnp.zeros((FF_TILE, D_MODEL), F32)

    @pl.when(jnp.logical_and(j == 0, nvalid >= 0))
    def _():
        if row_tiled:
            xb_ref[...] = _load_row_tiled(x_ref, FF_TILE).astype(BF16)
        else:
            xb_ref[...] = x_ref[...].astype(BF16)

    def swiglu_rows(n, chunks):
        xs = xb_ref[0:n, :]
        acts, wds = [], []
        for wg_ref, wu_ref, wd_ref in chunks:
            g = jnp.dot(xs, wg_ref[...].astype(BF16), preferred_element_type=F32)
            u = jnp.dot(xs, wu_ref[...].astype(BF16), preferred_element_type=F32)
            acts.append((_silu(g) * u).astype(BF16))
            wds.append(wd_ref[...].astype(BF16))
        a = acts[0] if len(acts) == 1 else jnp.concatenate(acts, axis=1)
        wd = wds[0] if len(wds) == 1 else jnp.concatenate(wds, axis=0)
        acc_ref[0:n, :] += jnp.dot(a, wd, preferred_element_type=F32)

    chunk_a = (wga_ref, wua_ref, wda_ref)
    chunk_b = (wgb_ref, wub_ref, wdb_ref)
    n_options = range(1, FF_TILE // FF_SUB + 1) if row_tiled else (FF_TILE // FF_SUB,)
    for k in n_options:
        rows_ok = (nsub == k) if row_tiled else True
        pl.when(jnp.logical_and(rows_ok, jnp.logical_not(last)))(
            functools.partial(swiglu_rows, k * FF_SUB, (chunk_a, chunk_b)))
        pl.when(jnp.logical_and(rows_ok, last))(
            functools.partial(swiglu_rows, k * FF_SUB, (chunk_a,)))

    @pl.when(last)
    def _():
        if row_tiled:
            _store_row_tiled(o_ref, acc_ref[...], FF_TILE)
        else:
            o_ref[...] = x1_ref[...] + mod_ref[5:6, :] * _rms(acc_ref[...], g_ref[3:4, :])


def _ffn(x_rows, w_gu, w_d, tile_expert, tile_nvalid, tile_eff, n_tiles, residual, name):
    def chunk(t, j, nv, second):
        c = jnp.minimum(2 * j + second, N_FF_CH - 1)
        return jnp.where(nv[t] >= 0, c, N_FF_CH - 1)

    def weight_specs(second):
        return [
            pl.BlockSpec((None, D_MODEL, FF_CH),
                         lambda t, j, te, nv, et: (te[t], 0, chunk(t, j, nv, second))),
            pl.BlockSpec((None, D_MODEL, FF_CH),
                         lambda t, j, te, nv, et: (te[t], 0, N_FF_CH + chunk(t, j, nv, second))),
            pl.BlockSpec((None, FF_CH, D_MODEL),
                         lambda t, j, te, nv, et: (te[t], chunk(t, j, nv, second), 0)),
        ]

    row_tiled = residual is None
    if row_tiled:
        x_block = (FF_TILE * ROW_SUB, LANES)
        out_shape = jax.ShapeDtypeStruct((n_tiles * FF_TILE * ROW_SUB, LANES), F32)
        extra_specs, extra_args = [], ()
    else:
        x1, mod, norm_g, l = residual
        x_block = (FF_TILE, D_MODEL)
        out_shape = jax.ShapeDtypeStruct((n_tiles * FF_TILE, D_MODEL), F32)
        tiles_per_tm = FF_TILE // TM
        extra_specs = [
            pl.BlockSpec(x_block, lambda t, j, te, nv, et: (t, 0)),
            pl.BlockSpec((None, None, N_MOD, D_MODEL),
                         lambda t, j, te, nv, et: (l, _group_of_tile(t * tiles_per_tm), 0, 0)),
            pl.BlockSpec((None, 4, D_MODEL), lambda t, j, te, nv, et: (l, 0, 0)),
        ]
        extra_args = (x1, mod, norm_g)
    grid_spec = pltpu.PrefetchScalarGridSpec(
        num_scalar_prefetch=3,
        grid=(n_tiles, N_FF_STEPS),
        in_specs=[pl.BlockSpec(x_block, lambda t, j, te, nv, et: (et[t], 0))]
        + weight_specs(0) + weight_specs(1) + extra_specs,
        out_specs=pl.BlockSpec(x_block, lambda t, j, te, nv, et: (t, 0)),
        scratch_shapes=[pltpu.VMEM((FF_TILE, D_MODEL), BF16),
                        pltpu.VMEM((FF_TILE, D_MODEL), F32)],
    )
    return pl.pallas_call(
        functools.partial(_ffn_kernel, row_tiled),
        grid_spec=grid_spec,
        out_shape=out_shape,
        compiler_params=pltpu.CompilerParams(
            dimension_semantics=("arbitrary", "arbitrary"), vmem_limit_bytes=VMEM_LIMIT),
        name=name,
    )(tile_expert, tile_nvalid, tile_eff, x_rows, w_gu, w_gu, w_d, w_gu, w_gu, w_d, *extra_args)


def _dispatch_kernel(pos_ref, h_ref, xs_in_ref, xs_ref, sem):
    del xs_in_ref

    def row_copy(r, k):
        dst_row = pl.multiple_of(pos_ref[0, k * GATHER_CH + r] * ROW_SUB, ROW_SUB)
        src_row = pl.multiple_of(r * ROW_SUB, ROW_SUB)
        return pltpu.make_async_copy(h_ref.at[pl.ds(src_row, ROW_SUB)],
                                     xs_ref.at[pl.ds(dst_row, ROW_SUB)], sem)

    def start_group(g, carry):
        for u in range(GATHER_UNROLL):
            r = g * GATHER_UNROLL + u
            row_copy(r, 0).start(priority=0)
            row_copy(r, 1).start(priority=1)
        return carry

    lax.fori_loop(0, GATHER_CH // GATHER_UNROLL, start_group, 0)
    for _ in range(TOP_K):
        pltpu.make_async_copy(h_ref, xs_ref.at[pl.ds(0, GATHER_CH * ROW_SUB)], sem).wait()


def _dispatch(h2r, pos1, pos2):
    steps = T // GATHER_CH
    pos = jnp.concatenate([pos1.reshape(steps, 1, GATHER_CH), pos2.reshape(steps, 1, GATHER_CH)],
                          axis=2)
    return pl.pallas_call(
        _dispatch_kernel,
        grid=(steps,),
        in_specs=[pl.BlockSpec((None, 1, TOP_K * GATHER_CH), lambda i: (i, 0, 0),
                               memory_space=pltpu.SMEM),
                  pl.BlockSpec((GATHER_CH * ROW_SUB, LANES), lambda i: (i, 0)),
                  pl.BlockSpec(memory_space=pl.ANY)],
        out_specs=pl.BlockSpec(memory_space=pl.ANY),
        out_shape=jax.ShapeDtypeStruct((R_MOE * ROW_SUB, LANES), F32),
        input_output_aliases={2: 0},
        scratch_shapes=[pltpu.SemaphoreType.DMA(())],
        compiler_params=pltpu.CompilerParams(
            dimension_semantics=("arbitrary",), vmem_limit_bytes=VMEM_LIMIT),
        name="moe_dispatch",
    )(pos, h2r, jnp.zeros((R_MOE * ROW_SUB, LANES), F32))


def _combine_kernel(split_out, pos_ref, pos_next_ref, x1_ref, route_ref, mod_ref, g_ref, ys_ref,
                    *rest):
    if split_out:
        op_ref, os_ref, ybuf_ref, sems = rest
    else:
        o_ref, ybuf_ref, sems = rest
    i = pl.program_id(0)
    slot = i % 2
    n_rows = TOP_K * TM

    def row_copy(p_ref, s, r):
        src_row = pl.multiple_of(p_ref[0, r] * ROW_SUB, ROW_SUB)
        dst_row = pl.multiple_of(r * ROW_SUB, ROW_SUB)
        return pltpu.make_async_copy(ys_ref.at[pl.ds(src_row, ROW_SUB)],
                                     ybuf_ref.at[s, pl.ds(dst_row, ROW_SUB)], sems.at[s])

    def combine_rows():
        pltpu.make_async_copy(ys_ref.at[pl.ds(0, n_rows * ROW_SUB)], ybuf_ref.at[slot],
                              sems.at[slot]).wait()
        ybuf = ybuf_ref.at[slot]
        y1 = jnp.concatenate([ybuf[pl.ds(c, TM, stride=ROW_SUB), :] for c in range(ROW_SUB)],
                             axis=1)
        y2 = jnp.concatenate([ybuf[pl.ds(TM * ROW_SUB + c, TM, stride=ROW_SUB), :]
                              for c in range(ROW_SUB)], axis=1)
        f = route_ref[:, 2:3] * y1 + route_ref[:, 3:4] * y2
        out = x1_ref[...] + mod_ref[5:6, :] * _rms(f, g_ref[3:4, :])
        if split_out:
            @pl.when(i < NT_P)
            def _():
                op_ref[...] = out

            @pl.when(i >= NT_P)
            def _():
                os_ref[...] = out
        else:
            o_ref[...] = out

    @pl.when(i == 0)
    def _():
        def start_group(g, carry):
            for u in range(GATHER_UNROLL):
                row_copy(pos_ref, 0, g * GATHER_UNROLL + u).start(priority=u % 2)
            return carry

        lax.fori_loop(0, n_rows // GATHER_UNROLL, start_group, 0)

    @pl.when(i + 1 < pl.num_programs(0))
    def _():
        for r in range(n_rows):
            row_copy(pos_next_ref, 1 - slot, r).start(priority=r % 2)
        combine_rows()

    @pl.when(i + 1 >= pl.num_programs(0))
    def _():
        combine_rows()


def _combine(x1, ys, pos1, pos2, route, mod, norm_g, l, split_out):
    pos = jnp.concatenate([pos1.reshape(NT, 1, TM), pos2.reshape(NT, 1, TM)], axis=2)
    row = pl.BlockSpec((TM, D_MODEL), lambda i: (i, 0))
    pos_block = (None, 1, TOP_K * TM)
    if split_out:
        out_specs = [pl.BlockSpec((TM, D_MODEL), lambda i: (jnp.minimum(i, NT_P - 1), 0)),
                     pl.BlockSpec((TM, D_MODEL), lambda i: (jnp.maximum(i - NT_P, 0), 0))]
        out_shape = [jax.ShapeDtypeStruct((T_P, D_MODEL), F32),
                     jax.ShapeDtypeStruct((T_S, D_MODEL), F32)]
    else:
        out_specs, out_shape = row, jax.ShapeDtypeStruct((T, D_MODEL), F32)
    return pl.pallas_call(
        functools.partial(_combine_kernel, split_out),
        grid=(NT,),
        in_specs=[pl.BlockSpec(pos_block, lambda i: (i, 0, 0), memory_space=pltpu.SMEM),
                  pl.BlockSpec(pos_block, lambda i: (jnp.minimum(i + 1, NT - 1), 0, 0),
                               memory_space=pltpu.SMEM),
                  row,
                  pl.BlockSpec((TM, 128), lambda i: (i, 0)),
                  pl.BlockSpec((None, None, N_MOD, D_MODEL),
                               lambda i: (l, _group_of_tile(i), 0, 0)),
                  pl.BlockSpec((None, 4, D_MODEL), lambda i: (l, 0, 0)),
                  pl.BlockSpec(memory_space=pl.ANY)],
        out_specs=out_specs,
        out_shape=out_shape,
        scratch_shapes=[pltpu.VMEM((2, TOP_K * TM * ROW_SUB, LANES), F32),
                        pltpu.SemaphoreType.DMA((2,))],
        compiler_params=pltpu.CompilerParams(
            dimension_semantics=("arbitrary",), vmem_limit_bytes=VMEM_LIMIT),
        name="moe_combine",
    )(pos, pos, x1, route, mod, norm_g, ys)


def _route_plan(route):
    e1 = route[:, 0].astype(jnp.int32)
    e2 = route[:, 1].astype(jnp.int32)
    ar = jnp.arange(N_EXPERTS, dtype=jnp.int32)
    oh1 = (e1[:, None] == ar[None, :]).astype(jnp.int32)
    oh2 = (e2[:, None] == ar[None, :]).astype(jnp.int32)
    oh = oh1 + oh2
    csum_incl = jnp.cumsum(oh, axis=0)
    csum = csum_incl - oh
    counts = csum_incl[-1]
    ntile_e = (counts + FF_TILE - 1) // FF_TILE
    tile_end_e = jnp.cumsum(ntile_e)
    tile_start_e = tile_end_e - ntile_e
    gstart = tile_start_e * FF_TILE
    pos1 = jnp.sum((gstart[None, :] + csum) * oh1, axis=1)
    pos2 = jnp.sum((gstart[None, :] + csum) * oh2, axis=1)
    n_used = tile_end_e[-1]
    tiles = jnp.arange(NT_MOE, dtype=jnp.int32)
    eff = jnp.minimum(tiles, n_used - 1)
    te = jnp.sum((eff[:, None] >= tile_end_e[None, :]).astype(jnp.int32), axis=1)
    te = jnp.minimum(te, N_EXPERTS - 1)
    nvalid = jnp.clip(counts[te] - (eff - tile_start_e[te]) * FF_TILE, 0, FF_TILE)
    nvalid = jnp.where(tiles < n_used, nvalid, -1)
    return pos1, pos2, te.astype(jnp.int32), nvalid.astype(jnp.int32), eff.astype(jnp.int32)


def kernel(x_prompt, x_sample, cache_k, cache_v, c, c_ctx, w_ada, b_ada, norm_g, w_in, w_out,
           rpb, g_v, w_s, b_s, w_conv, w_ffn_gu, w_ffn_d, w_router, w_moe_gu, w_moe_d):
    x = (x_prompt.reshape(T_P, D_MODEL), x_sample.reshape(T_S, D_MODEL))
    cvec = jnp.concatenate([c_ctx[None], c, jnp.zeros((GROUPS - 1 - DEC_BATCH, D_MODEL), F32)],
                           axis=0)
    mod = _modulation(cvec, w_ada, b_ada)

    ck_all = cache_k.reshape(DEC_BATCH, DEPTH, PAST_LEN, D_ATT).astype(BF16)
    cv_all = cache_v.reshape(DEC_BATCH, DEPTH, PAST_LEN, D_ATT).astype(BF16)
    bias_all = _na_bias_tables(rpb)
    g_v3 = g_v.reshape(DEPTH, 1, D_CMLP)
    bs_b = jnp.repeat(jnp.swapaxes(b_s, 1, 2), HEAD_DIM, axis=2)
    wc_t = jnp.swapaxes(w_conv, 1, 2)
    w_router_t = jnp.swapaxes(w_router, 1, 2)
    w_moe_gu_all = w_moe_gu.reshape(-1, D_MODEL, 2 * D_FF)
    w_moe_d_all = w_moe_d.reshape(-1, D_FF, D_MODEL)

    dense_te = jnp.zeros((NT_DENSE,), jnp.int32)
    dense_nv = jnp.full((NT_DENSE,), FF_TILE, jnp.int32)
    dense_eff = jnp.arange(NT_DENSE, dtype=jnp.int32)

    new_k = jnp.zeros((BATCH, DEPTH, SEQ, D_ATT), F32)
    new_v = jnp.zeros((BATCH, DEPTH, SEQ, D_ATT), F32)
    for l in range(DEPTH):
        qkv, p, new_k, new_v = _inproj(x, mod, norm_g, w_in, new_k, new_v, l)
        o_att = _attention(qkv, ck_all, cv_all, bias_all, l)
        if l % 2 == 0:
            x1, h2 = _mixer_out(p, o_att, x, mod, norm_g, g_v3, w_s, bs_b, wc_t, w_out,
                                None, l, None)
            x = _ffn(h2, w_ffn_gu, w_ffn_d, dense_te + l // 2, dense_nv, dense_eff, NT_DENSE,
                     (x1, mod, norm_g, l), "ffn_dense")
        else:
            x1, h2, route = _mixer_out(p, o_att, x, mod, norm_g, g_v3, w_s, bs_b, wc_t, w_out,
                                       w_router_t, l, l // 2)
            pos1, pos2, te, nv, eff = _route_plan(route)
            xs = _dispatch(h2, pos1, pos2)
            ys = _ffn(xs, w_moe_gu_all, w_moe_d_all, te + (l // 2) * N_EXPERTS, nv, eff, NT_MOE,
                      None, "ffn_moe")
            x = _combine(x1, ys, pos1, pos2, route, mod, norm_g, l, l == DEPTH - 1)

    assert DEPTH % 2 == 0
    y_prompt = x[0].reshape(BATCH, SEQ, D_MODEL)
    y_sample = x[1].reshape(DEC_BATCH, DEC_SEQ, D_MODEL)
    cache_shape = (BATCH, DEPTH, SEQ, H_ATT, HEAD_DIM)
    return y_prompt, y_sample, new_k.reshape(cache_shape), new_v.reshape(cache_shape)
```

```python
import functools

import numpy as np
import jax
import jax.numpy as jnp
from jax import lax
from jax.experimental import pallas as pl
from jax.experimental.pallas import tpu as pltpu

F32 = jnp.float32
BF16 = jnp.bfloat16

D_MODEL = 1024
BATCH = 16
SEQ = 256
DEPTH = 4
DEC_BATCH = 4
DEC_SEQ = 2048
PAST_LEN = 256
GRID_W = 64
HEAD_DIM = 64
D_ATT = 512
D_CMLP = 256
D_CONV = 256
H_ATT = 8
H_CMLP = 4
CHUNK = 128
NA_KH = 8
NA_KW = 16
D_IN = 2816
D_MIX_IN = D_IN - 3 * D_ATT
D_FF = 2816
N_EXPERTS = 8
TOP_K = 2
N_MOD = 6
EPS = 1e-6
NEG_INF = -1e30

T_P = BATCH * SEQ
T_S = DEC_BATCH * DEC_SEQ
T = T_P + T_S
TM = 256
NT = T // TM
NT_P = T_P // TM
TM_IN = 512
TILES_PER_DEC = DEC_SEQ // TM
GROUPS = 8

ROWS = DEC_SEQ // GRID_W
QROWS = TM // GRID_W
BAND_TILES = 3
BAND = BAND_TILES * TM

FF_TILE = 1024
FF_SUB = 256
FF_CH = 256
N_FF_CH = D_FF // FF_CH
N_FF_STEPS = (N_FF_CH + 1) // 2
R_MOE = 2 * T + N_EXPERTS * FF_TILE
NT_MOE = R_MOE // FF_TILE
NT_DENSE = T // FF_TILE
GATHER_CH = 512
GATHER_UNROLL = 16

VMEM_LIMIT = 56 * 1024 * 1024


def _group_of_tile(i):
    return jnp.where(i < NT_P, 0, 1 + (i - NT_P) // TILES_PER_DEC)


def _rms(x, g):
    return x * lax.rsqrt(jnp.mean(x * x, axis=-1, keepdims=True) + EPS) * g


def _silu(x):
    return x / (1.0 + jnp.exp(-x))


def _gelu_tanh(x):
    c = np.float32(np.sqrt(2.0 / np.pi))
    return 0.5 * x * (1.0 + jnp.tanh(c * (x + np.float32(0.044715) * (x * x * x))))


def _mod_kernel(cv_ref, w_ref, b_ref, o_ref):
    a = _silu(cv_ref[...])
    o_ref[...] = jnp.dot(a, w_ref[...], preferred_element_type=F32,
                         precision=lax.Precision.HIGHEST) + b_ref[...]


def _modulation(cvec, w_ada, b_ada):
    tn = 1536
    nn = (N_MOD * D_MODEL) // tn
    out = pl.pallas_call(
        _mod_kernel,
        grid=(DEPTH, nn),
        in_specs=[
            pl.BlockSpec((GROUPS, D_MODEL), lambda l, n: (0, 0)),
            pl.BlockSpec((None, D_MODEL, tn), lambda l, n: (l, 0, n)),
            pl.BlockSpec((None, 1, tn), lambda l, n: (l, 0, n)),
        ],
        out_specs=pl.BlockSpec((None, GROUPS, tn), lambda l, n: (l, 0, n)),
        out_shape=jax.ShapeDtypeStruct((DEPTH, GROUPS, N_MOD * D_MODEL), F32),
        compiler_params=pltpu.CompilerParams(
            dimension_semantics=("parallel", "parallel"), vmem_limit_bytes=VMEM_LIMIT),
        name="adaln_mod",
    )(cvec, w_ada, b_ada.reshape(DEPTH, 1, N_MOD * D_MODEL))
    return out.reshape(DEPTH, GROUPS, N_MOD, D_MODEL)


def _inproj_kernel(two_inputs, *refs):
    if two_inputs:
        (xp_ref, xs_ref, mod_ref, g_ref, w_ref, kc_in_ref, vc_in_ref, qkv_ref, pm_ref, kc_ref,
         vc_ref, wb_ref) = refs
        x = jnp.where(pl.program_id(0) < T_P // TM_IN, xp_ref[...], xs_ref[...])
    else:
        (x_ref, mod_ref, g_ref, w_ref, kc_in_ref, vc_in_ref, qkv_ref, pm_ref, kc_ref,
         vc_ref, wb_ref) = refs
        x = x_ref[...]
    del kc_in_ref, vc_in_ref

    @pl.when(pl.program_id(0) == 0)
    def _():
        wb_ref[...] = w_ref[...].astype(BF16)

    h = _rms(x, g_ref[0:1, :]) * (1.0 + mod_ref[1:2, :]) + mod_ref[0:1, :]
    hb = h.astype(BF16)
    qkv = jnp.dot(hb, wb_ref[:, 0:3 * D_ATT], preferred_element_type=F32)
    pm_ref[...] = jnp.dot(hb, wb_ref[:, 3 * D_ATT:], preferred_element_type=F32)
    qkv_ref[:, 0:D_ATT] = (qkv[:, 0:D_ATT] * np.float32(HEAD_DIM ** -0.5)).astype(BF16)
    qkv_ref[:, D_ATT:] = qkv[:, D_ATT:].astype(BF16)

    @pl.when(pl.program_id(0) < T_P // TM_IN)
    def _():
        for r in range(TM_IN // SEQ):
            rows = slice(r * SEQ, (r + 1) * SEQ)
            kc_ref[r] = qkv[rows, D_ATT:2 * D_ATT]
            vc_ref[r] = qkv[rows, 2 * D_ATT:3 * D_ATT]


def _token_stream_specs(x, tile):
    if not isinstance(x, tuple):
        return [pl.BlockSpec((tile, D_MODEL), lambda i: (i, 0))], [x]
    n_p = T_P // tile
    return ([pl.BlockSpec((tile, D_MODEL), lambda i: (jnp.minimum(i, n_p - 1), 0)),
             pl.BlockSpec((tile, D_MODEL), lambda i: (jnp.maximum(i - n_p, 0), 0))], list(x))


def _inproj(x, mod, norm_g, w_in, new_k, new_v, l):
    cache_spec = pl.BlockSpec((TM_IN // SEQ, None, SEQ, D_ATT),
                              lambda i: (jnp.minimum(i, T_P // TM_IN - 1), l, 0, 0))
    x_specs, x_args = _token_stream_specs(x, TM_IN)
    n_x = len(x_args)
    return pl.pallas_call(
        functools.partial(_inproj_kernel, n_x == 2),
        grid=(T // TM_IN,),
        in_specs=x_specs + [
            pl.BlockSpec((None, None, N_MOD, D_MODEL),
                         lambda i: (l, _group_of_tile(i * (TM_IN // TM)), 0, 0)),
            pl.BlockSpec((None, 4, D_MODEL), lambda i: (l, 0, 0)),
            pl.BlockSpec((None, D_MODEL, D_IN), lambda i: (l, 0, 0),
                         pipeline_mode=pl.Buffered(1)),
            pl.BlockSpec(memory_space=pl.ANY),
            pl.BlockSpec(memory_space=pl.ANY),
        ],
        out_specs=[pl.BlockSpec((TM_IN, 3 * D_ATT), lambda i: (i, 0)),
                   pl.BlockSpec((TM_IN, D_MIX_IN), lambda i: (i, 0)), cache_spec, cache_spec],
        out_shape=[jax.ShapeDtypeStruct((T, 3 * D_ATT), BF16),
                   jax.ShapeDtypeStruct((T, D_MIX_IN), F32),
                   jax.ShapeDtypeStruct(new_k.shape, F32),
                   jax.ShapeDtypeStruct(new_v.shape, F32)],
        input_output_aliases={n_x + 3: 2, n_x + 4: 3},
        scratch_shapes=[pltpu.VMEM((D_MODEL, D_IN), BF16)],
        compiler_params=pltpu.CompilerParams(
            dimension_semantics=("arbitrary",), vmem_limit_bytes=VMEM_LIMIT),
        name="inproj",
    )(*x_args, mod, norm_g, w_in, new_k, new_v)


def _dot_nt(a, b):
    return lax.dot_general(a, b, (((1,), (1,)), ((), ())), preferred_element_type=F32)


def _attn_kernel(q_ref, k0_ref, k1_ref, k2_ref, v0_ref, v1_ref, v2_ref,
                 ck_ref, cv_ref, bias_ref, o_ref):
    is_prompt = pl.program_id(0) < NT_P

    @pl.when(is_prompt)
    def _():
        _ctx_attn_body(q_ref, k0_ref, v0_ref, o_ref)

    @pl.when(jnp.logical_not(is_prompt))
    def _():
        _na_attn_body(q_ref, k0_ref, k1_ref, k2_ref, v0_ref, v1_ref, v2_ref,
                      ck_ref, cv_ref, bias_ref, o_ref)


def _ctx_attn_body(q_ref, k_ref, v_ref, o_ref):
    for h in range(H_ATT):
        sl = slice(h * HEAD_DIM, (h + 1) * HEAD_DIM)
        s = _dot_nt(q_ref[:, sl], k_ref[:, sl])
        m = jnp.max(s, axis=-1, keepdims=True)
        e = jnp.exp(s - m)
        den = jnp.sum(e, axis=-1, keepdims=True)
        o = jnp.dot(e.astype(BF16), v_ref[:, sl], preferred_element_type=F32) / den
        o_ref[:, sl] = o.astype(BF16)


def _na_attn_body(q_ref, k0_ref, k1_ref, k2_ref, v0_ref, v1_ref, v2_ref,
                  ck_ref, cv_ref, bias_ref, o_ref):
    k_refs = (k0_ref, k1_ref, k2_ref)
    v_refs = (v0_ref, v1_ref, v2_ref)
    for h in range(H_ATT):
        sl = slice(h * HEAD_DIM, (h + 1) * HEAD_DIM)
        q = q_ref[:, sl]
        s_loc = [_dot_nt(q, k_refs[j][:, sl]) + bias_ref[h, :, j * TM:(j + 1) * TM]
                 for j in range(BAND_TILES)]
        s_ctx = _dot_nt(q, ck_ref[:, sl])
        m = jnp.max(jnp.maximum(jnp.maximum(s_loc[0], s_loc[1]), jnp.maximum(s_loc[2], s_ctx)),
                    axis=-1, keepdims=True)
        es = [jnp.exp(s - m) for s in s_loc + [s_ctx]]
        den = jnp.sum((es[0] + es[1]) + (es[2] + es[3]), axis=-1, keepdims=True)
        e_all = jnp.concatenate([e.astype(BF16) for e in es], axis=1)
        v_all = jnp.concatenate([v_ref[:, sl] for v_ref in v_refs] + [cv_ref[:, sl]], axis=0)
        acc = jnp.dot(e_all, v_all, preferred_element_type=F32)
        o_ref[:, sl] = (acc / den).astype(BF16)


def _na_variant_tables():
    kh = min(NA_KH, ROWS)
    per_tile = []
    for rb in range(ROWS // QROWS):
        r0 = rb * QROWS
        bs = int(np.clip(rb - 1, 0, ROWS // QROWS - BAND_TILES)) * QROWS
        tab = -np.ones((QROWS, BAND_TILES * QROWS), np.int32)
        for qr in range(QROWS):
            r = r0 + qr
            rs = int(np.clip(r - kh // 2, 0, ROWS - kh))
            for kr in range(BAND_TILES * QROWS):
                ka = bs + kr
                if rs <= ka < rs + kh:
                    tab[qr, kr] = ka - r + (NA_KH - 1)
            assert (tab[qr] >= 0).sum() == kh
        per_tile.append(tab)
    variants, variant_of_tile = [], []
    for tab in per_tile:
        for vi, v in enumerate(variants):
            if np.array_equal(v, tab):
                variant_of_tile.append(vi)
                break
        else:
            variants.append(tab)
            variant_of_tile.append(len(variants) - 1)
    return np.asarray(variant_of_tile, np.int32), np.stack(variants)


_NA_VARIANT_OF_TILE, _NA_DR_IDX = _na_variant_tables()
_NA_NVAR = _NA_DR_IDX.shape[0]
_N_DR = 2 * NA_KH - 1
_N_DC = 2 * NA_KW - 1


def _bias_kernel(w_ref, o_ref):
    qc = lax.broadcasted_iota(jnp.int32, (GRID_W, LANES), 0)
    lane = lax.broadcasted_iota(jnp.int32, (GRID_W, LANES), 1)
    kc = lane % GRID_W
    cs = jnp.clip(qc - NA_KW // 2, 0, GRID_W - NA_KW)
    col_ok = jnp.logical_and(kc >= cs, kc < cs + NA_KW)
    left = lane < GRID_W
    neg = jnp.full((GRID_W, LANES), NEG_INF, F32)
    cache = {}
    for vi in range(_NA_NVAR):
        for qr in range(QROWS):
            for pp in range(BAND_TILES * QROWS // 2):
                d0 = int(_NA_DR_IDX[vi, qr, 2 * pp])
                d1 = int(_NA_DR_IDX[vi, qr, 2 * pp + 1])
                if (d0, d1) not in cache:
                    if d0 < 0 and d1 < 0:
                        tile = neg
                    else:
                        u = (w_ref[pl.ds(d0 if d0 >= 0 else _N_DR, 1), :]
                             + w_ref[pl.ds(_N_DR + 1 + (d1 if d1 >= 0 else _N_DR), 1), :])
                        t = pltpu.roll(jnp.broadcast_to(u, (GRID_W, LANES)), 0, 1,
                                       stride=1, stride_axis=0)
                        ok = col_ok
                        if d0 < 0:
                            ok = jnp.logical_and(ok, jnp.logical_not(left))
                        if d1 < 0:
                            ok = jnp.logical_and(ok, left)
                        tile = jnp.where(ok, t, neg)
                    cache[(d0, d1)] = tile
                o_ref[vi, qr * GRID_W:(qr + 1) * GRID_W, pp * LANES:(pp + 1) * LANES] = cache[(d0, d1)]


def _na_bias_tables(rpb):
    half = NA_KW - 1
    zeros = jnp.zeros((DEPTH, H_ATT, _N_DR, LANES - _N_DC), F32)
    w_lo = jnp.concatenate([rpb[..., half:], zeros, rpb[..., :half]], axis=-1)
    w_lo = jnp.pad(w_lo, ((0, 0), (0, 0), (0, 1), (0, 0)))
    w_hi = jnp.roll(w_lo, GRID_W, axis=-1)
    w = jnp.concatenate([w_lo, w_hi], axis=2)
    return pl.pallas_call(
        _bias_kernel,
        grid=(DEPTH, H_ATT),
        in_specs=[pl.BlockSpec((None, None, 2 * (_N_DR + 1), LANES), lambda l, h: (l, h, 0, 0))],
        out_specs=pl.BlockSpec((None, _NA_NVAR, None, TM, BAND), lambda l, h: (l, 0, h, 0, 0)),
        out_shape=jax.ShapeDtypeStruct((DEPTH, _NA_NVAR, H_ATT, TM, BAND), F32),
        compiler_params=pltpu.CompilerParams(
            dimension_semantics=("parallel", "parallel"), vmem_limit_bytes=VMEM_LIMIT),
        name="na_bias",
    )(w)


def _attention(p, ck_all, cv_all, bias_all, l):
    nrb = TILES_PER_DEC
    var_of_tile = [int(v) for v in _NA_VARIANT_OF_TILE]

    def dec_batch(i):
        return jnp.maximum(i - NT_P, 0) // nrb

    def band_tile(i, j):
        rb = (i - NT_P) % nrb
        first = NT_P + dec_batch(i) * nrb + jnp.clip(rb - 1, 0, nrb - BAND_TILES)
        return jnp.where(i < NT_P, i, first + j)

    def variant(i):
        rb = jnp.maximum(i - NT_P, 0) % nrb
        v = jnp.int32(var_of_tile[0])
        for t in range(1, nrb):
            v = jnp.where(rb >= t, jnp.int32(var_of_tile[t]), v)
        return v

    kv_specs = [pl.BlockSpec((TM, D_ATT), functools.partial(
        lambda i, j, col: (band_tile(i, j), col), j=j, col=col))
        for col in (1, 2) for j in range(BAND_TILES)]
    return pl.pallas_call(
        _attn_kernel,
        grid=(NT,),
        in_specs=[pl.BlockSpec((TM, D_ATT), lambda i: (i, 0))] + kv_specs + [
            pl.BlockSpec((None, None, PAST_LEN, D_ATT), lambda i: (dec_batch(i), l, 0, 0)),
            pl.BlockSpec((None, None, PAST_LEN, D_ATT), lambda i: (dec_batch(i), l, 0, 0)),
            pl.BlockSpec((None, None, H_ATT, TM, BAND), lambda i: (l, variant(i), 0, 0, 0)),
        ],
        out_specs=pl.BlockSpec((TM, D_ATT), lambda i: (i, 0)),
        out_shape=jax.ShapeDtypeStruct((T, D_ATT), BF16),
        compiler_params=pltpu.CompilerParams(
            dimension_semantics=("arbitrary",), vmem_limit_bytes=VMEM_LIMIT),
        name="attn",
    )(p, p, p, p, p, p, p, ck_all, cv_all, bias_all)


def _mixer_out_kernel(with_router, n_x, *refs):
    x_refs, refs = refs[4:4 + n_x], refs[:4] + refs[4 + n_x:]
    if with_router:
        (pm_ref, halo_prev_ref, halo_next_ref, oatt_ref, mod_ref, g_ref, gv_ref, ws_ref, bs_ref,
         wc_ref, wo_ref, wr_ref, x1_ref, h2_ref, route_ref, wob_ref) = refs
    else:
        (pm_ref, halo_prev_ref, halo_next_ref, oatt_ref, mod_ref, g_ref, gv_ref, ws_ref, bs_ref,
         wc_ref, wo_ref, x1_ref, h2_ref, wob_ref) = refs
    i = pl.program_id(0)

    def mix_col(ref, rows, k):
        return ref[rows, k * D_CMLP:(k + 1) * D_CMLP]

    all_rows = slice(None)
    if n_x == 2:
        x = jnp.where(i < NT_P, x_refs[0][...], x_refs[1][...])
    else:
        x = x_refs[0][...]

    @pl.when(i == 0)
    def _():
        wob_ref[...] = wo_ref[...].astype(BF16)

    u = _gelu_tanh(mix_col(pm_ref, all_rows, 0))
    vm = _rms(_gelu_tanh(mix_col(pm_ref, all_rows, 1)), gv_ref[...]).astype(BF16)
    chunks = []
    for c in range(TM // CHUNK):
        rows = slice(c * CHUNK, (c + 1) * CHUNK)
        heads = [jnp.dot(ws_ref[h].astype(BF16), vm[rows, h * HEAD_DIM:(h + 1) * HEAD_DIM],
                         preferred_element_type=F32) for h in range(H_CMLP)]
        chunks.append(jnp.concatenate(heads, axis=1) + bs_ref[...])
    o_mlp = u * jnp.concatenate(chunks, axis=0)

    j = (i - NT_P) % TILES_PER_DEC
    has_prev = jnp.logical_and(i >= NT_P, j > 0)
    has_next = jnp.logical_and(i >= NT_P, j < TILES_PER_DEC - 1)
    z = mix_col(pm_ref, all_rows, 3) * mix_col(pm_ref, all_rows, 4)
    last_row, first_row = slice(7, 8), slice(0, 1)
    z_halo_prev = jnp.where(
        has_prev, mix_col(halo_prev_ref, last_row, 3) * mix_col(halo_prev_ref, last_row, 4), 0.0)
    z_halo_next = jnp.where(
        has_next, mix_col(halo_next_ref, first_row, 3) * mix_col(halo_next_ref, first_row, 4), 0.0)
    row = lax.broadcasted_iota(jnp.int32, (TM, D_CONV), 0)
    z_prev = jnp.where(row == 0, z_halo_prev, pltpu.roll(z, 1, 0))
    z_next = jnp.where(row == TM - 1, z_halo_next, pltpu.roll(z, TM - 1, 0))
    y_conv = z_prev * wc_ref[0:1, :] + z * wc_ref[1:2, :] + z_next * wc_ref[2:3, :]
    o_conv = mix_col(pm_ref, all_rows, 2) * y_conv

    y = jnp.dot(oatt_ref[...], wob_ref[0:D_ATT, :], preferred_element_type=F32)
    y = y + jnp.dot(o_mlp.astype(BF16), wob_ref[D_ATT:D_ATT + D_CMLP, :],
                    preferred_element_type=F32)
    y = y + jnp.dot(o_conv.astype(BF16), wob_ref[D_ATT + D_CMLP:, :],
                    preferred_element_type=F32)

    x1 = x + mod_ref[2:3, :] * _rms(y, g_ref[1:2, :])
    h2 = _rms(x1, g_ref[2:3, :]) * (1.0 + mod_ref[4:5, :]) + mod_ref[3:4, :]
    x1_ref[...] = x1
    if with_router:
        _store_row_tiled(h2_ref, h2, TM)
    else:
        h2_ref[...] = h2

    if with_router:
        lg = [jnp.sum(h2 * wr_ref[e:e + 1, :], axis=-1, keepdims=True) for e in range(N_EXPERTS)]

        def top1(cols):
            m = cols[0]
            for col in cols[1:]:
                m = jnp.maximum(m, col)
            idx = jnp.full_like(m, N_EXPERTS - 1)
            for e in range(N_EXPERTS - 2, -1, -1):
                idx = jnp.where(cols[e] == m, np.float32(e), idx)
            return m, idx

        m1, i1 = top1(lg)
        m2, i2 = top1([jnp.where(i1 == np.float32(e), -jnp.inf, lg[e]) for e in range(N_EXPERTS)])
        lane = lax.broadcasted_iota(jnp.int32, (TM, 128), 1)
        e2 = jnp.exp(m2 - m1)
        den = 1.0 + e2
        gate1 = 1.0 / den
        gate2 = e2 / den
        route = jnp.where(lane == 0, i1,
                          jnp.where(lane == 1, i2,
                                    jnp.where(lane == 2, gate1,
                                              jnp.where(lane == 3, gate2, 0.0))))
        route_ref[...] = route


def _mixer_out(p, o_att, x, mod, norm_g, g_v, w_s, bs_b, wc_t, w_out, w_router_t, l, moe_idx):
    with_router = moe_idx is not None
    hb = TM // 8

    x_specs, x_args = _token_stream_specs(x, TM)
    in_specs = [pl.BlockSpec((TM, D_MIX_IN), lambda i: (i, 0)),
                pl.BlockSpec((8, D_MIX_IN), lambda i: (jnp.maximum(i * hb - 1, 0), 0)),
                pl.BlockSpec((8, D_MIX_IN), lambda i: (jnp.minimum((i + 1) * hb, T // 8 - 1), 0)),
                pl.BlockSpec((TM, D_ATT), lambda i: (i, 0))] + x_specs + [
                pl.BlockSpec((None, None, N_MOD, D_MODEL), lambda i: (l, _group_of_tile(i), 0, 0)),
                pl.BlockSpec((None, 4, D_MODEL), lambda i: (l, 0, 0)),
                pl.BlockSpec((None, 1, D_CMLP), lambda i: (l, 0, 0)),
                pl.BlockSpec((None, H_CMLP, CHUNK, CHUNK), lambda i: (l, 0, 0, 0)),
                pl.BlockSpec((None, CHUNK, D_CMLP), lambda i: (l, 0, 0)),
                pl.BlockSpec((None, 3, D_CONV), lambda i: (l, 0, 0)),
                pl.BlockSpec((None, D_MODEL, D_MODEL), lambda i: (l, 0, 0),
                             pipeline_mode=pl.Buffered(1))]
    args = [p, p, p, o_att] + x_args + [mod, norm_g, g_v, w_s, bs_b, wc_t, w_out]
    out_specs = [pl.BlockSpec((TM, D_MODEL), lambda i: (i, 0)),
                 pl.BlockSpec((TM, D_MODEL), lambda i: (i, 0))]
    out_shape = [jax.ShapeDtypeStruct((T, D_MODEL), F32), jax.ShapeDtypeStruct((T, D_MODEL), F32)]
    if with_router:
        in_specs.append(pl.BlockSpec((None, N_EXPERTS, D_MODEL), lambda i: (moe_idx, 0, 0)))
        args.append(w_router_t)
        out_specs[1] = pl.BlockSpec((TM * ROW_SUB, LANES), lambda i: (i, 0))
        out_shape[1] = jax.ShapeDtypeStruct((T * ROW_SUB, LANES), F32)
        out_specs.append(pl.BlockSpec((TM, 128), lambda i: (i, 0)))
        out_shape.append(jax.ShapeDtypeStruct((T, 128), F32))
    return pl.pallas_call(
        functools.partial(_mixer_out_kernel, with_router, len(x_args)),
        grid=(NT,),
        in_specs=in_specs,
        out_specs=out_specs,
        out_shape=out_shape,
        scratch_shapes=[pltpu.VMEM((D_MODEL, D_MODEL), BF16)],
        compiler_params=pltpu.CompilerParams(
            dimension_semantics=("arbitrary",), vmem_limit_bytes=VMEM_LIMIT),
        name="mixer_out_router" if with_router else "mixer_out",
    )(*args)


LANES = 128
ROW_SUB = D_MODEL // LANES


def _load_row_tiled(ref, n):
    return jnp.concatenate([ref[pl.ds(c, n, stride=ROW_SUB), :] for c in range(ROW_SUB)], axis=1)


def _store_row_tiled(ref, val, n):
    for c in range(ROW_SUB):
        ref[pl.ds(c, n, stride=ROW_SUB), :] = val[:, c * LANES:(c + 1) * LANES]


def _ffn_kernel(row_tiled, te_ref, nv_ref, et_ref, x_ref, wga_ref, wua_ref, wda_ref,
                wgb_ref, wub_ref, wdb_ref, *rest):
    if row_tiled:
        o_ref, xb_ref, acc_ref = rest
    else:
        x1_ref, mod_ref, g_ref, o_ref, xb_ref, acc_ref = rest
    del te_ref, et_ref
    t = pl.program_id(0)
    j = pl.program_id(1)
    nvalid = nv_ref[t]
    nsub = (nvalid + (FF_SUB - 1)) // FF_SUB
    first = j == 0
    last = j == N_FF_STEPS - 1

    @pl.when(jnp.logical_and(first, nsub == 0))
    def _():
        acc_ref[...] = jnp.zeros((FF_TILE, D_MODEL), F32)

    def swiglu_rows(n, chunks, is_first):
        if is_first:
            if row_tiled:
                xb_ref[0:n, :] = _load_row_tiled(x_ref, n).astype(BF16)
            else:
                xb_ref[0:n, :] = x_ref[0:n, :].astype(BF16)
        xs = xb_ref[0:n, :]
        acts, wds = [], []
        for wg_ref, wu_ref, wd_ref in chunks:
            g = jnp.dot(xs, wg_ref[...].astype(BF16), preferred_element_type=F32)
            u = jnp.dot(xs, wu_ref[...].astype(BF16), preferred_element_type=F32)
            acts.append((_silu(g) * u).astype(BF16))
            wds.append(wd_ref[...].astype(BF16))
        a = acts[0] if len(acts) == 1 else jnp.concatenate(acts, axis=1)
        wd = wds[0] if len(wds) == 1 else jnp.concatenate(wds, axis=0)
        part = jnp.dot(a, wd, preferred_element_type=F32)
        if is_first:
            acc_ref[0:n, :] = part
            if n < FF_TILE:
                acc_ref[n:, :] = jnp.zeros((FF_TILE - n, D_MODEL), F32)
        else:
            acc_ref[0:n, :] += part

    chunk_a = (wga_ref, wua_ref, wda_ref)
    chunk_b = (wgb_ref, wub_ref, wdb_ref)
    middle = jnp.logical_not(jnp.logical_or(first, last))
    n_options = range(1, FF_TILE // FF_SUB + 1) if row_tiled else (FF_TILE // FF_SUB,)
    for k in n_options:
        rows_ok = (nsub == k) if row_tiled else True
        pl.when(jnp.logical_and(rows_ok, first))(
            functools.partial(swiglu_rows, k * FF_SUB, (chunk_a, chunk_b), True))
        pl.when(jnp.logical_and(rows_ok, middle))(
            functools.partial(swiglu_rows, k * FF_SUB, (chunk_a, chunk_b), False))
        pl.when(jnp.logical_and(rows_ok, last))(
            functools.partial(swiglu_rows, k * FF_SUB, (chunk_a,), False))

    @pl.when(last)
    def _():
        if row_tiled:
            _store_row_tiled(o_ref, acc_ref[...], FF_TILE)
        else:
            o_ref[...] = x1_ref[...] + mod_ref[5:6, :] * _rms(acc_ref[...], g_ref[3:4, :])


def _ffn(x_rows, w_gu, w_d, tile_expert, tile_nvalid, tile_eff, n_tiles, residual, name):
    def chunk(t, j, nv, second):
        c = jnp.minimum(2 * j + second, N_FF_CH - 1)
        return jnp.where(nv[t] >= 0, c, N_FF_CH - 1)

    def weight_specs(second):
        return [
            pl.BlockSpec((None, D_MODEL, FF_CH),
                         lambda t, j, te, nv, et: (te[t], 0, chunk(t, j, nv, second))),
            pl.BlockSpec((None, D_MODEL, FF_CH),
                         lambda t, j, te, nv, et: (te[t], 0, N_FF_CH + chunk(t, j, nv, second))),
            pl.BlockSpec((None, FF_CH, D_MODEL),
                         lambda t, j, te, nv, et: (te[t], chunk(t, j, nv, second), 0)),
        ]

    row_tiled = residual is None
    if row_tiled:
        x_block = (FF_TILE * ROW_SUB, LANES)
        out_shape = jax.ShapeDtypeStruct((n_tiles * FF_TILE * ROW_SUB, LANES), F32)
        extra_specs, extra_args = [], ()
    else:
        x1, mod, norm_g, l = residual
        x_block = (FF_TILE, D_MODEL)
        out_shape = jax.ShapeDtypeStruct((n_tiles * FF_TILE, D_MODEL), F32)
        tiles_per_tm = FF_TILE // TM
        extra_specs = [
            pl.BlockSpec(x_block, lambda t, j, te, nv, et: (t, 0)),
            pl.BlockSpec((None, None, N_MOD, D_MODEL),
                         lambda t, j, te, nv, et: (l, _group_of_tile(t * tiles_per_tm), 0, 0)),
            pl.BlockSpec((None, 4, D_MODEL), lambda t, j, te, nv, et: (l, 0, 0)),
        ]
        extra_args = (x1, mod, norm_g)
    grid_spec = pltpu.PrefetchScalarGridSpec(
        num_scalar_prefetch=3,
        grid=(n_tiles, N_FF_STEPS),
        in_specs=[pl.BlockSpec(x_block, lambda t, j, te, nv, et: (et[t], 0))]
        + weight_specs(0) + weight_specs(1) + extra_specs,
        out_specs=pl.BlockSpec(x_block, lambda t, j, te, nv, et: (t, 0)),
        scratch_shapes=[pltpu.VMEM((FF_TILE, D_MODEL), BF16),
                        pltpu.VMEM((FF_TILE, D_MODEL), F32)],
    )
    return pl.pallas_call(
        functools.partial(_ffn_kernel, row_tiled),
        grid_spec=grid_spec,
        out_shape=out_shape,
        compiler_params=pltpu.CompilerParams(
            dimension_semantics=("arbitrary", "arbitrary"), vmem_limit_bytes=VMEM_LIMIT),
        name=name,
    )(tile_expert, tile_nvalid, tile_eff, x_rows, w_gu, w_gu, w_d, w_gu, w_gu, w_d, *extra_args)


def _dispatch_kernel(pos_ref, h_ref, xs_in_ref, xs_ref, sem):
    del xs_in_ref

    def row_copy(r, k):
        dst_row = pl.multiple_of(pos_ref[0, k * GATHER_CH + r] * ROW_SUB, ROW_SUB)
        src_row = pl.multiple_of(r * ROW_SUB, ROW_SUB)
        return pltpu.make_async_copy(h_ref.at[pl.ds(src_row, ROW_SUB)],
                                     xs_ref.at[pl.ds(dst_row, ROW_SUB)], sem)

    def start_group(g, carry):
        for u in range(GATHER_UNROLL):
            r = g * GATHER_UNROLL + u
            row_copy(r, 0).start(priority=0)
            row_copy(r, 1).start(priority=1)
        return carry

    lax.fori_loop(0, GATHER_CH // GATHER_UNROLL, start_group, 0)
    for _ in range(TOP_K):
        pltpu.make_async_copy(h_ref, xs_ref.at[pl.ds(0, GATHER_CH * ROW_SUB)], sem).wait()


def _dispatch(h2r, pos1, pos2):
    steps = T // GATHER_CH
    pos = jnp.concatenate([pos1.reshape(steps, 1, GATHER_CH), pos2.reshape(steps, 1, GATHER_CH)],
                          axis=2)
    return pl.pallas_call(
        _dispatch_kernel,
        grid=(steps,),
        in_specs=[pl.BlockSpec((None, 1, TOP_K * GATHER_CH), lambda i: (i, 0, 0),
                               memory_space=pltpu.SMEM),
                  pl.BlockSpec((GATHER_CH * ROW_SUB, LANES), lambda i: (i, 0)),
                  pl.BlockSpec(memory_space=pl.ANY)],
        out_specs=pl.BlockSpec(memory_space=pl.ANY),
        out_shape=jax.ShapeDtypeStruct((R_MOE * ROW_SUB, LANES), F32),
        input_output_aliases={2: 0},
        scratch_shapes=[pltpu.SemaphoreType.DMA(())],
        compiler_params=pltpu.CompilerParams(
            dimension_semantics=("arbitrary",), vmem_limit_bytes=VMEM_LIMIT),
        name="moe_dispatch",
    )(pos, h2r, jnp.zeros((R_MOE * ROW_SUB, LANES), F32))


def _combine_kernel(split_out, pos_ref, pos_next_ref, x1_ref, route_ref, mod_ref, g_ref, ys_ref,
                    *rest):
    if split_out:
        op_ref, os_ref, ybuf_ref, sems = rest
    else:
        o_ref, ybuf_ref, sems = rest
    i = pl.program_id(0)
    slot = i % 2
    n_rows = TOP_K * TM

    def row_copy(p_ref, s, r):
        src_row = pl.multiple_of(p_ref[0, r] * ROW_SUB, ROW_SUB)
        dst_row = pl.multiple_of(r * ROW_SUB, ROW_SUB)
        return pltpu.make_async_copy(ys_ref.at[pl.ds(src_row, ROW_SUB)],
                                     ybuf_ref.at[s, pl.ds(dst_row, ROW_SUB)], sems.at[s])

    def combine_rows():
        pltpu.make_async_copy(ys_ref.at[pl.ds(0, n_rows * ROW_SUB)], ybuf_ref.at[slot],
                              sems.at[slot]).wait()
        ybuf = ybuf_ref.at[slot]
        y1 = jnp.concatenate([ybuf[pl.ds(c, TM, stride=ROW_SUB), :] for c in range(ROW_SUB)],
                             axis=1)
        y2 = jnp.concatenate([ybuf[pl.ds(TM * ROW_SUB + c, TM, stride=ROW_SUB), :]
                              for c in range(ROW_SUB)], axis=1)
        f = route_ref[:, 2:3] * y1 + route_ref[:, 3:4] * y2
        out = x1_ref[...] + mod_ref[5:6, :] * _rms(f, g_ref[3:4, :])
        if split_out:
            @pl.when(i < NT_P)
            def _():
                op_ref[...] = out

            @pl.when(i >= NT_P)
            def _():
                os_ref[...] = out
        else:
            o_ref[...] = out

    @pl.when(i == 0)
    def _():
        def start_group(g, carry):
            for u in range(GATHER_UNROLL):
                row_copy(pos_ref, 0, g * GATHER_UNROLL + u).start(priority=u % 2)
            return carry

        lax.fori_loop(0, n_rows // GATHER_UNROLL, start_group, 0)

    @pl.when(i + 1 < pl.num_programs(0))
    def _():
        for r in range(n_rows):
            row_copy(pos_next_ref, 1 - slot, r).start(priority=r % 2)
        combine_rows()

    @pl.when(i + 1 >= pl.num_programs(0))
    def _():
        combine_rows()


def _combine(x1, ys, pos1, pos2, route, mod, norm_g, l, split_out):
    pos = jnp.concatenate([pos1.reshape(NT, 1, TM), pos2.reshape(NT, 1, TM)], axis=2)
    row = pl.BlockSpec((TM, D_MODEL), lambda i: (i, 0))
    pos_block = (None, 1, TOP_K * TM)
    if split_out:
        out_specs = [pl.BlockSpec((TM, D_MODEL), lambda i: (jnp.minimum(i, NT_P - 1), 0)),
                     pl.BlockSpec((TM, D_MODEL), lambda i: (jnp.maximum(i - NT_P, 0), 0))]
        out_shape = [jax.ShapeDtypeStruct((T_P, D_MODEL), F32),
                     jax.ShapeDtypeStruct((T_S, D_MODEL), F32)]
    else:
        out_specs, out_shape = row, jax.ShapeDtypeStruct((T, D_MODEL), F32)
    return pl.pallas_call(
        functools.partial(_combine_kernel, split_out),
        grid=(NT,),
        in_specs=[pl.BlockSpec(pos_block, lambda i: (i, 0, 0), memory_space=pltpu.SMEM),
                  pl.BlockSpec(pos_block, lambda i: (jnp.minimum(i + 1, NT - 1), 0, 0),
                               memory_space=pltpu.SMEM),
                  row,
                  pl.BlockSpec((TM, 128), lambda i: (i, 0)),
                  pl.BlockSpec((None, None, N_MOD, D_MODEL),
                               lambda i: (l, _group_of_tile(i), 0, 0)),
                  pl.BlockSpec((None, 4, D_MODEL), lambda i: (l, 0, 0)),
                  pl.BlockSpec(memory_space=pl.ANY)],
        out_specs=out_specs,
        out_shape=out_shape,
        scratch_shapes=[pltpu.VMEM((2, TOP_K * TM * ROW_SUB, LANES), F32),
                        pltpu.SemaphoreType.DMA((2,))],
        compiler_params=pltpu.CompilerParams(
            dimension_semantics=("arbitrary",), vmem_limit_bytes=VMEM_LIMIT),
        name="moe_combine",
    )(pos, pos, x1, route, mod, norm_g, ys)


def _route_plan(route):
    e1 = route[:, 0].astype(jnp.int32)
    e2 = route[:, 1].astype(jnp.int32)
    ar = jnp.arange(N_EXPERTS, dtype=jnp.int32)
    oh1 = (e1[:, None] == ar[None, :]).astype(jnp.int32)
    oh2 = (e2[:, None] == ar[None, :]).astype(jnp.int32)
    oh = oh1 + oh2
    csum_incl = jnp.cumsum(oh, axis=0)
    csum = csum_incl - oh
    counts = csum_incl[-1]
    ntile_e = (counts + FF_TILE - 1) // FF_TILE
    tile_end_e = jnp.cumsum(ntile_e)
    tile_start_e = tile_end_e - ntile_e
    gstart = tile_start_e * FF_TILE
    pos1 = jnp.sum((gstart[None, :] + csum) * oh1, axis=1)
    pos2 = jnp.sum((gstart[None, :] + csum) * oh2, axis=1)
    n_used = tile_end_e[-1]
    tiles = jnp.arange(NT_MOE, dtype=jnp.int32)
    eff = jnp.minimum(tiles, n_used - 1)
    te = jnp.sum((eff[:, None] >= tile_end_e[None, :]).astype(jnp.int32), axis=1)
    te = jnp.minimum(te, N_EXPERTS - 1)
    nvalid = jnp.clip(counts[te] - (eff - tile_start_e[te]) * FF_TILE, 0, FF_TILE)
    nvalid = jnp.where(tiles < n_used, nvalid, -1)
    return pos1, pos2, te.astype(jnp.int32), nvalid.astype(jnp.int32), eff.astype(jnp.int32)


def kernel(x_prompt, x_sample, cache_k, cache_v, c, c_ctx, w_ada, b_ada, norm_g, w_in, w_out,
           rpb, g_v, w_s, b_s, w_conv, w_ffn_gu, w_ffn_d, w_router, w_moe_gu, w_moe_d):
    x = (x_prompt.reshape(T_P, D_MODEL), x_sample.reshape(T_S, D_MODEL))
    cvec = jnp.concatenate([c_ctx[None], c, jnp.zeros((GROUPS - 1 - DEC_BATCH, D_MODEL), F32)],
                           axis=0)
    mod = _modulation(cvec, w_ada, b_ada)

    ck_all = cache_k.reshape(DEC_BATCH, DEPTH, PAST_LEN, D_ATT).astype(BF16)
    cv_all = cache_v.reshape(DEC_BATCH, DEPTH, PAST_LEN, D_ATT).astype(BF16)
    bias_all = _na_bias_tables(rpb)
    g_v3 = g_v.reshape(DEPTH, 1, D_CMLP)
    bs_b = jnp.repeat(jnp.swapaxes(b_s, 1, 2), HEAD_DIM, axis=2)
    wc_t = jnp.swapaxes(w_conv, 1, 2)
    w_router_t = jnp.swapaxes(w_router, 1, 2)
    w_moe_gu_all = w_moe_gu.reshape(-1, D_MODEL, 2 * D_FF)
    w_moe_d_all = w_moe_d.reshape(-1, D_FF, D_MODEL)

    dense_te = jnp.zeros((NT_DENSE,), jnp.int32)
    dense_nv = jnp.full((NT_DENSE,), FF_TILE, jnp.int32)
    dense_eff = jnp.arange(NT_DENSE, dtype=jnp.int32)

    new_k = jnp.zeros((BATCH, DEPTH, SEQ, D_ATT), F32)
    new_v = jnp.zeros((BATCH, DEPTH, SEQ, D_ATT), F32)
    for l in range(DEPTH):
        qkv, p, new_k, new_v = _inproj(x, mod, norm_g, w_in, new_k, new_v, l)
        o_att = _attention(qkv, ck_all, cv_all, bias_all, l)
        if l % 2 == 0:
            x1, h2 = _mixer_out(p, o_att, x, mod, norm_g, g_v3, w_s, bs_b, wc_t, w_out,
                                None, l, None)
            x = _ffn(h2, w_ffn_gu, w_ffn_d, dense_te + l // 2, dense_nv, dense_eff, NT_DENSE,
                     (x1, mod, norm_g, l), "ffn_dense")
        else:
            x1, h2, route = _mixer_out(p, o_att, x, mod, norm_g, g_v3, w_s, bs_b, wc_t, w_out,
                                       w_router_t, l, l // 2)
            pos1, pos2, te, nv, eff = _route_plan(route)
            xs = _dispatch(h2, pos1, pos2)
            ys = _ffn(xs, w_moe_gu_all, w_moe_d_all, te + (l // 2) * N_EXPERTS, nv, eff, NT_MOE,
                      None, "ffn_moe")
            x = _combine(x1, ys, pos1, pos2, route, mod, norm_g, l, l == DEPTH - 1)

    assert DEPTH % 2 == 0
    y_prompt = x[0].reshape(BATCH, SEQ, D_MODEL)
    y_sample = x[1].reshape(DEC_BATCH, DEC_SEQ, D_MODEL)
    cache_shape = (BATCH, DEPTH, SEQ, H_ATT, HEAD_DIM)
    return y_prompt, y_sample, new_k.reshape(cache_shape), new_v.reshape(cache_shape)
```

```python
import functools

import numpy as np
import jax
import jax.numpy as jnp
from jax import lax
from jax.experimental import pallas as pl
from jax.experimental.pallas import tpu as pltpu

F32 = jnp.float32
BF16 = jnp.bfloat16

D_MODEL = 1024
BATCH = 16
SEQ = 256
DEPTH = 4
DEC_BATCH = 4
DEC_SEQ = 2048
PAST_LEN = 256
GRID_W = 64
HEAD_DIM = 64
D_ATT = 512
D_CMLP = 256
D_CONV = 256
H_ATT = 8
H_CMLP = 4
CHUNK = 128
NA_KH = 8
NA_KW = 16
D_IN = 2816
D_MIX_IN = D_IN - 3 * D_ATT
D_FF = 2816
N_EXPERTS = 8
TOP_K = 2
N_MOD = 6
EPS = 1e-6
NEG_INF = -1e30

T_P = BATCH * SEQ
T_S = DEC_BATCH * DEC_SEQ
T = T_P + T_S
TM = 256
NT = T // TM
NT_P = T_P // TM
TM_IN = 512
TILES_PER_DEC = DEC_SEQ // TM
GROUPS = 8

ROWS = DEC_SEQ // GRID_W
QROWS = TM // GRID_W
BAND_TILES = 3
BAND = BAND_TILES * TM

FF_TILE = 1024
FF_SUB = 256
FF_CH = 256
N_FF_CH = D_FF // FF_CH
N_FF_STEPS = (N_FF_CH + 1) // 2
R_MOE = 2 * T + N_EXPERTS * FF_TILE
NT_MOE = R_MOE // FF_TILE
NT_DENSE = T // FF_TILE
GATHER_CH = 512
GATHER_UNROLL = 16

VMEM_LIMIT = 56 * 1024 * 1024


def _group_of_tile(i):
    return jnp.where(i < NT_P, 0, 1 + (i - NT_P) // TILES_PER_DEC)


def _rms(x, g):
    return x * lax.rsqrt(jnp.mean(x * x, axis=-1, keepdims=True) + EPS) * g


def _silu(x):
    return x / (1.0 + jnp.exp(-x))


def _gelu_tanh(x):
    c = np.float32(np.sqrt(2.0 / np.pi))
    return 0.5 * x * (1.0 + jnp.tanh(c * (x + np.float32(0.044715) * (x * x * x))))


def _mod_kernel(cv_ref, w_ref, b_ref, o_ref):
    a = _silu(cv_ref[...])
    o_ref[...] = jnp.dot(a, w_ref[...], preferred_element_type=F32,
                         precision=lax.Precision.HIGHEST) + b_ref[...]


def _modulation(cvec, w_ada, b_ada):
    tn = 3072
    nn = (N_MOD * D_MODEL) // tn
    out = pl.pallas_call(
        _mod_kernel,
        grid=(DEPTH, nn),
        in_specs=[
            pl.BlockSpec((GROUPS, D_MODEL), lambda l, n: (0, 0)),
            pl.BlockSpec((None, D_MODEL, tn), lambda l, n: (l, 0, n)),
            pl.BlockSpec((None, 1, tn), lambda l, n: (l, 0, n)),
        ],
        out_specs=pl.BlockSpec((None, GROUPS, tn), lambda l, n: (l, 0, n)),
        out_shape=jax.ShapeDtypeStruct((DEPTH, GROUPS, N_MOD * D_MODEL), F32),
        compiler_params=pltpu.CompilerParams(
            dimension_semantics=("parallel", "parallel"), vmem_limit_bytes=VMEM_LIMIT),
        name="adaln_mod",
    )(cvec, w_ada, b_ada.reshape(DEPTH, 1, N_MOD * D_MODEL))
    return out.reshape(DEPTH, GROUPS, N_MOD, D_MODEL)


def _inproj_kernel(two_inputs, *refs):
    if two_inputs:
        (xp_ref, xs_ref, mod_ref, g_ref, w_ref, kc_in_ref, vc_in_ref, qkv_ref, pm_ref, kc_ref,
         vc_ref, wb_ref) = refs
        x = jnp.where(pl.program_id(0) < T_P // TM_IN, xp_ref[...], xs_ref[...])
    else:
        (x_ref, mod_ref, g_ref, w_ref, kc_in_ref, vc_in_ref, qkv_ref, pm_ref, kc_ref,
         vc_ref, wb_ref) = refs
        x = x_ref[...]
    del kc_in_ref, vc_in_ref

    @pl.when(pl.program_id(0) == 0)
    def _():
        wb_ref[...] = w_ref[...].astype(BF16)

    h = _rms(x, g_ref[0:1, :]) * (1.0 + mod_ref[1:2, :]) + mod_ref[0:1, :]
    hb = h.astype(BF16)
    qkv = jnp.dot(hb, wb_ref[:, 0:3 * D_ATT], preferred_element_type=F32)
    pm_ref[...] = jnp.dot(hb, wb_ref[:, 3 * D_ATT:], preferred_element_type=F32)
    qkv_ref[:, 0:D_ATT] = (qkv[:, 0:D_ATT] * np.float32(HEAD_DIM ** -0.5)).astype(BF16)
    qkv_ref[:, D_ATT:] = qkv[:, D_ATT:].astype(BF16)

    @pl.when(pl.program_id(0) < T_P // TM_IN)
    def _():
        for r in range(TM_IN // SEQ):
            rows = slice(r * SEQ, (r + 1) * SEQ)
            kc_ref[r] = qkv[rows, D_ATT:2 * D_ATT]
            vc_ref[r] = qkv[rows, 2 * D_ATT:3 * D_ATT]


def _token_stream_specs(x, tile):
    if not isinstance(x, tuple):
        return [pl.BlockSpec((tile, D_MODEL), lambda i: (i, 0))], [x]
    n_p = T_P // tile
    return ([pl.BlockSpec((tile, D_MODEL), lambda i: (jnp.minimum(i, n_p - 1), 0)),
             pl.BlockSpec((tile, D_MODEL), lambda i: (jnp.maximum(i - n_p, 0), 0))], list(x))


def _inproj(x, mod, norm_g, w_in, new_k, new_v, l):
    cache_spec = pl.BlockSpec((TM_IN // SEQ, None, SEQ, D_ATT),
                              lambda i: (jnp.minimum(i, T_P // TM_IN - 1), l, 0, 0))
    x_specs, x_args = _token_stream_specs(x, TM_IN)
    n_x = len(x_args)
    return pl.pallas_call(
        functools.partial(_inproj_kernel, n_x == 2),
        grid=(T // TM_IN,),
        in_specs=x_specs + [
            pl.BlockSpec((None, None, N_MOD, D_MODEL),
                         lambda i: (l, _group_of_tile(i * (TM_IN // TM)), 0, 0)),
            pl.BlockSpec((None, 4, D_MODEL), lambda i: (l, 0, 0)),
            pl.BlockSpec((None, D_MODEL, D_IN), lambda i: (l, 0, 0),
                         pipeline_mode=pl.Buffered(1)),
            pl.BlockSpec(memory_space=pl.ANY),
            pl.BlockSpec(memory_space=pl.ANY),
        ],
        out_specs=[pl.BlockSpec((TM_IN, 3 * D_ATT), lambda i: (i, 0)),
                   pl.BlockSpec((TM_IN, D_MIX_IN), lambda i: (i, 0)), cache_spec, cache_spec],
        out_shape=[jax.ShapeDtypeStruct((T, 3 * D_ATT), BF16),
                   jax.ShapeDtypeStruct((T, D_MIX_IN), F32),
                   jax.ShapeDtypeStruct(new_k.shape, F32),
                   jax.ShapeDtypeStruct(new_v.shape, F32)],
        input_output_aliases={n_x + 3: 2, n_x + 4: 3},
        scratch_shapes=[pltpu.VMEM((D_MODEL, D_IN), BF16)],
        compiler_params=pltpu.CompilerParams(
            dimension_semantics=("arbitrary",), vmem_limit_bytes=VMEM_LIMIT),
        name="inproj",
    )(*x_args, mod, norm_g, w_in, new_k, new_v)


def _dot_nt(a, b):
    return lax.dot_general(a, b, (((1,), (1,)), ((), ())), preferred_element_type=F32)


def _attn_kernel(q_ref, k0_ref, k1_ref, k2_ref, v0_ref, v1_ref, v2_ref,
                 ck_ref, cv_ref, bias_ref, o_ref):
    is_prompt = pl.program_id(0) < NT_P

    @pl.when(is_prompt)
    def _():
        _ctx_attn_body(q_ref, k0_ref, v0_ref, o_ref)

    @pl.when(jnp.logical_not(is_prompt))
    def _():
        _na_attn_body(q_ref, k0_ref, k1_ref, k2_ref, v0_ref, v1_ref, v2_ref,
                      ck_ref, cv_ref, bias_ref, o_ref)


def _ctx_attn_body(q_ref, k_ref, v_ref, o_ref):
    for h in range(H_ATT):
        sl = slice(h * HEAD_DIM, (h + 1) * HEAD_DIM)
        s = _dot_nt(q_ref[:, sl], k_ref[:, sl])
        m = jnp.max(s, axis=-1, keepdims=True)
        e = jnp.exp(s - m)
        den = jnp.sum(e, axis=-1, keepdims=True)
        o = jnp.dot(e.astype(BF16), v_ref[:, sl], preferred_element_type=F32) / den
        o_ref[:, sl] = o.astype(BF16)


def _na_attn_body(q_ref, k0_ref, k1_ref, k2_ref, v0_ref, v1_ref, v2_ref,
                  ck_ref, cv_ref, bias_ref, o_ref):
    k_refs = (k0_ref, k1_ref, k2_ref)
    v_refs = (v0_ref, v1_ref, v2_ref)
    for h in range(H_ATT):
        sl = slice(h * HEAD_DIM, (h + 1) * HEAD_DIM)
        q = q_ref[:, sl]
        s_loc = [_dot_nt(q, k_refs[j][:, sl]) + bias_ref[h, :, j * TM:(j + 1) * TM]
                 for j in range(BAND_TILES)]
        s_ctx = _dot_nt(q, ck_ref[:, sl])
        m = jnp.max(jnp.maximum(jnp.maximum(s_loc[0], s_loc[1]), jnp.maximum(s_loc[2], s_ctx)),
                    axis=-1, keepdims=True)
        es = [jnp.exp(s - m) for s in s_loc + [s_ctx]]
        den = jnp.sum((es[0] + es[1]) + (es[2] + es[3]), axis=-1, keepdims=True)
        e_all = jnp.concatenate([e.astype(BF16) for e in es], axis=1)
        v_all = jnp.concatenate([v_ref[:, sl] for v_ref in v_refs] + [cv_ref[:, sl]], axis=0)
        acc = jnp.dot(e_all, v_all, preferred_element_type=F32)
        o_ref[:, sl] = (acc / den).astype(BF16)


def _na_variant_tables():
    kh = min(NA_KH, ROWS)
    per_tile = []
    for rb in range(ROWS // QROWS):
        r0 = rb * QROWS
        bs = int(np.clip(rb - 1, 0, ROWS // QROWS - BAND_TILES)) * QROWS
        tab = -np.ones((QROWS, BAND_TILES * QROWS), np.int32)
        for qr in range(QROWS):
            r = r0 + qr
            rs = int(np.clip(r - kh // 2, 0, ROWS - kh))
            for kr in range(BAND_TILES * QROWS):
                ka = bs + kr
                if rs <= ka < rs + kh:
                    tab[qr, kr] = ka - r + (NA_KH - 1)
            assert (tab[qr] >= 0).sum() == kh
        per_tile.append(tab)
    variants, variant_of_tile = [], []
    for tab in per_tile:
        for vi, v in enumerate(variants):
            if np.array_equal(v, tab):
                variant_of_tile.append(vi)
                break
        else:
            variants.append(tab)
            variant_of_tile.append(len(variants) - 1)
    return np.asarray(variant_of_tile, np.int32), np.stack(variants)


_NA_VARIANT_OF_TILE, _NA_DR_IDX = _na_variant_tables()
_NA_NVAR = _NA_DR_IDX.shape[0]
_N_DR = 2 * NA_KH - 1
_N_DC = 2 * NA_KW - 1


def _bias_kernel(w_ref, o_ref):
    qc = lax.broadcasted_iota(jnp.int32, (GRID_W, LANES), 0)
    lane = lax.broadcasted_iota(jnp.int32, (GRID_W, LANES), 1)
    kc = lane % GRID_W
    cs = jnp.clip(qc - NA_KW // 2, 0, GRID_W - NA_KW)
    col_ok = jnp.logical_and(kc >= cs, kc < cs + NA_KW)
    left = lane < GRID_W
    neg = jnp.full((GRID_W, LANES), NEG_INF, F32)
    cache = {}
    for vi in range(_NA_NVAR):
        for qr in range(QROWS):
            for pp in range(BAND_TILES * QROWS // 2):
                d0 = int(_NA_DR_IDX[vi, qr, 2 * pp])
                d1 = int(_NA_DR_IDX[vi, qr, 2 * pp + 1])
                if (d0, d1) not in cache:
                    if d0 < 0 and d1 < 0:
                        tile = neg
                    else:
                        u = (w_ref[pl.ds(d0 if d0 >= 0 else _N_DR, 1), :]
                             + w_ref[pl.ds(_N_DR + 1 + (d1 if d1 >= 0 else _N_DR), 1), :])
                        t = pltpu.roll(jnp.broadcast_to(u, (GRID_W, LANES)), 0, 1,
                                       stride=1, stride_axis=0)
                        ok = col_ok
                        if d0 < 0:
                            ok = jnp.logical_and(ok, jnp.logical_not(left))
                        if d1 < 0:
                            ok = jnp.logical_and(ok, left)
                        tile = jnp.where(ok, t, neg)
                    cache[(d0, d1)] = tile
                o_ref[vi, qr * GRID_W:(qr + 1) * GRID_W, pp * LANES:(pp + 1) * LANES] = cache[(d0, d1)]


def _na_bias_tables(rpb):
    half = NA_KW - 1
    zeros = jnp.zeros((DEPTH, H_ATT, _N_DR, LANES - _N_DC), F32)
    w_lo = jnp.concatenate([rpb[..., half:], zeros, rpb[..., :half]], axis=-1)
    w_lo = jnp.pad(w_lo, ((0, 0), (0, 0), (0, 1), (0, 0)))
    w_hi = jnp.roll(w_lo, GRID_W, axis=-1)
    w = jnp.concatenate([w_lo, w_hi], axis=2)
    return pl.pallas_call(
        _bias_kernel,
        grid=(DEPTH, H_ATT),
        in_specs=[pl.BlockSpec((None, None, 2 * (_N_DR + 1), LANES), lambda l, h: (l, h, 0, 0))],
        out_specs=pl.BlockSpec((None, _NA_NVAR, None, TM, BAND), lambda l, h: (l, 0, h, 0, 0)),
        out_shape=jax.ShapeDtypeStruct((DEPTH, _NA_NVAR, H_ATT, TM, BAND), F32),
        compiler_params=pltpu.CompilerParams(
            dimension_semantics=("parallel", "parallel"), vmem_limit_bytes=VMEM_LIMIT),
        name="na_bias",
    )(w)


def _attention(p, ck_all, cv_all, bias_all, l):
    nrb = TILES_PER_DEC
    var_of_tile = [int(v) for v in _NA_VARIANT_OF_TILE]

    def dec_batch(i):
        return jnp.maximum(i - NT_P, 0) // nrb

    def band_tile(i, j):
        rb = (i - NT_P) % nrb
        first = NT_P + dec_batch(i) * nrb + jnp.clip(rb - 1, 0, nrb - BAND_TILES)
        return jnp.where(i < NT_P, i, first + j)

    def variant(i):
        rb = jnp.maximum(i - NT_P, 0) % nrb
        v = jnp.int32(var_of_tile[0])
        for t in range(1, nrb):
            v = jnp.where(rb >= t, jnp.int32(var_of_tile[t]), v)
        return v

    kv_specs = [pl.BlockSpec((TM, D_ATT), functools.partial(
        lambda i, j, col: (band_tile(i, j), col), j=j, col=col))
        for col in (1, 2) for j in range(BAND_TILES)]
    return pl.pallas_call(
        _attn_kernel,
        grid=(NT,),
        in_specs=[pl.BlockSpec((TM, D_ATT), lambda i: (i, 0))] + kv_specs + [
            pl.BlockSpec((None, None, PAST_LEN, D_ATT), lambda i: (dec_batch(i), l, 0, 0)),
            pl.BlockSpec((None, None, PAST_LEN, D_ATT), lambda i: (dec_batch(i), l, 0, 0)),
            pl.BlockSpec((None, None, H_ATT, TM, BAND), lambda i: (l, variant(i), 0, 0, 0)),
        ],
        out_specs=pl.BlockSpec((TM, D_ATT), lambda i: (i, 0)),
        out_shape=jax.ShapeDtypeStruct((T, D_ATT), BF16),
        compiler_params=pltpu.CompilerParams(
            dimension_semantics=("arbitrary",), vmem_limit_bytes=VMEM_LIMIT),
        name="attn",
    )(p, p, p, p, p, p, p, ck_all, cv_all, bias_all)


def _mixer_out_kernel(with_router, n_x, *refs):
    x_refs, refs = refs[4:4 + n_x], refs[:4] + refs[4 + n_x:]
    if with_router:
        (pm_ref, halo_prev_ref, halo_next_ref, oatt_ref, mod_ref, g_ref, gv_ref, ws_ref, bs_ref,
         wc_ref, wo_ref, wr_ref, x1_ref, h2_ref, route_ref, wob_ref) = refs
    else:
        (pm_ref, halo_prev_ref, halo_next_ref, oatt_ref, mod_ref, g_ref, gv_ref, ws_ref, bs_ref,
         wc_ref, wo_ref, x1_ref, h2_ref, wob_ref) = refs
    i = pl.program_id(0)

    def mix_col(ref, rows, k):
        return ref[rows, k * D_CMLP:(k + 1) * D_CMLP]

    all_rows = slice(None)
    if n_x == 2:
        x = jnp.where(i < NT_P, x_refs[0][...], x_refs[1][...])
    else:
        x = x_refs[0][...]

    @pl.when(i == 0)
    def _():
        wob_ref[...] = wo_ref[...].astype(BF16)

    u = _gelu_tanh(mix_col(pm_ref, all_rows, 0))
    vm = _rms(_gelu_tanh(mix_col(pm_ref, all_rows, 1)), gv_ref[...]).astype(BF16)
    chunks = []
    for c in range(TM // CHUNK):
        rows = slice(c * CHUNK, (c + 1) * CHUNK)
        heads = [jnp.dot(ws_ref[h].astype(BF16), vm[rows, h * HEAD_DIM:(h + 1) * HEAD_DIM],
                         preferred_element_type=F32) for h in range(H_CMLP)]
        chunks.append(jnp.concatenate(heads, axis=1) + bs_ref[...])
    o_mlp = u * jnp.concatenate(chunks, axis=0)

    j = (i - NT_P) % TILES_PER_DEC
    has_prev = jnp.logical_and(i >= NT_P, j > 0)
    has_next = jnp.logical_and(i >= NT_P, j < TILES_PER_DEC - 1)
    z = mix_col(pm_ref, all_rows, 3) * mix_col(pm_ref, all_rows, 4)
    last_row, first_row = slice(7, 8), slice(0, 1)
    z_halo_prev = jnp.where(
        has_prev, mix_col(halo_prev_ref, last_row, 3) * mix_col(halo_prev_ref, last_row, 4), 0.0)
    z_halo_next = jnp.where(
        has_next, mix_col(halo_next_ref, first_row, 3) * mix_col(halo_next_ref, first_row, 4), 0.0)
    row = lax.broadcasted_iota(jnp.int32, (TM, D_CONV), 0)
    z_prev = jnp.where(row == 0, z_halo_prev, pltpu.roll(z, 1, 0))
    z_next = jnp.where(row == TM - 1, z_halo_next, pltpu.roll(z, TM - 1, 0))
    y_conv = z_prev * wc_ref[0:1, :] + z * wc_ref[1:2, :] + z_next * wc_ref[2:3, :]
    o_conv = mix_col(pm_ref, all_rows, 2) * y_conv

    y = jnp.dot(oatt_ref[...], wob_ref[0:D_ATT, :], preferred_element_type=F32)
    y = y + jnp.dot(o_mlp.astype(BF16), wob_ref[D_ATT:D_ATT + D_CMLP, :],
                    preferred_element_type=F32)
    y = y + jnp.dot(o_conv.astype(BF16), wob_ref[D_ATT + D_CMLP:, :],
                    preferred_element_type=F32)

    x1 = x + mod_ref[2:3, :] * _rms(y, g_ref[1:2, :])
    h2 = _rms(x1, g_ref[2:3, :]) * (1.0 + mod_ref[4:5, :]) + mod_ref[3:4, :]
    x1_ref[...] = x1
    if with_router:
        _store_row_tiled(h2_ref, h2, TM)
    else:
        h2_ref[...] = h2

    if with_router:
        lg = [jnp.sum(h2 * wr_ref[e:e + 1, :], axis=-1, keepdims=True) for e in range(N_EXPERTS)]

        def top1(cols):
            m = cols[0]
            for col in cols[1:]:
                m = jnp.maximum(m, col)
            idx = jnp.full_like(m, N_EXPERTS - 1)
            for e in range(N_EXPERTS - 2, -1, -1):
                idx = jnp.where(cols[e] == m, np.float32(e), idx)
            return m, idx

        m1, i1 = top1(lg)
        m2, i2 = top1([jnp.where(i1 == np.float32(e), -jnp.inf, lg[e]) for e in range(N_EXPERTS)])
        lane = lax.broadcasted_iota(jnp.int32, (TM, 128), 1)
        e2 = jnp.exp(m2 - m1)
        den = 1.0 + e2
        gate1 = 1.0 / den
        gate2 = e2 / den
        route = jnp.where(lane == 0, i1,
                          jnp.where(lane == 1, i2,
                                    jnp.where(lane == 2, gate1,
                                              jnp.where(lane == 3, gate2, 0.0))))
        route_ref[...] = route


def _mixer_out(p, o_att, x, mod, norm_g, g_v, w_s, bs_b, wc_t, w_out, w_router_t, l, moe_idx):
    with_router = moe_idx is not None
    hb = TM // 8

    x_specs, x_args = _token_stream_specs(x, TM)
    in_specs = [pl.BlockSpec((TM, D_MIX_IN), lambda i: (i, 0)),
                pl.BlockSpec((8, D_MIX_IN), lambda i: (jnp.maximum(i * hb - 1, 0), 0)),
                pl.BlockSpec((8, D_MIX_IN), lambda i: (jnp.minimum((i + 1) * hb, T // 8 - 1), 0)),
                pl.BlockSpec((TM, D_ATT), lambda i: (i, 0))] + x_specs + [
                pl.BlockSpec((None, None, N_MOD, D_MODEL), lambda i: (l, _group_of_tile(i), 0, 0)),
                pl.BlockSpec((None, 4, D_MODEL), lambda i: (l, 0, 0)),
                pl.BlockSpec((None, 1, D_CMLP), lambda i: (l, 0, 0)),
                pl.BlockSpec((None, H_CMLP, CHUNK, CHUNK), lambda i: (l, 0, 0, 0)),
                pl.BlockSpec((None, CHUNK, D_CMLP), lambda i: (l, 0, 0)),
                pl.BlockSpec((None, 3, D_CONV), lambda i: (l, 0, 0)),
                pl.BlockSpec((None, D_MODEL, D_MODEL), lambda i: (l, 0, 0),
                             pipeline_mode=pl.Buffered(1))]
    args = [p, p, p, o_att] + x_args + [mod, norm_g, g_v, w_s, bs_b, wc_t, w_out]
    out_specs = [pl.BlockSpec((TM, D_MODEL), lambda i: (i, 0)),
                 pl.BlockSpec((TM, D_MODEL), lambda i: (i, 0))]
    out_shape = [jax.ShapeDtypeStruct((T, D_MODEL), F32), jax.ShapeDtypeStruct((T, D_MODEL), F32)]
    if with_router:
        in_specs.append(pl.BlockSpec((None, N_EXPERTS, D_MODEL), lambda i: (moe_idx, 0, 0)))
        args.append(w_router_t)
        out_specs[1] = pl.BlockSpec((TM * ROW_SUB, LANES), lambda i: (i, 0))
        out_shape[1] = jax.ShapeDtypeStruct((T * ROW_SUB, LANES), F32)
        out_specs.append(pl.BlockSpec((TM, 128), lambda i: (i, 0)))
        out_shape.append(jax.ShapeDtypeStruct((T, 128), F32))
    return pl.pallas_call(
        functools.partial(_mixer_out_kernel, with_router, len(x_args)),
        grid=(NT,),
        in_specs=in_specs,
        out_specs=out_specs,
        out_shape=out_shape,
        scratch_shapes=[pltpu.VMEM((D_MODEL, D_MODEL), BF16)],
        compiler_params=pltpu.CompilerParams(
            dimension_semantics=("arbitrary",), vmem_limit_bytes=VMEM_LIMIT),
        name="mixer_out_router" if with_router else "mixer_out",
    )(*args)


LANES = 128
ROW_SUB = D_MODEL // LANES


def _load_row_tiled(ref, n):
    return jnp.concatenate([ref[pl.ds(c, n, stride=ROW_SUB), :] for c in range(ROW_SUB)], axis=1)


def _store_row_tiled(ref, val, n):
    for c in range(ROW_SUB):
        ref[pl.ds(c, n, stride=ROW_SUB), :] = val[:, c * LANES:(c + 1) * LANES]


def _ffn_kernel(row_tiled, te_ref, nv_ref, et_ref, x_ref, wga_ref, wua_ref, wda_ref,
                wgb_ref, wub_ref, wdb_ref, *rest):
    if row_tiled:
        o_ref, xb_ref, acc_ref = rest
    else:
        x1_ref, mod_ref, g_ref, o_ref, xb_ref, acc_ref = rest
    del te_ref, et_ref
    t = pl.program_id(0)
    j = pl.program_id(1)
    nvalid = nv_ref[t]
    nsub = (nvalid + (FF_SUB - 1)) // FF_SUB
    first = j == 0
    last = j == N_FF_STEPS - 1

    @pl.when(jnp.logical_and(last, nsub == 0))
    def _():
        o_ref[...] = jnp.zeros(o_ref.shape, F32)

    def swiglu_rows(n, chunks, is_first, is_last):
        if is_first:
            if row_tiled:
                xb_ref[0:n, :] = _load_row_tiled(x_ref, n).astype(BF16)
            else:
                xb_ref[0:n, :] = x_ref[0:n, :].astype(BF16)
        xs = xb_ref[0:n, :]
        acts, wds = [], []
        for wg_ref, wu_ref, wd_ref in chunks:
            g = jnp.dot(xs, wg_ref[...].astype(BF16), preferred_element_type=F32)
            u = jnp.dot(xs, wu_ref[...].astype(BF16), preferred_element_type=F32)
            acts.append((_silu(g) * u).astype(BF16))
            wds.append(wd_ref[...].astype(BF16))
        a = acts[0] if len(acts) == 1 else jnp.concatenate(acts, axis=1)
        wd = wds[0] if len(wds) == 1 else jnp.concatenate(wds, axis=0)
        part = jnp.dot(a, wd, preferred_element_type=F32)
        if is_first:
            acc_ref[0:n, :] = part
        elif not is_last:
            acc_ref[0:n, :] += part
        elif row_tiled:
            _store_row_tiled(o_ref, acc_ref[0:n, :] + part, n)
            if n < FF_TILE:
                o_ref[n * ROW_SUB:, :] = jnp.zeros(((FF_TILE - n) * ROW_SUB, LANES), F32)
        else:
            f = acc_ref[0:n, :] + part
            o_ref[...] = x1_ref[...] + mod_ref[5:6, :] * _rms(f, g_ref[3:4, :])

    chunk_a = (wga_ref, wua_ref, wda_ref)
    chunk_b = (wgb_ref, wub_ref, wdb_ref)
    middle = jnp.logical_not(jnp.logical_or(first, last))
    n_options = range(1, FF_TILE // FF_SUB + 1) if row_tiled else (FF_TILE // FF_SUB,)
    for k in n_options:
        rows_ok = (nsub == k) if row_tiled else True
        pl.when(jnp.logical_and(rows_ok, first))(
            functools.partial(swiglu_rows, k * FF_SUB, (chunk_a, chunk_b), True, False))
        pl.when(jnp.logical_and(rows_ok, middle))(
            functools.partial(swiglu_rows, k * FF_SUB, (chunk_a, chunk_b), False, False))
        pl.when(jnp.logical_and(rows_ok, last))(
            functools.partial(swiglu_rows, k * FF_SUB, (chunk_a,), False, True))


def _ffn(x_rows, w_gu, w_d, tile_expert, tile_nvalid, tile_eff, n_tiles, residual, name):
    def chunk(t, j, nv, second):
        c = jnp.minimum(2 * j + second, N_FF_CH - 1)
        return jnp.where(nv[t] >= 0, c, N_FF_CH - 1)

    def weight_specs(second):
        return [
            pl.BlockSpec((None, D_MODEL, FF_CH),
                         lambda t, j, te, nv, et: (te[t], 0, chunk(t, j, nv, second))),
            pl.BlockSpec((None, D_MODEL, FF_CH),
                         lambda t, j, te, nv, et: (te[t], 0, N_FF_CH + chunk(t, j, nv, second))),
            pl.BlockSpec((None, FF_CH, D_MODEL),
                         lambda t, j, te, nv, et: (te[t], chunk(t, j, nv, second), 0)),
        ]

    row_tiled = residual is None
    if row_tiled:
        x_block = (FF_TILE * ROW_SUB, LANES)
        out_shape = jax.ShapeDtypeStruct((n_tiles * FF_TILE * ROW_SUB, LANES), F32)
        extra_specs, extra_args = [], ()
    else:
        x1, mod, norm_g, l = residual
        x_block = (FF_TILE, D_MODEL)
        out_shape = jax.ShapeDtypeStruct((n_tiles * FF_TILE, D_MODEL), F32)
        tiles_per_tm = FF_TILE // TM
        extra_specs = [
            pl.BlockSpec(x_block, lambda t, j, te, nv, et: (t, 0)),
            pl.BlockSpec((None, None, N_MOD, D_MODEL),
                         lambda t, j, te, nv, et: (l, _group_of_tile(t * tiles_per_tm), 0, 0)),
            pl.BlockSpec((None, 4, D_MODEL), lambda t, j, te, nv, et: (l, 0, 0)),
        ]
        extra_args = (x1, mod, norm_g)
    grid_spec = pltpu.PrefetchScalarGridSpec(
        num_scalar_prefetch=3,
        grid=(n_tiles, N_FF_STEPS),
        in_specs=[pl.BlockSpec(x_block, lambda t, j, te, nv, et: (et[t], 0))]
        + weight_specs(0) + weight_specs(1) + extra_specs,
        out_specs=pl.BlockSpec(x_block, lambda t, j, te, nv, et: (t, 0)),
        scratch_shapes=[pltpu.VMEM((FF_TILE, D_MODEL), BF16),
                        pltpu.VMEM((FF_TILE, D_MODEL), F32)],
    )
    return pl.pallas_call(
        functools.partial(_ffn_kernel, row_tiled),
        grid_spec=grid_spec,
        out_shape=out_shape,
        compiler_params=pltpu.CompilerParams(
            dimension_semantics=("arbitrary", "arbitrary"), vmem_limit_bytes=VMEM_LIMIT),
        name=name,
    )(tile_expert, tile_nvalid, tile_eff, x_rows, w_gu, w_gu, w_d, w_gu, w_gu, w_d, *extra_args)


def _dispatch_kernel(pos_ref, h_ref, xs_in_ref, xs_ref, sem):
    del xs_in_ref

    def row_copy(r, k):
        dst_row = pl.multiple_of(pos_ref[0, k * GATHER_CH + r] * ROW_SUB, ROW_SUB)
        src_row = pl.multiple_of(r * ROW_SUB, ROW_SUB)
        return pltpu.make_async_copy(h_ref.at[pl.ds(src_row, ROW_SUB)],
                                     xs_ref.at[pl.ds(dst_row, ROW_SUB)], sem)

    def start_group(g, carry):
        for u in range(GATHER_UNROLL):
            r = g * GATHER_UNROLL + u
            row_copy(r, 0).start(priority=0)
            row_copy(r, 1).start(priority=1)
        return carry

    lax.fori_loop(0, GATHER_CH // GATHER_UNROLL, start_group, 0)
    for _ in range(TOP_K):
        pltpu.make_async_copy(h_ref, xs_ref.at[pl.ds(0, GATHER_CH * ROW_SUB)], sem).wait()


def _dispatch(h2r, pos1, pos2):
    steps = T // GATHER_CH
    pos = jnp.concatenate([pos1.reshape(steps, 1, GATHER_CH), pos2.reshape(steps, 1, GATHER_CH)],
                          axis=2)
    return pl.pallas_call(
        _dispatch_kernel,
        grid=(steps,),
        in_specs=[pl.BlockSpec((None, 1, TOP_K * GATHER_CH), lambda i: (i, 0, 0),
                               memory_space=pltpu.SMEM),
                  pl.BlockSpec((GATHER_CH * ROW_SUB, LANES), lambda i: (i, 0)),
                  pl.BlockSpec(memory_space=pl.ANY)],
        out_specs=pl.BlockSpec(memory_space=pl.ANY),
        out_shape=jax.ShapeDtypeStruct((R_MOE * ROW_SUB, LANES), F32),
        input_output_aliases={2: 0},
        scratch_shapes=[pltpu.SemaphoreType.DMA(())],
        compiler_params=pltpu.CompilerParams(
            dimension_semantics=("arbitrary",), vmem_limit_bytes=VMEM_LIMIT),
        name="moe_dispatch",
    )(pos, h2r, jnp.zeros((R_MOE * ROW_SUB, LANES), F32))


def _combine_kernel(split_out, pos_ref, pos_next_ref, x1_ref, route_ref, mod_ref, g_ref, ys_ref,
                    *rest):
    if split_out:
        op_ref, os_ref, ybuf_ref, sems = rest
    else:
        o_ref, ybuf_ref, sems = rest
    i = pl.program_id(0)
    slot = i % 2
    n_rows = TOP_K * TM

    def row_copy(p_ref, s, r):
        src_row = pl.multiple_of(p_ref[0, r] * ROW_SUB, ROW_SUB)
        dst_row = pl.multiple_of(r * ROW_SUB, ROW_SUB)
        return pltpu.make_async_copy(ys_ref.at[pl.ds(src_row, ROW_SUB)],
                                     ybuf_ref.at[s, pl.ds(dst_row, ROW_SUB)], sems.at[s])

    def combine_rows():
        pltpu.make_async_copy(ys_ref.at[pl.ds(0, n_rows * ROW_SUB)], ybuf_ref.at[slot],
                              sems.at[slot]).wait()
        ybuf = ybuf_ref.at[slot]
        y1 = jnp.concatenate([ybuf[pl.ds(c, TM, stride=ROW_SUB), :] for c in range(ROW_SUB)],
                             axis=1)
        y2 = jnp.concatenate([ybuf[pl.ds(TM * ROW_SUB + c, TM, stride=ROW_SUB), :]
                              for c in range(ROW_SUB)], axis=1)
        f = route_ref[:, 2:3] * y1 + route_ref[:, 3:4] * y2
        out = x1_ref[...] + mod_ref[5:6, :] * _rms(f, g_ref[3:4, :])
        if split_out:
            @pl.when(i < NT_P)
            def _():
                op_ref[...] = out

            @pl.when(i >= NT_P)
            def _():
                os_ref[...] = out
        else:
            o_ref[...] = out

    @pl.when(i == 0)
    def _():
        def start_group(g, carry):
            for u in range(GATHER_UNROLL):
                row_copy(pos_ref, 0, g * GATHER_UNROLL + u).start(priority=u % 2)
            return carry

        lax.fori_loop(0, n_rows // GATHER_UNROLL, start_group, 0)

    @pl.when(i + 1 < pl.num_programs(0))
    def _():
        for r in range(n_rows):
            row_copy(pos_next_ref, 1 - slot, r).start(priority=r % 2)
        combine_rows()

    @pl.when(i + 1 >= pl.num_programs(0))
    def _():
        combine_rows()


def _combine(x1, ys, pos1, pos2, route, mod, norm_g, l, split_out):
    pos = jnp.concatenate([pos1.reshape(NT, 1, TM), pos2.reshape(NT, 1, TM)], axis=2)
    row = pl.BlockSpec((TM, D_MODEL), lambda i: (i, 0))
    pos_block = (None, 1, TOP_K * TM)
    if split_out:
        out_specs = [pl.BlockSpec((TM, D_MODEL), lambda i: (jnp.minimum(i, NT_P - 1), 0)),
                     pl.BlockSpec((TM, D_MODEL), lambda i: (jnp.maximum(i - NT_P, 0), 0))]
        out_shape = [jax.ShapeDtypeStruct((T_P, D_MODEL), F32),
                     jax.ShapeDtypeStruct((T_S, D_MODEL), F32)]
    else:
        out_specs, out_shape = row, jax.ShapeDtypeStruct((T, D_MODEL), F32)
    return pl.pallas_call(
        functools.partial(_combine_kernel, split_out),
        grid=(NT,),
        in_specs=[pl.BlockSpec(pos_block, lambda i: (i, 0, 0), memory_space=pltpu.SMEM),
                  pl.BlockSpec(pos_block, lambda i: (jnp.minimum(i + 1, NT - 1), 0, 0),
                               memory_space=pltpu.SMEM),
                  row,
                  pl.BlockSpec((TM, 128), lambda i: (i, 0)),
                  pl.BlockSpec((None, None, N_MOD, D_MODEL),
                               lambda i: (l, _group_of_tile(i), 0, 0)),
                  pl.BlockSpec((None, 4, D_MODEL), lambda i: (l, 0, 0)),
                  pl.BlockSpec(memory_space=pl.ANY)],
        out_specs=out_specs,
        out_shape=out_shape,
        scratch_shapes=[pltpu.VMEM((2, TOP_K * TM * ROW_SUB, LANES), F32),
                        pltpu.SemaphoreType.DMA((2,))],
        compiler_params=pltpu.CompilerParams(
            dimension_semantics=("arbitrary",), vmem_limit_bytes=VMEM_LIMIT),
        name="moe_combine",
    )(pos, pos, x1, route, mod, norm_g, ys)


def _route_plan(route):
    e1 = route[:, 0].astype(jnp.int32)
    e2 = route[:, 1].astype(jnp.int32)
    ar = jnp.arange(N_EXPERTS, dtype=jnp.int32)
    oh1 = (e1[:, None] == ar[None, :]).astype(jnp.int32)
    oh2 = (e2[:, None] == ar[None, :]).astype(jnp.int32)
    oh = oh1 + oh2
    csum_incl = jnp.cumsum(oh, axis=0)
    csum = csum_incl - oh
    counts = csum_incl[-1]
    ntile_e = (counts + FF_TILE - 1) // FF_TILE
    tile_end_e = jnp.cumsum(ntile_e)
    tile_start_e = tile_end_e - ntile_e
    gstart = tile_start_e * FF_TILE
    pos1 = jnp.sum((gstart[None, :] + csum) * oh1, axis=1)
    pos2 = jnp.sum((gstart[None, :] + csum) * oh2, axis=1)
    n_used = tile_end_e[-1]
    tiles = jnp.arange(NT_MOE, dtype=jnp.int32)
    eff = jnp.minimum(tiles, n_used - 1)
    te = jnp.sum((eff[:, None] >= tile_end_e[None, :]).astype(jnp.int32), axis=1)
    te = jnp.minimum(te, N_EXPERTS - 1)
    nvalid = jnp.clip(counts[te] - (eff - tile_start_e[te]) * FF_TILE, 0, FF_TILE)
    nvalid = jnp.where(tiles < n_used, nvalid, -1)
    return pos1, pos2, te.astype(jnp.int32), nvalid.astype(jnp.int32), eff.astype(jnp.int32)


def kernel(x_prompt, x_sample, cache_k, cache_v, c, c_ctx, w_ada, b_ada, norm_g, w_in, w_out,
           rpb, g_v, w_s, b_s, w_conv, w_ffn_gu, w_ffn_d, w_router, w_moe_gu, w_moe_d):
    x = (x_prompt.reshape(T_P, D_MODEL), x_sample.reshape(T_S, D_MODEL))
    cvec = jnp.concatenate([c_ctx[None], c, jnp.zeros((GROUPS - 1 - DEC_BATCH, D_MODEL), F32)],
                           axis=0)
    mod = _modulation(cvec, w_ada, b_ada)

    ck_all = cache_k.reshape(DEC_BATCH, DEPTH, PAST_LEN, D_ATT).astype(BF16)
    cv_all = cache_v.reshape(DEC_BATCH, DEPTH, PAST_LEN, D_ATT).astype(BF16)
    bias_all = _na_bias_tables(rpb)
    g_v3 = g_v.reshape(DEPTH, 1, D_CMLP)
    bs_b = jnp.repeat(jnp.swapaxes(b_s, 1, 2), HEAD_DIM, axis=2)
    wc_t = jnp.swapaxes(w_conv, 1, 2)
    w_router_t = jnp.swapaxes(w_router, 1, 2)
    w_moe_gu_all = w_moe_gu.reshape(-1, D_MODEL, 2 * D_FF)
    w_moe_d_all = w_moe_d.reshape(-1, D_FF, D_MODEL)

    dense_te = jnp.zeros((NT_DENSE,), jnp.int32)
    dense_nv = jnp.full((NT_DENSE,), FF_TILE, jnp.int32)
    dense_eff = jnp.arange(NT_DENSE, dtype=jnp.int32)

    new_k = jnp.zeros((BATCH, DEPTH, SEQ, D_ATT), F32)
    new_v = jnp.zeros((BATCH, DEPTH, SEQ, D_ATT), F32)
    for l in range(DEPTH):
        qkv, p, new_k, new_v = _inproj(x, mod, norm_g, w_in, new_k, new_v, l)
        o_att = _attention(qkv, ck_all, cv_all, bias_all, l)
        if l % 2 == 0:
            x1, h2 = _mixer_out(p, o_att, x, mod, norm_g, g_v3, w_s, bs_b, wc_t, w_out,
                                None, l, None)
            x = _ffn(h2, w_ffn_gu, w_ffn_d, dense_te + l // 2, dense_nv, dense_eff, NT_DENSE,
                     (x1, mod, norm_g, l), "ffn_dense")
        else:
            x1, h2, route = _mixer_out(p, o_att, x, mod, norm_g, g_v3, w_s, bs_b, wc_t, w_out,
                                       w_router_t, l, l // 2)
            pos1, pos2, te, nv, eff = _route_plan(route)
            xs = _dispatch(h2, pos1, pos2)
            ys = _ffn(xs, w_moe_gu_all, w_moe_d_all, te + (l // 2) * N_EXPERTS, nv, eff, NT_MOE,
                      None, "ffn_moe")
            x = _combine(x1, ys, pos1, pos2, route, mod, norm_g, l, l == DEPTH - 1)

    assert DEPTH % 2 == 0
    y_prompt = x[0].reshape(BATCH, SEQ, D_MODEL)
    y_sample = x[1].reshape(DEC_BATCH, DEC_SEQ, D_MODEL)
    cache_shape = (BATCH, DEPTH, SEQ, H_ATT, HEAD_DIM)
    return y_prompt, y_sample, new_k.reshape(cache_shape), new_v.reshape(cache_shape)
```

```python
import functools

import numpy as np
import jax
import jax.numpy as jnp
from jax import lax
from jax.experimental import pallas as pl
from jax.experimental.pallas import tpu as pltpu

F32 = jnp.float32
BF16 = jnp.bfloat16

D_MODEL = 1024
BATCH = 16
SEQ = 256
DEPTH = 4
DEC_BATCH = 4
DEC_SEQ = 2048
PAST_LEN = 256
GRID_W = 64
HEAD_DIM = 64
D_ATT = 512
D_CMLP = 256
D_CONV = 256
H_ATT = 8
HEADS_PER_STEP = 4
D_HSTEP = HEADS_PER_STEP * 64
H_CMLP = 4
CHUNK = 128
NA_KH = 8
NA_KW = 16
D_IN = 2816
D_MIX_IN = D_IN - 3 * D_ATT
D_FF = 2816
N_EXPERTS = 8
TOP_K = 2
N_MOD = 6
EPS = 1e-6
NEG_INF = -1e30

T_P = BATCH * SEQ
T_S = DEC_BATCH * DEC_SEQ
T = T_P + T_S
TM = 256
NT = T // TM
NT_P = T_P // TM
TM_IN = 512
TILES_PER_DEC = DEC_SEQ // TM
GROUPS = 8

ROWS = DEC_SEQ // GRID_W
QROWS = TM // GRID_W
BAND_TILES = 3
BAND = BAND_TILES * TM

FF_TILE = 1024
FF_SUB = 256
FF_CH = 256
N_FF_CH = D_FF // FF_CH
N_FF_STEPS = (N_FF_CH + 1) // 2
R_MOE = 2 * T + N_EXPERTS * FF_TILE
NT_MOE = R_MOE // FF_TILE
NT_DENSE = T // FF_TILE
GATHER_CH = 512
GATHER_UNROLL = 16

VMEM_LIMIT = 56 * 1024 * 1024


def _group_of_tile(i):
    return jnp.where(i < NT_P, 0, 1 + (i - NT_P) // TILES_PER_DEC)


def _rms(x, g):
    return x * lax.rsqrt(jnp.mean(x * x, axis=-1, keepdims=True) + EPS) * g


def _silu(x):
    return x / (1.0 + jnp.exp(-x))


def _gelu_tanh(x):
    c = np.float32(np.sqrt(2.0 / np.pi))
    return 0.5 * x * (1.0 + jnp.tanh(c * (x + np.float32(0.044715) * (x * x * x))))


def _mod_kernel(cv_ref, w_ref, b_ref, o_ref):
    a = _silu(cv_ref[...])
    o_ref[...] = jnp.dot(a, w_ref[...], preferred_element_type=F32,
                         precision=lax.Precision.HIGHEST) + b_ref[...]


def _modulation(cvec, w_ada, b_ada):
    tn = 1536
    nn = (N_MOD * D_MODEL) // tn
    out = pl.pallas_call(
        _mod_kernel,
        grid=(DEPTH, nn),
        in_specs=[
            pl.BlockSpec((GROUPS, D_MODEL), lambda l, n: (0, 0)),
            pl.BlockSpec((None, D_MODEL, tn), lambda l, n: (l, 0, n)),
            pl.BlockSpec((None, 1, tn), lambda l, n: (l, 0, n)),
        ],
        out_specs=pl.BlockSpec((None, GROUPS, tn), lambda l, n: (l, 0, n)),
        out_shape=jax.ShapeDtypeStruct((DEPTH, GROUPS, N_MOD * D_MODEL), F32),
        compiler_params=pltpu.CompilerParams(
            dimension_semantics=("parallel", "parallel"), vmem_limit_bytes=VMEM_LIMIT),
        name="adaln_mod",
    )(cvec, w_ada, b_ada.reshape(DEPTH, 1, N_MOD * D_MODEL))
    return out.reshape(DEPTH, GROUPS, N_MOD, D_MODEL)


def _inproj_kernel(two_inputs, *refs):
    if two_inputs:
        (xp_ref, xs_ref, mod_ref, g_ref, w_ref, kc_in_ref, vc_in_ref, qkv_ref, pm_ref, kc_ref,
         vc_ref, wb_ref) = refs
        x = jnp.where(pl.program_id(0) < T_P // TM_IN, xp_ref[...], xs_ref[...])
    else:
        (x_ref, mod_ref, g_ref, w_ref, kc_in_ref, vc_in_ref, qkv_ref, pm_ref, kc_ref,
         vc_ref, wb_ref) = refs
        x = x_ref[...]
    del kc_in_ref, vc_in_ref

    @pl.when(pl.program_id(0) == 0)
    def _():
        wb_ref[...] = w_ref[...].astype(BF16)

    h = _rms(x, g_ref[0:1, :]) * (1.0 + mod_ref[1:2, :]) + mod_ref[0:1, :]
    hb = h.astype(BF16)
    qkv = jnp.dot(hb, wb_ref[:, 0:3 * D_ATT], preferred_element_type=F32)
    pm_ref[...] = jnp.dot(hb, wb_ref[:, 3 * D_ATT:], preferred_element_type=F32)
    qkv_ref[:, 0:D_ATT] = (qkv[:, 0:D_ATT] * np.float32(HEAD_DIM ** -0.5)).astype(BF16)
    qkv_ref[:, D_ATT:] = qkv[:, D_ATT:].astype(BF16)

    @pl.when(pl.program_id(0) < T_P // TM_IN)
    def _():
        for r in range(TM_IN // SEQ):
            rows = slice(r * SEQ, (r + 1) * SEQ)
            kc_ref[r] = qkv[rows, D_ATT:2 * D_ATT]
            vc_ref[r] = qkv[rows, 2 * D_ATT:3 * D_ATT]


def _token_stream_specs(x, tile):
    if not isinstance(x, tuple):
        return [pl.BlockSpec((tile, D_MODEL), lambda i: (i, 0))], [x]
    n_p = T_P // tile
    return ([pl.BlockSpec((tile, D_MODEL), lambda i: (jnp.minimum(i, n_p - 1), 0)),
             pl.BlockSpec((tile, D_MODEL), lambda i: (jnp.maximum(i - n_p, 0), 0))], list(x))


def _inproj(x, mod, norm_g, w_in, new_k, new_v, l):
    cache_spec = pl.BlockSpec((TM_IN // SEQ, None, SEQ, D_ATT),
                              lambda i: (jnp.minimum(i, T_P // TM_IN - 1), l, 0, 0))
    x_specs, x_args = _token_stream_specs(x, TM_IN)
    n_x = len(x_args)
    return pl.pallas_call(
        functools.partial(_inproj_kernel, n_x == 2),
        grid=(T // TM_IN,),
        in_specs=x_specs + [
            pl.BlockSpec((None, None, N_MOD, D_MODEL),
                         lambda i: (l, _group_of_tile(i * (TM_IN // TM)), 0, 0)),
            pl.BlockSpec((None, 4, D_MODEL), lambda i: (l, 0, 0)),
            pl.BlockSpec((None, D_MODEL, D_IN), lambda i: (l, 0, 0),
                         pipeline_mode=pl.Buffered(1)),
            pl.BlockSpec(memory_space=pl.ANY),
            pl.BlockSpec(memory_space=pl.ANY),
        ],
        out_specs=[pl.BlockSpec((TM_IN, 3 * D_ATT), lambda i: (i, 0)),
                   pl.BlockSpec((TM_IN, D_MIX_IN), lambda i: (i, 0)), cache_spec, cache_spec],
        out_shape=[jax.ShapeDtypeStruct((T, 3 * D_ATT), BF16),
                   jax.ShapeDtypeStruct((T, D_MIX_IN), F32),
                   jax.ShapeDtypeStruct(new_k.shape, F32),
                   jax.ShapeDtypeStruct(new_v.shape, F32)],
        input_output_aliases={n_x + 3: 2, n_x + 4: 3},
        scratch_shapes=[pltpu.VMEM((D_MODEL, D_IN), BF16)],
        compiler_params=pltpu.CompilerParams(
            dimension_semantics=("arbitrary",), vmem_limit_bytes=VMEM_LIMIT),
        name="inproj",
    )(*x_args, mod, norm_g, w_in, new_k, new_v)


def _dot_nt(a, b):
    return lax.dot_general(a, b, (((1,), (1,)), ((), ())), preferred_element_type=F32)


def _attn_kernel(q_ref, k0_ref, k1_ref, k2_ref, v0_ref, v1_ref, v2_ref,
                 ck_ref, cv_ref, bias_ref, o_ref):
    is_prompt = pl.program_id(0) < NT_P

    @pl.when(is_prompt)
    def _():
        _ctx_attn_body(q_ref, k0_ref, v0_ref, o_ref)

    @pl.when(jnp.logical_not(is_prompt))
    def _():
        _na_attn_body(q_ref, k0_ref, k1_ref, k2_ref, v0_ref, v1_ref, v2_ref,
                      ck_ref, cv_ref, bias_ref, o_ref)


def _ctx_attn_body(q_ref, k_ref, v_ref, o_ref):
    for h in range(HEADS_PER_STEP):
        sl = slice(h * HEAD_DIM, (h + 1) * HEAD_DIM)
        s = _dot_nt(q_ref[:, sl], k_ref[:, sl])
        m = jnp.max(s, axis=-1, keepdims=True)
        e = jnp.exp(s - m)
        den = jnp.sum(e, axis=-1, keepdims=True)
        o = jnp.dot(e.astype(BF16), v_ref[:, sl], preferred_element_type=F32) / den
        o_ref[:, sl] = o.astype(BF16)


def _na_attn_body(q_ref, k0_ref, k1_ref, k2_ref, v0_ref, v1_ref, v2_ref,
                  ck_ref, cv_ref, bias_ref, o_ref):
    k_refs = (k0_ref, k1_ref, k2_ref)
    v_refs = (v0_ref, v1_ref, v2_ref)
    for h in range(HEADS_PER_STEP):
        sl = slice(h * HEAD_DIM, (h + 1) * HEAD_DIM)
        q = q_ref[:, sl]
        s_loc = [_dot_nt(q, k_refs[j][:, sl]) + bias_ref[h, :, j * TM:(j + 1) * TM]
                 for j in range(BAND_TILES)]
        s_ctx = _dot_nt(q, ck_ref[:, sl])
        m = jnp.max(jnp.maximum(jnp.maximum(s_loc[0], s_loc[1]), jnp.maximum(s_loc[2], s_ctx)),
                    axis=-1, keepdims=True)
        es = [jnp.exp(s - m) for s in s_loc + [s_ctx]]
        den = jnp.sum((es[0] + es[1]) + (es[2] + es[3]), axis=-1, keepdims=True)
        e_all = jnp.concatenate([e.astype(BF16) for e in es], axis=1)
        v_all = jnp.concatenate([v_ref[:, sl] for v_ref in v_refs] + [cv_ref[:, sl]], axis=0)
        acc = jnp.dot(e_all, v_all, preferred_element_type=F32)
        o_ref[:, sl] = (acc / den).astype(BF16)


def _na_variant_tables():
    kh = min(NA_KH, ROWS)
    per_tile = []
    for rb in range(ROWS // QROWS):
        r0 = rb * QROWS
        bs = int(np.clip(rb - 1, 0, ROWS // QROWS - BAND_TILES)) * QROWS
        tab = -np.ones((QROWS, BAND_TILES * QROWS), np.int32)
        for qr in range(QROWS):
            r = r0 + qr
            rs = int(np.clip(r - kh // 2, 0, ROWS - kh))
            for kr in range(BAND_TILES * QROWS):
                ka = bs + kr
                if rs <= ka < rs + kh:
                    tab[qr, kr] = ka - r + (NA_KH - 1)
            assert (tab[qr] >= 0).sum() == kh
        per_tile.append(tab)
    variants, variant_of_tile = [], []
    for tab in per_tile:
        for vi, v in enumerate(variants):
            if np.array_equal(v, tab):
                variant_of_tile.append(vi)
                break
        else:
            variants.append(tab)
            variant_of_tile.append(len(variants) - 1)
    return np.asarray(variant_of_tile, np.int32), np.stack(variants)


_NA_VARIANT_OF_TILE, _NA_DR_IDX = _na_variant_tables()
_NA_NVAR = _NA_DR_IDX.shape[0]
_N_DR = 2 * NA_KH - 1
_N_DC = 2 * NA_KW - 1


def _bias_kernel(w_ref, o_ref):
    qc = lax.broadcasted_iota(jnp.int32, (GRID_W, LANES), 0)
    lane = lax.broadcasted_iota(jnp.int32, (GRID_W, LANES), 1)
    kc = lane % GRID_W
    cs = jnp.clip(qc - NA_KW // 2, 0, GRID_W - NA_KW)
    col_ok = jnp.logical_and(kc >= cs, kc < cs + NA_KW)
    left = lane < GRID_W
    neg = jnp.full((GRID_W, LANES), NEG_INF, F32)
    cache = {}
    for vi in range(_NA_NVAR):
        for qr in range(QROWS):
            for pp in range(BAND_TILES * QROWS // 2):
                d0 = int(_NA_DR_IDX[vi, qr, 2 * pp])
                d1 = int(_NA_DR_IDX[vi, qr, 2 * pp + 1])
                if (d0, d1) not in cache:
                    if d0 < 0 and d1 < 0:
                        tile = neg
                    else:
                        u = (w_ref[pl.ds(d0 if d0 >= 0 else _N_DR, 1), :]
                             + w_ref[pl.ds(_N_DR + 1 + (d1 if d1 >= 0 else _N_DR), 1), :])
                        t = pltpu.roll(jnp.broadcast_to(u, (GRID_W, LANES)), 0, 1,
                                       stride=1, stride_axis=0)
                        ok = col_ok
                        if d0 < 0:
                            ok = jnp.logical_and(ok, jnp.logical_not(left))
                        if d1 < 0:
                            ok = jnp.logical_and(ok, left)
                        tile = jnp.where(ok, t, neg)
                    cache[(d0, d1)] = tile
                o_ref[vi, qr * GRID_W:(qr + 1) * GRID_W, pp * LANES:(pp + 1) * LANES] = cache[(d0, d1)]


def _na_bias_tables(rpb):
    half = NA_KW - 1
    zeros = jnp.zeros((DEPTH, H_ATT, _N_DR, LANES - _N_DC), F32)
    w_lo = jnp.concatenate([rpb[..., half:], zeros, rpb[..., :half]], axis=-1)
    w_lo = jnp.pad(w_lo, ((0, 0), (0, 0), (0, 1), (0, 0)))
    w_hi = jnp.roll(w_lo, GRID_W, axis=-1)
    w = jnp.concatenate([w_lo, w_hi], axis=2)
    return pl.pallas_call(
        _bias_kernel,
        grid=(DEPTH, H_ATT),
        in_specs=[pl.BlockSpec((None, None, 2 * (_N_DR + 1), LANES), lambda l, h: (l, h, 0, 0))],
        out_specs=pl.BlockSpec((None, _NA_NVAR, None, TM, BAND), lambda l, h: (l, 0, h, 0, 0)),
        out_shape=jax.ShapeDtypeStruct((DEPTH, _NA_NVAR, H_ATT, TM, BAND), F32),
        compiler_params=pltpu.CompilerParams(
            dimension_semantics=("parallel", "parallel"), vmem_limit_bytes=VMEM_LIMIT),
        name="na_bias",
    )(w)


def _attention(p, ck_all, cv_all, bias_all, l):
    nrb = TILES_PER_DEC
    var_of_tile = [int(v) for v in _NA_VARIANT_OF_TILE]

    def dec_batch(i):
        return jnp.maximum(i - NT_P, 0) // nrb

    def band_tile(i, j):
        rb = (i - NT_P) % nrb
        first = NT_P + dec_batch(i) * nrb + jnp.clip(rb - 1, 0, nrb - BAND_TILES)
        return jnp.where(i < NT_P, i, first + j)

    def variant(i):
        rb = jnp.maximum(i - NT_P, 0) % nrb
        v = jnp.int32(var_of_tile[0])
        for t in range(1, nrb):
            v = jnp.where(rb >= t, jnp.int32(var_of_tile[t]), v)
        return v

    n_hg = H_ATT // HEADS_PER_STEP
    kv_specs = [pl.BlockSpec((TM, D_HSTEP), functools.partial(
        lambda i, g, j, col: (band_tile(i, j), col * n_hg + g), j=j, col=col))
        for col in (1, 2) for j in range(BAND_TILES)]
    return pl.pallas_call(
        _attn_kernel,
        grid=(NT, n_hg),
        in_specs=[pl.BlockSpec((TM, D_HSTEP), lambda i, g: (i, g))] + kv_specs + [
            pl.BlockSpec((None, None, PAST_LEN, D_HSTEP), lambda i, g: (dec_batch(i), l, 0, g)),
            pl.BlockSpec((None, None, PAST_LEN, D_HSTEP), lambda i, g: (dec_batch(i), l, 0, g)),
            pl.BlockSpec((None, None, HEADS_PER_STEP, TM, BAND),
                         lambda i, g: (l, variant(i), g, 0, 0)),
        ],
        out_specs=pl.BlockSpec((TM, D_HSTEP), lambda i, g: (i, g)),
        out_shape=jax.ShapeDtypeStruct((T, D_ATT), BF16),
        compiler_params=pltpu.CompilerParams(
            dimension_semantics=("arbitrary", "arbitrary"), vmem_limit_bytes=VMEM_LIMIT),
        name="attn",
    )(p, p, p, p, p, p, p, ck_all, cv_all, bias_all)


def _mixer_out_kernel(with_router, n_x, *refs):
    x_refs, refs = refs[4:4 + n_x], refs[:4] + refs[4 + n_x:]
    if with_router:
        (pm_ref, halo_prev_ref, halo_next_ref, oatt_ref, mod_ref, g_ref, gv_ref, ws_ref, bs_ref,
         wc_ref, wo_ref, wr_ref, x1_ref, h2_ref, route_ref, wob_ref) = refs
    else:
        (pm_ref, halo_prev_ref, halo_next_ref, oatt_ref, mod_ref, g_ref, gv_ref, ws_ref, bs_ref,
         wc_ref, wo_ref, x1_ref, h2_ref, wob_ref) = refs
    i = pl.program_id(0)

    def mix_col(ref, rows, k):
        return ref[rows, k * D_CMLP:(k + 1) * D_CMLP]

    all_rows = slice(None)
    if n_x == 2:
        x = jnp.where(i < NT_P, x_refs[0][...], x_refs[1][...])
    else:
        x = x_refs[0][...]

    @pl.when(i == 0)
    def _():
        wob_ref[...] = wo_ref[...].astype(BF16)

    u = _gelu_tanh(mix_col(pm_ref, all_rows, 0))
    vm = _rms(_gelu_tanh(mix_col(pm_ref, all_rows, 1)), gv_ref[...]).astype(BF16)
    chunks = []
    for c in range(TM // CHUNK):
        rows = slice(c * CHUNK, (c + 1) * CHUNK)
        heads = [jnp.dot(ws_ref[h].astype(BF16), vm[rows, h * HEAD_DIM:(h + 1) * HEAD_DIM],
                         preferred_element_type=F32) for h in range(H_CMLP)]
        chunks.append(jnp.concatenate(heads, axis=1) + bs_ref[...])
    o_mlp = u * jnp.concatenate(chunks, axis=0)

    j = (i - NT_P) % TILES_PER_DEC
    has_prev = jnp.logical_and(i >= NT_P, j > 0)
    has_next = jnp.logical_and(i >= NT_P, j < TILES_PER_DEC - 1)
    z = mix_col(pm_ref, all_rows, 3) * mix_col(pm_ref, all_rows, 4)
    last_row, first_row = slice(7, 8), slice(0, 1)
    z_halo_prev = jnp.where(
        has_prev, mix_col(halo_prev_ref, last_row, 3) * mix_col(halo_prev_ref, last_row, 4), 0.0)
    z_halo_next = jnp.where(
        has_next, mix_col(halo_next_ref, first_row, 3) * mix_col(halo_next_ref, first_row, 4), 0.0)
    row = lax.broadcasted_iota(jnp.int32, (TM, D_CONV), 0)
    z_prev = jnp.where(row == 0, z_halo_prev, pltpu.roll(z, 1, 0))
    z_next = jnp.where(row == TM - 1, z_halo_next, pltpu.roll(z, TM - 1, 0))
    y_conv = z_prev * wc_ref[0:1, :] + z * wc_ref[1:2, :] + z_next * wc_ref[2:3, :]
    o_conv = mix_col(pm_ref, all_rows, 2) * y_conv

    y = jnp.dot(oatt_ref[...], wob_ref[0:D_ATT, :], preferred_element_type=F32)
    y = y + jnp.dot(o_mlp.astype(BF16), wob_ref[D_ATT:D_ATT + D_CMLP, :],
                    preferred_element_type=F32)
    y = y + jnp.dot(o_conv.astype(BF16), wob_ref[D_ATT + D_CMLP:, :],
                    preferred_element_type=F32)

    x1 = x + mod_ref[2:3, :] * _rms(y, g_ref[1:2, :])
    h2 = _rms(x1, g_ref[2:3, :]) * (1.0 + mod_ref[4:5, :]) + mod_ref[3:4, :]
    x1_ref[...] = x1
    if with_router:
        _store_row_tiled(h2_ref, h2, TM)
    else:
        h2_ref[...] = h2

    if with_router:
        lg = [jnp.sum(h2 * wr_ref[e:e + 1, :], axis=-1, keepdims=True) for e in range(N_EXPERTS)]

        def top1(cols):
            m = cols[0]
            for col in cols[1:]:
                m = jnp.maximum(m, col)
            idx = jnp.full_like(m, N_EXPERTS - 1)
            for e in range(N_EXPERTS - 2, -1, -1):
                idx = jnp.where(cols[e] == m, np.float32(e), idx)
            return m, idx

        m1, i1 = top1(lg)
        m2, i2 = top1([jnp.where(i1 == np.float32(e), -jnp.inf, lg[e]) for e in range(N_EXPERTS)])
        lane = lax.broadcasted_iota(jnp.int32, (TM, 128), 1)
        e2 = jnp.exp(m2 - m1)
        den = 1.0 + e2
        gate1 = 1.0 / den
        gate2 = e2 / den
        route = jnp.where(lane == 0, i1,
                          jnp.where(lane == 1, i2,
                                    jnp.where(lane == 2, gate1,
                                              jnp.where(lane == 3, gate2, 0.0))))
        route_ref[...] = route


def _mixer_out(p, o_att, x, mod, norm_g, g_v, w_s, bs_b, wc_t, w_out, w_router_t, l, moe_idx):
    with_router = moe_idx is not None
    hb = TM // 8

    x_specs, x_args = _token_stream_specs(x, TM)
    in_specs = [pl.BlockSpec((TM, D_MIX_IN), lambda i: (i, 0)),
                pl.BlockSpec((8, D_MIX_IN), lambda i: (jnp.maximum(i * hb - 1, 0), 0)),
                pl.BlockSpec((8, D_MIX_IN), lambda i: (jnp.minimum((i + 1) * hb, T // 8 - 1), 0)),
                pl.BlockSpec((TM, D_ATT), lambda i: (i, 0))] + x_specs + [
                pl.BlockSpec((None, None, N_MOD, D_MODEL), lambda i: (l, _group_of_tile(i), 0, 0)),
                pl.BlockSpec((None, 4, D_MODEL), lambda i: (l, 0, 0)),
                pl.BlockSpec((None, 1, D_CMLP), lambda i: (l, 0, 0)),
                pl.BlockSpec((None, H_CMLP, CHUNK, CHUNK), lambda i: (l, 0, 0, 0)),
                pl.BlockSpec((None, CHUNK, D_CMLP), lambda i: (l, 0, 0)),
                pl.BlockSpec((None, 3, D_CONV), lambda i: (l, 0, 0)),
                pl.BlockSpec((None, D_MODEL, D_MODEL), lambda i: (l, 0, 0),
                             pipeline_mode=pl.Buffered(1))]
    args = [p, p, p, o_att] + x_args + [mod, norm_g, g_v, w_s, bs_b, wc_t, w_out]
    out_specs = [pl.BlockSpec((TM, D_MODEL), lambda i: (i, 0)),
                 pl.BlockSpec((TM, D_MODEL), lambda i: (i, 0))]
    out_shape = [jax.ShapeDtypeStruct((T, D_MODEL), F32), jax.ShapeDtypeStruct((T, D_MODEL), F32)]
    if with_router:
        in_specs.append(pl.BlockSpec((None, N_EXPERTS, D_MODEL), lambda i: (moe_idx, 0, 0)))
        args.append(w_router_t)
        out_specs[1] = pl.BlockSpec((TM * ROW_SUB, LANES), lambda i: (i, 0))
        out_shape[1] = jax.ShapeDtypeStruct((T * ROW_SUB, LANES), F32)
        out_specs.append(pl.BlockSpec((TM, 128), lambda i: (i, 0)))
        out_shape.append(jax.ShapeDtypeStruct((T, 128), F32))
    return pl.pallas_call(
        functools.partial(_mixer_out_kernel, with_router, len(x_args)),
        grid=(NT,),
        in_specs=in_specs,
        out_specs=out_specs,
        out_shape=out_shape,
        scratch_shapes=[pltpu.VMEM((D_MODEL, D_MODEL), BF16)],
        compiler_params=pltpu.CompilerParams(
            dimension_semantics=("arbitrary",), vmem_limit_bytes=VMEM_LIMIT),
        name="mixer_out_router" if with_router else "mixer_out",
    )(*args)


LANES = 128
ROW_SUB = D_MODEL // LANES


def _load_row_tiled(ref, n):
    return jnp.concatenate([ref[pl.ds(c, n, stride=ROW_SUB), :] for c in range(ROW_SUB)], axis=1)


def _store_row_tiled(ref, val, n):
    for c in range(ROW_SUB):
        ref[pl.ds(c, n, stride=ROW_SUB), :] = val[:, c * LANES:(c + 1) * LANES]


def _ffn_kernel(row_tiled, te_ref, nv_ref, et_ref, x_ref, wga_ref, wua_ref, wda_ref,
                wgb_ref, wub_ref, wdb_ref, *rest):
    if row_tiled:
        o_ref, xb_ref, acc_ref = rest
    else:
        x1_ref, mod_ref, g_ref, o_ref, xb_ref, acc_ref = rest
    del te_ref, et_ref
    t = pl.program_id(0)
    j = pl.program_id(1)
    nvalid = nv_ref[t]
    nsub = (nvalid + (FF_SUB - 1)) // FF_SUB
    first = j == 0
    last = j == N_FF_STEPS - 1

    @pl.when(jnp.logical_and(first, nsub == 0))
    def _():
        acc_ref[...] = jnp.zeros((FF_TILE, D_MODEL), F32)

    def swiglu_rows(n, chunks, is_first):
        if is_first:
            if row_tiled:
                xb_ref[0:n, :] = _load_row_tiled(x_ref, n).astype(BF16)
            else:
                xb_ref[0:n, :] = x_ref[0:n, :].astype(BF16)
        xs = xb_ref[0:n, :]
        acts, wds = [], []
        for wg_ref, wu_ref, wd_ref in chunks:
            g = jnp.dot(xs, wg_ref[...].astype(BF16), preferred_element_type=F32)
            u = jnp.dot(xs, wu_ref[...].astype(BF16), preferred_element_type=F32)
            acts.append((_silu(g) * u).astype(BF16))
            wds.append(wd_ref[...].astype(BF16))
        a = acts[0] if len(acts) == 1 else jnp.concatenate(acts, axis=1)
        wd = wds[0] if len(wds) == 1 else jnp.concatenate(wds, axis=0)
        part = jnp.dot(a, wd, preferred_element_type=F32)
        if is_first:
            acc_ref[0:n, :] = part
            if n < FF_TILE:
                acc_ref[n:, :] = jnp.zeros((FF_TILE - n, D_MODEL), F32)
        else:
            acc_ref[0:n, :] += part

    chunk_a = (wga_ref, wua_ref, wda_ref)
    chunk_b = (wgb_ref, wub_ref, wdb_ref)
    middle = jnp.logical_not(jnp.logical_or(first, last))
    n_options = range(1, FF_TILE // FF_SUB + 1) if row_tiled else (FF_TILE // FF_SUB,)
    for k in n_options:
        rows_ok = (nsub == k) if row_tiled else True
        pl.when(jnp.logical_and(rows_ok, first))(
            functools.partial(swiglu_rows, k * FF_SUB, (chunk_a, chunk_b), True))
        pl.when(jnp.logical_and(rows_ok, middle))(
            functools.partial(swiglu_rows, k * FF_SUB, (chunk_a, chunk_b), False))
        pl.when(jnp.logical_and(rows_ok, last))(
            functools.partial(swiglu_rows, k * FF_SUB, (chunk_a,), False))

    @pl.when(last)
    def _():
        if row_tiled:
            _store_row_tiled(o_ref, acc_ref[...], FF_TILE)
        else:
            o_ref[...] = x1_ref[...] + mod_ref[5:6, :] * _rms(acc_ref[...], g_ref[3:4, :])


def _ffn(x_rows, w_gu, w_d, tile_expert, tile_nvalid, tile_eff, n_tiles, residual, name):
    def chunk(t, j, nv, second):
        c = jnp.minimum(2 * j + second, N_FF_CH - 1)
        return jnp.where(nv[t] >= 0, c, N_FF_CH - 1)

    def weight_specs(second):
        return [
            pl.BlockSpec((None, D_MODEL, FF_CH),
                         lambda t, j, te, nv, et: (te[t], 0, chunk(t, j, nv, second))),
            pl.BlockSpec((None, D_MODEL, FF_CH),
                         lambda t, j, te, nv, et: (te[t], 0, N_FF_CH + chunk(t, j, nv, second))),
            pl.BlockSpec((None, FF_CH, D_MODEL),
                         lambda t, j, te, nv, et: (te[t], chunk(t, j, nv, second), 0)),
        ]

    row_tiled = residual is None
    if row_tiled:
        x_block = (FF_TILE * ROW_SUB, LANES)
        out_shape = jax.ShapeDtypeStruct((n_tiles * FF_TILE * ROW_SUB, LANES), F32)
        extra_specs, extra_args = [], ()
    else:
        x1, mod, norm_g, l = residual
        x_block = (FF_TILE, D_MODEL)
        out_shape = jax.ShapeDtypeStruct((n_tiles * FF_TILE, D_MODEL), F32)
        tiles_per_tm = FF_TILE // TM
        extra_specs = [
            pl.BlockSpec(x_block, lambda t, j, te, nv, et: (t, 0)),
            pl.BlockSpec((None, None, N_MOD, D_MODEL),
                         lambda t, j, te, nv, et: (l, _group_of_tile(t * tiles_per_tm), 0, 0)),
            pl.BlockSpec((None, 4, D_MODEL), lambda t, j, te, nv, et: (l, 0, 0)),
        ]
        extra_args = (x1, mod, norm_g)
    grid_spec = pltpu.PrefetchScalarGridSpec(
        num_scalar_prefetch=3,
        grid=(n_tiles, N_FF_STEPS),
        in_specs=[pl.BlockSpec(x_block, lambda t, j, te, nv, et: (et[t], 0))]
        + weight_specs(0) + weight_specs(1) + extra_specs,
        out_specs=pl.BlockSpec(x_block, lambda t, j, te, nv, et: (t, 0)),
        scratch_shapes=[pltpu.VMEM((FF_TILE, D_MODEL), BF16),
                        pltpu.VMEM((FF_TILE, D_MODEL), F32)],
    )
    return pl.pallas_call(
        functools.partial(_ffn_kernel, row_tiled),
        grid_spec=grid_spec,
        out_shape=out_shape,
        compiler_params=pltpu.CompilerParams(
            dimension_semantics=("arbitrary", "arbitrary"), vmem_limit_bytes=VMEM_LIMIT),
        name=name,
    )(tile_expert, tile_nvalid, tile_eff, x_rows, w_gu, w_gu, w_d, w_gu, w_gu, w_d, *extra_args)


def _dispatch_kernel(pos_ref, h_ref, xs_in_ref, xs_ref, sem):
    del xs_in_ref

    def row_copy(r, k):
        dst_row = pl.multiple_of(pos_ref[0, k * GATHER_CH + r] * ROW_SUB, ROW_SUB)
        src_row = pl.multiple_of(r * ROW_SUB, ROW_SUB)
        return pltpu.make_async_copy(h_ref.at[pl.ds(src_row, ROW_SUB)],
                                     xs_ref.at[pl.ds(dst_row, ROW_SUB)], sem)

    def start_group(g, carry):
        for u in range(GATHER_UNROLL):
            r = g * GATHER_UNROLL + u
            row_copy(r, 0).start(priority=0)
            row_copy(r, 1).start(priority=1)
        return carry

    lax.fori_loop(0, GATHER_CH // GATHER_UNROLL, start_group, 0)
    for _ in range(TOP_K):
        pltpu.make_async_copy(h_ref, xs_ref.at[pl.ds(0, GATHER_CH * ROW_SUB)], sem).wait()


def _dispatch(h2r, pos1, pos2):
    steps = T // GATHER_CH
    pos = jnp.concatenate([pos1.reshape(steps, 1, GATHER_CH), pos2.reshape(steps, 1, GATHER_CH)],
                          axis=2)
    return pl.pallas_call(
        _dispatch_kernel,
        grid=(steps,),
        in_specs=[pl.BlockSpec((None, 1, TOP_K * GATHER_CH), lambda i: (i, 0, 0),
                               memory_space=pltpu.SMEM),
                  pl.BlockSpec((GATHER_CH * ROW_SUB, LANES), lambda i: (i, 0)),
                  pl.BlockSpec(memory_space=pl.ANY)],
        out_specs=pl.BlockSpec(memory_space=pl.ANY),
        out_shape=jax.ShapeDtypeStruct((R_MOE * ROW_SUB, LANES), F32),
        input_output_aliases={2: 0},
        scratch_shapes=[pltpu.SemaphoreType.DMA(())],
        compiler_params=pltpu.CompilerParams(
            dimension_semantics=("arbitrary",), vmem_limit_bytes=VMEM_LIMIT),
        name="moe_dispatch",
    )(pos, h2r, jnp.zeros((R_MOE * ROW_SUB, LANES), F32))


def _combine_kernel(split_out, pos_ref, pos_next_ref, x1_ref, route_ref, mod_ref, g_ref, ys_ref,
                    *rest):
    if split_out:
        op_ref, os_ref, ybuf_ref, sems = rest
    else:
        o_ref, ybuf_ref, sems = rest
    i = pl.program_id(0)
    slot = i % 2
    n_rows = TOP_K * TM

    def row_copy(p_ref, s, r):
        src_row = pl.multiple_of(p_ref[0, r] * ROW_SUB, ROW_SUB)
        dst_row = pl.multiple_of(r * ROW_SUB, ROW_SUB)
        return pltpu.make_async_copy(ys_ref.at[pl.ds(src_row, ROW_SUB)],
                                     ybuf_ref.at[s, pl.ds(dst_row, ROW_SUB)], sems.at[s])

    def combine_rows():
        pltpu.make_async_copy(ys_ref.at[pl.ds(0, n_rows * ROW_SUB)], ybuf_ref.at[slot],
                              sems.at[slot]).wait()
        ybuf = ybuf_ref.at[slot]
        y1 = jnp.concatenate([ybuf[pl.ds(c, TM, stride=ROW_SUB), :] for c in range(ROW_SUB)],
                             axis=1)
        y2 = jnp.concatenate([ybuf[pl.ds(TM * ROW_SUB + c, TM, stride=ROW_SUB), :]
                              for c in range(ROW_SUB)], axis=1)
        f = route_ref[:, 2:3] * y1 + route_ref[:, 3:4] * y2
        out = x1_ref[...] + mod_ref[5:6, :] * _rms(f, g_ref[3:4, :])
        if split_out:
            @pl.when(i < NT_P)
            def _():
                op_ref[...] = out

            @pl.when(i >= NT_P)
            def _():
                os_ref[...] = out
        else:
            o_ref[...] = out

    @pl.when(i == 0)
    def _():
        def start_group(g, carry):
            for u in range(GATHER_UNROLL):
                row_copy(pos_ref, 0, g * GATHER_UNROLL + u).start(priority=u % 2)
            return carry

        lax.fori_loop(0, n_rows // GATHER_UNROLL, start_group, 0)

    @pl.when(i + 1 < pl.num_programs(0))
    def _():
        for r in range(n_rows):
            row_copy(pos_next_ref, 1 - slot, r).start(priority=r % 2)
        combine_rows()

    @pl.when(i + 1 >= pl.num_programs(0))
    def _():
        combine_rows()


def _combine(x1, ys, pos1, pos2, route, mod, norm_g, l, split_out):
    pos = jnp.concatenate([pos1.reshape(NT, 1, TM), pos2.reshape(NT, 1, TM)], axis=2)
    row = pl.BlockSpec((TM, D_MODEL), lambda i: (i, 0))
    pos_block = (None, 1, TOP_K * TM)
    if split_out:
        out_specs = [pl.BlockSpec((TM, D_MODEL), lambda i: (jnp.minimum(i, NT_P - 1), 0)),
                     pl.BlockSpec((TM, D_MODEL), lambda i: (jnp.maximum(i - NT_P, 0), 0))]
        out_shape = [jax.ShapeDtypeStruct((T_P, D_MODEL), F32),
                     jax.ShapeDtypeStruct((T_S, D_MODEL), F32)]
    else:
        out_specs, out_shape = row, jax.ShapeDtypeStruct((T, D_MODEL), F32)
    return pl.pallas_call(
        functools.partial(_combine_kernel, split_out),
        grid=(NT,),
        in_specs=[pl.BlockSpec(pos_block, lambda i: (i, 0, 0), memory_space=pltpu.SMEM),
                  pl.BlockSpec(pos_block, lambda i: (jnp.minimum(i + 1, NT - 1), 0, 0),
                               memory_space=pltpu.SMEM),
                  row,
                  pl.BlockSpec((TM, 128), lambda i: (i, 0)),
                  pl.BlockSpec((None, None, N_MOD, D_MODEL),
                               lambda i: (l, _group_of_tile(i), 0, 0)),
                  pl.BlockSpec((None, 4, D_MODEL), lambda i: (l, 0, 0)),
                  pl.BlockSpec(memory_space=pl.ANY)],
        out_specs=out_specs,
        out_shape=out_shape,
        scratch_shapes=[pltpu.VMEM((2, TOP_K * TM * ROW_SUB, LANES), F32),
                        pltpu.SemaphoreType.DMA((2,))],
        compiler_params=pltpu.CompilerParams(
            dimension_semantics=("arbitrary",), vmem_limit_bytes=VMEM_LIMIT),
        name="moe_combine",
    )(pos, pos, x1, route, mod, norm_g, ys)


def _route_plan(route):
    e1 = route[:, 0].astype(jnp.int32)
    e2 = route[:, 1].astype(jnp.int32)
    ar = jnp.arange(N_EXPERTS, dtype=jnp.int32)
    oh1 = (e1[:, None] == ar[None, :]).astype(jnp.int32)
    oh2 = (e2[:, None] == ar[None, :]).astype(jnp.int32)
    oh = oh1 + oh2
    csum_incl = jnp.cumsum(oh, axis=0)
    csum = csum_incl - oh
    counts = csum_incl[-1]
    ntile_e = (counts + FF_TILE - 1) // FF_TILE
    tile_end_e = jnp.cumsum(ntile_e)
    tile_start_e = tile_end_e - ntile_e
    gstart = tile_start_e * FF_TILE
    pos1 = jnp.sum((gstart[None, :] + csum) * oh1, axis=1)
    pos2 = jnp.sum((gstart[None, :] + csum) * oh2, axis=1)
    n_used = tile_end_e[-1]
    tiles = jnp.arange(NT_MOE, dtype=jnp.int32)
    eff = jnp.minimum(tiles, n_used - 1)
    te = jnp.sum((eff[:, None] >= tile_end_e[None, :]).astype(jnp.int32), axis=1)
    te = jnp.minimum(te, N_EXPERTS - 1)
    nvalid = jnp.clip(counts[te] - (eff - tile_start_e[te]) * FF_TILE, 0, FF_TILE)
    nvalid = jnp.where(tiles < n_used, nvalid, -1)
    return pos1, pos2, te.astype(jnp.int32), nvalid.astype(jnp.int32), eff.astype(jnp.int32)


def kernel(x_prompt, x_sample, cache_k, cache_v, c, c_ctx, w_ada, b_ada, norm_g, w_in, w_out,
           rpb, g_v, w_s, b_s, w_conv, w_ffn_gu, w_ffn_d, w_router, w_moe_gu, w_moe_d):
    x = (x_prompt.reshape(T_P, D_MODEL), x_sample.reshape(T_S, D_MODEL))
    cvec = jnp.concatenate([c_ctx[None], c, jnp.zeros((GROUPS - 1 - DEC_BATCH, D_MODEL), F32)],
                           axis=0)
    mod = _modulation(cvec, w_ada, b_ada)

    ck_all = cache_k.reshape(DEC_BATCH, DEPTH, PAST_LEN, D_ATT).astype(BF16)
    cv_all = cache_v.reshape(DEC_BATCH, DEPTH, PAST_LEN, D_ATT).astype(BF16)
    bias_all = _na_bias_tables(rpb)
    g_v3 = g_v.reshape(DEPTH, 1, D_CMLP)
    bs_b = jnp.repeat(jnp.swapaxes(b_s, 1, 2), HEAD_DIM, axis=2)
    wc_t = jnp.swapaxes(w_conv, 1, 2)
    w_router_t = jnp.swapaxes(w_router, 1, 2)
    w_moe_gu_all = w_moe_gu.reshape(-1, D_MODEL, 2 * D_FF)
    w_moe_d_all = w_moe_d.reshape(-1, D_FF, D_MODEL)

    dense_te = jnp.zeros((NT_DENSE,), jnp.int32)
    dense_nv = jnp.full((NT_DENSE,), FF_TILE, jnp.int32)
    dense_eff = jnp.arange(NT_DENSE, dtype=jnp.int32)

    new_k = jnp.zeros((BATCH, DEPTH, SEQ, D_ATT), F32)
    new_v = jnp.zeros((BATCH, DEPTH, SEQ, D_ATT), F32)
    for l in range(DEPTH):
        qkv, p, new_k, new_v = _inproj(x, mod, norm_g, w_in, new_k, new_v, l)
        o_att = _attention(qkv, ck_all, cv_all, bias_all, l)
        if l % 2 == 0:
            x1, h2 = _mixer_out(p, o_att, x, mod, norm_g, g_v3, w_s, bs_b, wc_t, w_out,
                                None, l, None)
            x = _ffn(h2, w_ffn_gu, w_ffn_d, dense_te + l // 2, dense_nv, dense_eff, NT_DENSE,
                     (x1, mod, norm_g, l), "ffn_dense")
        else:
            x1, h2, route = _mixer_out(p, o_att, x, mod, norm_g, g_v3, w_s, bs_b, wc_t, w_out,
                                       w_router_t, l, l // 2)
            pos1, pos2, te, nv, eff = _route_plan(route)
            xs = _dispatch(h2, pos1, pos2)
            ys = _ffn(xs, w_moe_gu_all, w_moe_d_all, te + (l // 2) * N_EXPERTS, nv, eff, NT_MOE,
                      None, "ffn_moe")
            x = _combine(x1, ys, pos1, pos2, route, mod, norm_g, l, l == DEPTH - 1)

    assert DEPTH % 2 == 0
    y_prompt = x[0].reshape(BATCH, SEQ, D_MODEL)
    y_sample = x[1].reshape(DEC_BATCH, DEC_SEQ, D_MODEL)
    cache_shape = (BATCH, DEPTH, SEQ, H_ATT, HEAD_DIM)
    return y_prompt, y_sample, new_k.reshape(cache_shape), new_v.reshape(cache_shape)
```
